```python
import math
import jax, jax.numpy as jnp
from jax import lax
import numpy as np

D_MODEL = 2048
BATCH = 4
SEQ = 2048
DEPTH = 1
DEC_BATCH = 16
DEC_SEQ = 32
PAST_LEN = 2048

CHUNK = 64
HEAD_DIM = 128
N_HEADS_GDN = 8
N_HEADS_SB = 8
GDN_WIDTH = N_HEADS_GDN * HEAD_DIM
SB_WIDTH = N_HEADS_SB * HEAD_DIM
MIX_WIDTH = GDN_WIDTH + SB_WIDTH
CONV_W = 4
D_FF = 4 * D_MODEL
SB_BLOCK = 128
DEEPNORM_ALPHA = (2 * DEPTH) ** 0.25
DEEPNORM_BETA = (8 * DEPTH) ** -0.25
LN_EPS = 1e-5
RMS_EPS = 1e-6
L2_EPS = 1e-6

OFF_GDN_Z = 3 * GDN_WIDTH
OFF_GDN_B = 4 * GDN_WIDTH
OFF_GDN_A = OFF_GDN_B + N_HEADS_GDN
OFF_SB = OFF_GDN_A + N_HEADS_GDN
PROJ_WIDTH = OFF_SB + 3 * SB_WIDTH

kernel_name = "hybrid_gdn_stickbreaking_stream_step"


def layer_norm(x, g, b):
    xf = x.astype(jnp.float32)
    mu = jnp.mean(xf, axis=-1, keepdims=True)
    var = jnp.mean(jnp.square(xf - mu), axis=-1, keepdims=True)
    y = (xf - mu) * lax.rsqrt(var + LN_EPS) * g.astype(jnp.float32) + b.astype(jnp.float32)
    return y.astype(x.dtype)


def l2_normalize(t):
    return t * lax.rsqrt(jnp.sum(jnp.square(t), axis=-1, keepdims=True) + L2_EPS)


def causal_short_conv(u, buf, w):
    T = u.shape[1]
    up = jnp.concatenate([buf.astype(u.dtype), u], axis=1)
    out = up[:, 0:T] * w[0]
    for i in range(1, CONV_W):
        out = out + up[:, i:i + T] * w[i]
    return jax.nn.silu(out), up[:, T:]


def gated_delta_rule(q, k, v, beta, g, S0):
    B, T, H, dk = q.shape
    dv = v.shape[-1]
    C = CHUNK if T % CHUNK == 0 else T
    N = T // C

    def blocks(t):
        t = t.reshape((B, N, C, H) + t.shape[3:])
        return jnp.moveaxis(t, 3, 1)

    q, k, v, beta, g = blocks(q), blocks(k), blocks(v), blocks(beta), blocks(g)
    g_cum = jnp.cumsum(g, axis=-1)
    idx = jnp.arange(C)
    incl = idx[:, None] >= idx[None, :]
    strict = idx[:, None] > idx[None, :]
    decay = jnp.exp(jnp.where(incl, g_cum[..., :, None] - g_cum[..., None, :], -jnp.inf))
    k_beta = k * beta[..., None]
    m = jnp.where(strict, jnp.einsum('bhnid,bhnjd->bhnij', k_beta, k) * decay, 0.0)
    eye = jnp.eye(C, dtype=m.dtype)
    t_inv = lax.linalg.triangular_solve(eye + m, jnp.broadcast_to(eye, m.shape),
                                        left_side=True, lower=True, unit_diagonal=True)
    u = jnp.einsum('bhnij,bhnjd->bhnid', t_inv, v * beta[..., None])
    w = jnp.einsum('bhnij,bhnjd->bhnid', t_inv, k_beta * jnp.exp(g_cum)[..., None])
    attn = jnp.einsum('bhnid,bhnjd->bhnij', q, k) * decay
    q_dec = q * jnp.exp(g_cum)[..., None]
    k_tail = k * jnp.exp(g_cum[..., -1:] - g_cum)[..., None]
    g_tot = jnp.exp(g_cum[..., -1])
    xs = tuple(jnp.moveaxis(t, 2, 0) for t in (u, w, q_dec, attn, k_tail, g_tot))

    def step(S, inp):
        u_n, w_n, qd_n, at_n, kt_n, gt_n = inp
        v_new = u_n - jnp.einsum('bhcd,bhde->bhce', w_n, S)
        o = jnp.einsum('bhcd,bhde->bhce', qd_n, S) + jnp.einsum('bhij,bhje->bhie', at_n, v_new)
        S = S * gt_n[..., None, None] + jnp.einsum('bhcd,bhce->bhde', kt_n, v_new)
        return S, o

    S_fin, o = lax.scan(step, S0, xs)
    o = jnp.transpose(o, (1, 0, 3, 2, 4)).reshape(B, T, H, dv)
    return o, S_fin


def stick_breaking(q, k, v, q_pos, k_pos):
    z = jnp.einsum('bqhd,bkhd->bhqk', q.astype(jnp.float32), k.astype(jnp.float32)) * (HEAD_DIM ** -0.5)
    causal = k_pos[None, :] < q_pos[:, None]
    log_1m = jnp.where(causal, jax.nn.log_sigmoid(-z), 0.0)
    rest = lax.cumsum(log_1m, axis=3, reverse=True) - log_1m
    wts = jnp.where(causal, jnp.exp(jax.nn.log_sigmoid(z) + rest), 0.0)
    return jnp.einsum('bhqk,bkhd->bqhd', wts, v.astype(jnp.float32))


def stick_breaking_prompt(q, k, v):
    B, T, H, d = q.shape
    nb = T // SB_BLOCK
    pos = jnp.arange(T)
    qb = jnp.moveaxis(q.reshape(B, nb, SB_BLOCK, H, d), 1, 0)
    qpos = pos.reshape(nb, SB_BLOCK)
    out = lax.map(lambda blk: stick_breaking(blk[0], k, v, blk[1], pos), (qb, qpos))
    return jnp.moveaxis(out, 0, 1).reshape(B, T, H, d)


def hybrid_layer(x, conv_buf, S0, k_past, v_past, w_in, conv_w, a_log, dt_bias, gdn_norm_w,
                 w_out, ln1_g, ln1_b, w_up, w_down, ln2_g, ln2_b):
    B, T, _ = x.shape
    proj = x @ w_in
    qkv, conv_new = causal_short_conv(proj[..., :OFF_GDN_Z], conv_buf, conv_w)
    qkv = qkv.astype(jnp.float32).reshape(B, T, 3, N_HEADS_GDN, HEAD_DIM)
    q_a = l2_normalize(qkv[:, :, 0]) * (HEAD_DIM ** -0.5)
    k_a = l2_normalize(qkv[:, :, 1])
    v_a = qkv[:, :, 2]
    z = proj[..., OFF_GDN_Z:OFF_GDN_B].astype(jnp.float32).reshape(B, T, N_HEADS_GDN, HEAD_DIM)
    beta = jax.nn.sigmoid(proj[..., OFF_GDN_B:OFF_GDN_A].astype(jnp.float32))
    g = -jnp.exp(a_log.astype(jnp.float32)) * jax.nn.softplus(
        proj[..., OFF_GDN_A:OFF_SB].astype(jnp.float32) + dt_bias.astype(jnp.float32))
    o_a, S_new = gated_delta_rule(q_a, k_a, v_a, beta, g, S0.astype(jnp.float32))
    o_a = (o_a * lax.rsqrt(jnp.mean(jnp.square(o_a), axis=-1, keepdims=True) + RMS_EPS)
           * gdn_norm_w.astype(jnp.float32) * jax.nn.silu(z))
    sb = proj[..., OFF_SB:].reshape(B, T, 3, N_HEADS_SB, HEAD_DIM)
    q_b, k_b, v_b = sb[:, :, 0], sb[:, :, 1], sb[:, :, 2]
    if k_past is None:
        o_b = stick_breaking_prompt(q_b, k_b, v_b)
    else:
        P = k_past.shape[1]
        k_all = jnp.concatenate([k_past.astype(k_b.dtype), k_b], axis=1)
        v_all = jnp.concatenate([v_past.astype(v_b.dtype), v_b], axis=1)
        o_b = stick_breaking(q_b, k_all, v_all, P + jnp.arange(T), jnp.arange(P + T))
    mixed = jnp.concatenate([o_a.reshape(B, T, GDN_WIDTH), o_b.reshape(B, T, SB_WIDTH)],
                            axis=-1).astype(x.dtype)
    x = layer_norm(DEEPNORM_ALPHA * x + mixed @ w_out, ln1_g, ln1_b)
    h = jnp.square(jax.nn.relu(x @ w_up))
    x = layer_norm(DEEPNORM_ALPHA * x + h @ w_down, ln2_g, ln2_b)
    return x, conv_new, S_new, k_b, v_b


def setup_inputs(seed: int = 0) -> dict:
    key = jax.random.key(seed)
    ks = jax.random.split(key, 18)
    f32 = jnp.float32
    col_scale = jnp.ones((PROJ_WIDTH,), f32)
    col_scale = col_scale.at[2 * GDN_WIDTH:3 * GDN_WIDTH].set(DEEPNORM_BETA)
    col_scale = col_scale.at[OFF_SB + 2 * SB_WIDTH:].set(DEEPNORM_BETA)
    w_in = jax.random.normal(ks[6], (DEPTH, D_MODEL, PROJ_WIDTH), f32) * (D_MODEL ** -0.5) * col_scale
    log_lo, log_hi = math.log(1e-3), math.log(1e-1)
    dt = jnp.exp(jax.random.uniform(ks[9], (DEPTH, N_HEADS_GDN), f32) * (log_hi - log_lo) + log_lo)
    dt_bias = dt + jnp.log(-jnp.expm1(-dt))
    return {
        "x_prompt": jax.random.normal(ks[0], (BATCH, SEQ, D_MODEL), f32),
        "x_sample": jax.random.normal(ks[1], (DEC_BATCH, DEC_SEQ, D_MODEL), f32),
        "state_gdn_conv": jax.random.normal(ks[2], (DEPTH, DEC_BATCH, CONV_W - 1, 3 * GDN_WIDTH), f32),
        "state_gdn_S": 0.5 * jax.random.normal(ks[3], (DEPTH, DEC_BATCH, N_HEADS_GDN, HEAD_DIM, HEAD_DIM), f32),
        "cache_sb_k": jax.random.normal(ks[4], (DEPTH, DEC_BATCH, PAST_LEN, N_HEADS_SB, HEAD_DIM), f32),
        "cache_sb_v": DEEPNORM_BETA * jax.random.normal(ks[5], (DEPTH, DEC_BATCH, PAST_LEN, N_HEADS_SB, HEAD_DIM), f32),
        "w_in": w_in,
        "conv_w": 0.5 * jax.random.normal(ks[7], (DEPTH, CONV_W, 3 * GDN_WIDTH), f32),
        "a_log": jnp.log(jax.random.uniform(ks[8], (DEPTH, N_HEADS_GDN), f32, 1.0, 16.0)),
        "dt_bias": dt_bias,
        "gdn_norm_w": 1.0 + 0.02 * jax.random.normal(ks[10], (DEPTH, HEAD_DIM), f32),
        "w_out": jax.random.normal(ks[11], (DEPTH, MIX_WIDTH, D_MODEL), f32) * (MIX_WIDTH ** -0.5) * DEEPNORM_BETA,
        "ln1_g": 1.0 + 0.02 * jax.random.normal(ks[12], (DEPTH, D_MODEL), f32),
        "ln1_b": 0.02 * jax.random.normal(ks[13], (DEPTH, D_MODEL), f32),
        "w_up": jax.random.normal(ks[14], (DEPTH, D_MODEL, D_FF), f32) * (D_MODEL ** -0.5),
        "w_down": jax.random.normal(ks[15], (DEPTH, D_FF, D_MODEL), f32) * (D_FF ** -0.5) * DEEPNORM_BETA,
        "ln2_g": 1.0 + 0.02 * jax.random.normal(ks[16], (DEPTH, D_MODEL), f32),
        "ln2_b": 0.02 * jax.random.normal(ks[17], (DEPTH, D_MODEL), f32),
    }


def reference(x_prompt, x_sample, state_gdn_conv, state_gdn_S, cache_sb_k, cache_sb_v,
              w_in, conv_w, a_log, dt_bias, gdn_norm_w, w_out, ln1_g, ln1_b,
              w_up, w_down, ln2_g, ln2_b):
    yp, ys = x_prompt, x_sample
    conv_p, S_p, k_p, v_p = [], [], [], []
    conv_s, S_s, k_s, v_s = [], [], [], []
    for l in range(DEPTH):
        wl = (w_in[l], conv_w[l], a_log[l], dt_bias[l], gdn_norm_w[l], w_out[l],
              ln1_g[l], ln1_b[l], w_up[l], w_down[l], ln2_g[l], ln2_b[l])
        b_p = yp.shape[0]
        zero_conv = jnp.zeros((b_p, CONV_W - 1, 3 * GDN_WIDTH), yp.dtype)
        zero_S = jnp.zeros((b_p, N_HEADS_GDN, HEAD_DIM, HEAD_DIM), jnp.float32)
        yp, c1, s1, kk1, vv1 = hybrid_layer(yp, zero_conv, zero_S, None, None, *wl)
        ys, c2, s2, kk2, vv2 = hybrid_layer(ys, state_gdn_conv[l], state_gdn_S[l],
                                            cache_sb_k[l], cache_sb_v[l], *wl)
        conv_p.append(c1); S_p.append(s1); k_p.append(kk1); v_p.append(vv1)
        conv_s.append(c2); S_s.append(s2); k_s.append(kk2); v_s.append(vv2)
    new_conv_p = jnp.stack(conv_p)
    new_S_p = jnp.stack(S_p)
    new_k_p = jnp.stack(k_p)
    new_v_p = jnp.stack(v_p)
    new_conv_s = jnp.stack(conv_s)
    new_S_s = jnp.stack(S_s)
    new_k_s = jnp.stack(k_s)
    new_v_s = jnp.stack(v_s)
    return (yp, ys, new_conv_p, new_S_p, new_k_p, new_v_p, new_conv_s, new_S_s, new_k_s, new_v_s)
```

```python
import functools

import jax
import jax.numpy as jnp
from jax import lax
from jax.experimental import pallas as pl
from jax.experimental.pallas import tpu as pltpu

HEAD_DIM = 128
N_HEADS = 8
WIDTH = N_HEADS * HEAD_DIM
CONV_W = 4
CHUNK = 64
LN_EPS = 1e-5
RMS_EPS = 1e-6
L2_EPS = 1e-6
LANES = 128
SUBLANES = 8
VMEM_LIMIT = 56 * 1024 * 1024

F32 = jnp.float32
BF16 = jnp.bfloat16
HI = lax.Precision.HIGHEST


def _pick(n, pref):
    if n <= pref:
        return n
    t = pref
    while n % t:
        t //= 2
    return t


def _dot(a, b, precision=None):
    return jnp.dot(a, b, preferred_element_type=F32, precision=precision)


def _dot_nt(a, b, precision=None):
    return lax.dot_general(a, b, (((1,), (1,)), ((), ())), preferred_element_type=F32,
                           precision=precision)


def _dot_tn(a, b, precision=None):
    return lax.dot_general(a, b, (((0,), (0,)), ((), ())), preferred_element_type=F32,
                           precision=precision)


def _sigmoid(x):
    return 1.0 / (1.0 + jnp.exp(-x))


def _softplus(x):
    return jnp.maximum(x, 0.0) + jnp.log1p(jnp.exp(-jnp.abs(x)))


def _mm_kernel(x_ref, w_ref, o_ref):
    o_ref[...] = _dot(x_ref[...], w_ref[...]).astype(o_ref.dtype)


def _matmul(x, w, name):
    m, k = x.shape
    n = w.shape[1]
    tm = _pick(m, 512)
    tn = _pick(n, 1024)
    return pl.pallas_call(
        _mm_kernel,
        grid=(n // tn, m // tm),
        in_specs=[pl.BlockSpec((tm, k), lambda j, i: (i, 0)),
                  pl.BlockSpec((k, tn), lambda j, i: (0, j))],
        out_specs=pl.BlockSpec((tm, tn), lambda j, i: (i, j)),
        out_shape=jax.ShapeDtypeStruct((m, n), F32),
        compiler_params=pltpu.CompilerParams(
            dimension_semantics=("parallel", "parallel"), vmem_limit_bytes=VMEM_LIMIT),
        name=name,
    )(x, w)


def _gdn_kernel(q_ref, k_ref, v_ref, z_ref, ba_ref, cq_ref, ck_ref, cv_ref,
                wq_ref, wk_ref, wv_ref, s0_ref, alog_ref, dtb_ref, nw_ref,
                o_ref, s_ref, *, seq, chunk):
    C = chunk
    h = pl.program_id(1)
    row = lax.broadcasted_iota(jnp.int32, (C, C), 0)
    col = lax.broadcasted_iota(jnp.int32, (C, C), 1)
    incl = row >= col
    strict = row > col
    l_incl = incl.astype(F32)
    u_incl = (row <= col).astype(F32)
    ones = jnp.ones((C, C), F32)
    eye = (row == col).astype(F32)
    lane = lax.broadcasted_iota(jnp.int32, (C, LANES), 1)
    level_masks = []
    b = 1
    while b < C:
        s = b.bit_length() - 1
        same = (row >> (s + 1)) == (col >> (s + 1))
        level_masks.append(same & (((row >> s) & 1) == 1) & (((col >> s) & 1) == 0))
        b *= 2

    neg_a = -jnp.exp(alog_ref[...])
    dtb = dtb_ref[...]
    nw = nw_ref[...]
    wq = wq_ref[...]
    wk = wk_ref[...]
    wv = wv_ref[...]

    def conv(e, w):
        acc = e[5:5 + C] * w[0:1]
        for i in range(1, CONV_W):
            acc = acc + e[5 + i:5 + i + C] * w[i:i + 1]
        return acc * _sigmoid(acc)

    def body(n, carry):
        S, tq, tk, tv = carry
        c0 = pl.multiple_of(n * C, C)
        xq = q_ref[pl.ds(c0, C), :]
        xk = k_ref[pl.ds(c0, C), :]
        xv = v_ref[pl.ds(c0, C), :]
        q = conv(jnp.concatenate([tq, xq], axis=0), wq)
        k = conv(jnp.concatenate([tk, xk], axis=0), wk)
        v = conv(jnp.concatenate([tv, xv], axis=0), wv)
        q = q * lax.rsqrt(jnp.sum(q * q, axis=-1, keepdims=True) + L2_EPS) * (HEAD_DIM ** -0.5)
        k = k * lax.rsqrt(jnp.sum(k * k, axis=-1, keepdims=True) + L2_EPS)

        ba = ba_ref[pl.ds(c0, C), :]
        beta_all = _sigmoid(ba)
        g_all = neg_a * _softplus(ba + dtb)
        beta = jnp.sum(jnp.where(lane == h, beta_all, 0.0), axis=-1, keepdims=True)
        g = jnp.sum(jnp.where(lane == h + N_HEADS, g_all, 0.0), axis=-1, keepdims=True)

        gcb = _dot(l_incl, jnp.broadcast_to(g, (C, LANES)), HI)
        gc_row = _dot(ones, jnp.broadcast_to(g, (C, C)) * u_incl, HI)
        decay = jnp.where(incl, jnp.exp(gcb[:, :C] - gc_row), 0.0)
        egc = jnp.exp(gcb)
        g_last = gcb[C - 1:C, :]

        k_beta = k * beta
        m = jnp.where(strict, _dot_nt(k_beta, k, HI) * decay, 0.0)
        t_inv = eye - jnp.where(level_masks[0], m, 0.0)
        for mask in level_masks[1:]:
            t_inv = t_inv - _dot(_dot(t_inv, jnp.where(mask, m, 0.0), HI), t_inv, HI)

        u = _dot(t_inv, v * beta, HI)
        w = _dot(t_inv, k_beta * egc, HI)
        attn = _dot_nt(q, k, HI) * decay
        q_dec = q * egc
        k_tail = k * jnp.exp(g_last - gcb)
        v_new = u - _dot(w, S, HI)
        o = _dot(q_dec, S, HI) + _dot(attn, v_new, HI)
        S = S * jnp.exp(g_last) + _dot_tn(k_tail, v_new, HI)

        zc = z_ref[pl.ds(c0, C), :]
        o = (o * lax.rsqrt(jnp.mean(o * o, axis=-1, keepdims=True) + RMS_EPS)
             * nw * (zc * _sigmoid(zc)))
        o_ref[pl.ds(c0, C), :] = o.astype(o_ref.dtype)
        return S, xq[C - 8:C], xk[C - 8:C], xv[C - 8:C]

    init = (s0_ref[...], cq_ref[...], ck_ref[...], cv_ref[...])
    S, _, _, _ = lax.fori_loop(0, seq // C, body, init)
    s_ref[...] = S


def _gdn(pre, ba, conv8, s0, conv_w, alog_row, dtb_row, nw_row, name):
    bsz, seq, _ = pre.shape
    chunk = CHUNK if seq % CHUNK == 0 else seq
    nh = N_HEADS

    def col(off):
        return pl.BlockSpec((None, seq, HEAD_DIM), lambda b, h: (b, 0, h + off))

    def st(off):
        return pl.BlockSpec((None, SUBLANES, HEAD_DIM), lambda b, h: (b, 0, h + off))

    def cw(off):
        return pl.BlockSpec((CONV_W, HEAD_DIM), lambda b, h: (0, h + off))

    row = pl.BlockSpec((1, LANES), lambda b, h: (0, 0))
    return pl.pallas_call(
        functools.partial(_gdn_kernel, seq=seq, chunk=chunk),
        grid=(bsz, nh),
        in_specs=[col(0), col(nh), col(2 * nh), col(3 * nh),
                  pl.BlockSpec((None, seq, LANES), lambda b, h: (b, 0, 0)),
                  st(0), st(nh), st(2 * nh), cw(0), cw(nh), cw(2 * nh),
                  pl.BlockSpec((None, None, HEAD_DIM, HEAD_DIM), lambda b, h: (b, h, 0, 0)),
                  row, row, row],
        out_specs=[pl.BlockSpec((None, seq, HEAD_DIM), lambda b, h: (b, 0, h)),
                   pl.BlockSpec((None, None, HEAD_DIM, HEAD_DIM), lambda b, h: (b, h, 0, 0))],
        out_shape=[jax.ShapeDtypeStruct((bsz, seq, WIDTH), BF16),
                   jax.ShapeDtypeStruct((bsz, nh, HEAD_DIM, HEAD_DIM), F32)],
        compiler_params=pltpu.CompilerParams(
            dimension_semantics=("parallel", "parallel"), vmem_limit_bytes=VMEM_LIMIT),
        name=name,
    )(pre, pre, pre, pre, ba, conv8, conv8, conv8, conv_w, conv_w, conv_w, s0,
      alog_row, dtb_row, nw_row)


def _split3(x):
    hi = x.astype(BF16)
    r = x - hi.astype(F32)
    mid = r.astype(BF16)
    lo = (r - mid.astype(F32)).astype(BF16)
    return hi, mid, lo


def _sb_block(qb, kb, vb, l_incl, causal, run, scale):
    z = _dot_nt(qb, kb) * scale
    l1m = -_softplus(z)
    if causal is not None:
        l1m = jnp.where(causal, l1m, 0.0)
    hi, mid, lo = _split3(l1m)
    cs = _dot(hi, l_incl) + _dot(mid, l_incl) + _dot(lo, l_incl)
    w = jnp.exp(z + cs + run)
    if causal is not None:
        w = jnp.where(causal, w, 0.0)
    return _dot(w.astype(BF16), vb), run + cs[:, 0:1]


def _suffix_ones(n):
    r = lax.broadcasted_iota(jnp.int32, (n, n), 0)
    c = lax.broadcasted_iota(jnp.int32, (n, n), 1)
    return (r >= c).astype(BF16)


def _sb_prompt_kernel(q_ref, k_ref, v_ref, o_ref, *, blk, scale):
    qi = pl.program_id(2)
    qb = q_ref[...].astype(BF16)
    l_incl = _suffix_ones(blk)
    r = lax.broadcasted_iota(jnp.int32, (blk, blk), 0)
    c = lax.broadcasted_iota(jnp.int32, (blk, blk), 1)

    def body(jj, carry):
        acc, run = carry
        j = qi - jj
        k0 = pl.multiple_of(j * blk, blk)
        kb = k_ref[pl.ds(k0, blk), :].astype(BF16)
        vb = v_ref[pl.ds(k0, blk), :].astype(BF16)
        causal = (c + k0) < (r + qi * blk)
        d, run = _sb_block(qb, kb, vb, l_incl, causal, run, scale)
        return acc + d, run

    acc, _ = lax.fori_loop(0, qi + 1, body,
                           (jnp.zeros((blk, HEAD_DIM), F32), jnp.zeros((blk, 1), F32)))
    o_ref[...] = acc.astype(o_ref.dtype)


def _sb_prompt(q, k, v, name):
    bsz, seq, _ = q.shape
    blk = _pick(seq, 256)
    kv = pl.BlockSpec((None, seq, HEAD_DIM), lambda b, h, i: (b, 0, h))
    qo = pl.BlockSpec((None, blk, HEAD_DIM), lambda b, h, i: (b, i, h))
    return pl.pallas_call(
        functools.partial(_sb_prompt_kernel, blk=blk, scale=HEAD_DIM ** -0.5),
        grid=(bsz, N_HEADS, seq // blk),
        in_specs=[qo, kv, kv],
        out_specs=qo,
        out_shape=jax.ShapeDtypeStruct((bsz, seq, WIDTH), BF16),
        compiler_params=pltpu.CompilerParams(
            dimension_semantics=("parallel", "parallel", "arbitrary"),
            vmem_limit_bytes=VMEM_LIMIT),
        name=name,
    )(q, k, v)


def _sb_sample_kernel(q_ref, kn_ref, vn_ref, kc_ref, vc_ref, o_ref, *, seq, past, blk, scale):
    qb = q_ref[...].astype(BF16)
    r = lax.broadcasted_iota(jnp.int32, (seq, seq), 0)
    c = lax.broadcasted_iota(jnp.int32, (seq, seq), 1)
    run = jnp.zeros((seq, 1), F32)
    acc, run = _sb_block(qb, kn_ref[...].astype(BF16), vn_ref[...].astype(BF16),
                         _suffix_ones(seq), c < r, run, scale)
    l_incl = _suffix_ones(blk)
    for j in reversed(range(past // blk)):
        kb = kc_ref[j * blk:(j + 1) * blk, :].astype(BF16)
        vb = vc_ref[j * blk:(j + 1) * blk, :].astype(BF16)
        d, run = _sb_block(qb, kb, vb, l_incl, None, run, scale)
        acc = acc + d
    o_ref[...] = acc.astype(o_ref.dtype)


def _sb_sample(q, kn, vn, kc, vc, name):
    bsz, seq, _ = q.shape
    past = kc.shape[1]
    blk = _pick(past, 256)
    new = pl.BlockSpec((None, seq, HEAD_DIM), lambda b, h: (b, 0, h))
    old = pl.BlockSpec((None, past, HEAD_DIM), lambda b, h: (b, 0, h))
    return pl.pallas_call(
        functools.partial(_sb_sample_kernel, seq=seq, past=past, blk=blk, scale=HEAD_DIM ** -0.5),
        grid=(bsz, N_HEADS),
        in_specs=[new, new, new, old, old],
        out_specs=new,
        out_shape=jax.ShapeDtypeStruct((bsz, seq, WIDTH), BF16),
        compiler_params=pltpu.CompilerParams(
            dimension_semantics=("parallel", "parallel"), vmem_limit_bytes=VMEM_LIMIT),
        name=name,
    )(q, kn, vn, kc, vc)


def _layer_norm(r, g, b):
    mu = jnp.mean(r, axis=-1, keepdims=True)
    d = r - mu
    var = jnp.mean(d * d, axis=-1, keepdims=True)
    return d * lax.rsqrt(var + LN_EPS) * g + b


def _outproj_kernel(oa_ref, ob_ref, x_ref, wa_ref, wb_ref, g_ref, b_ref, y_ref, *, alpha):
    acc = _dot(oa_ref[...], wa_ref[...]) + _dot(ob_ref[...], wb_ref[...])
    y_ref[...] = _layer_norm(alpha * x_ref[...] + acc, g_ref[...], b_ref[...])


def _outproj_ln(oa, ob, x, wa, wb, g, b, alpha, name):
    m, d = x.shape
    tm = _pick(m, 256)
    rows = lambda w: pl.BlockSpec((tm, w), lambda i: (i, 0))
    full = lambda a: pl.BlockSpec(a.shape, lambda i: (0, 0))
    return pl.pallas_call(
        functools.partial(_outproj_kernel, alpha=alpha),
        grid=(m // tm,),
        in_specs=[rows(WIDTH), rows(WIDTH), rows(d), full(wa), full(wb), full(g), full(b)],
        out_specs=rows(d),
        out_shape=jax.ShapeDtypeStruct((m, d), F32),
        compiler_params=pltpu.CompilerParams(
            dimension_semantics=("parallel",), vmem_limit_bytes=VMEM_LIMIT),
        name=name,
    )(oa, ob, x, wa, wb, g, b)


def _ffn_kernel(x_ref, wu_ref, wd_ref, g_ref, b_ref, y_ref, xb_ref, acc_ref, *, alpha):
    j = pl.program_id(1)

    @pl.when(j == 0)
    def _():
        xb_ref[...] = x_ref[...].astype(BF16)
        acc_ref[...] = jnp.zeros_like(acc_ref)

    hid = jnp.maximum(_dot(xb_ref[...], wu_ref[...]), 0.0)
    acc_ref[...] += _dot((hid * hid).astype(BF16), wd_ref[...])

    @pl.when(j == pl.num_programs(1) - 1)
    def _():
        y_ref[...] = _layer_norm(alpha * x_ref[...] + acc_ref[...], g_ref[...], b_ref[...])


def _ffn_ln(x, wu, wd, g, b, alpha, name):
    m, d = x.shape
    dff = wu.shape[1]
    tm = _pick(m, 512)
    tf = _pick(dff, 1024)
    return pl.pallas_call(
        functools.partial(_ffn_kernel, alpha=alpha),
        grid=(m // tm, dff // tf),
        in_specs=[pl.BlockSpec((tm, d), lambda i, j: (i, 0)),
                  pl.BlockSpec((d, tf), lambda i, j: (0, j)),
                  pl.BlockSpec((tf, d), lambda i, j: (j, 0)),
                  pl.BlockSpec((1, d), lambda i, j: (0, 0)),
                  pl.BlockSpec((1, d), lambda i, j: (0, 0))],
        out_specs=pl.BlockSpec((tm, d), lambda i, j: (i, 0)),
        out_shape=jax.ShapeDtypeStruct((m, d), F32),
        scratch_shapes=[pltpu.VMEM((tm, d), BF16), pltpu.VMEM((tm, d), F32)],
        compiler_params=pltpu.CompilerParams(
            dimension_semantics=("parallel", "arbitrary"), vmem_limit_bytes=VMEM_LIMIT),
        name=name,
    )(x, wu, wd, g, b)


def _layer(x, conv_buf, s0, k_past, v_past, wts, alpha, tag):
    (w_a, w_ba, w_q, w_k, w_v, conv_w, alog_row, dtb_row, nw_row,
     wo_a, wo_b, ln1_g, ln1_b, w_up, w_down, ln2_g, ln2_b) = wts
    bsz, seq, d = x.shape
    m = bsz * seq
    x2 = x.reshape(m, d)
    xb = x2.astype(BF16)

    pre = _matmul(xb, w_a, "proj_gdn_" + tag).reshape(bsz, seq, 4 * WIDTH)
    ba = _matmul(xb, w_ba, "proj_gate_" + tag).reshape(bsz, seq, LANES)
    q_b = _matmul(xb, w_q, "proj_sbq_" + tag).reshape(bsz, seq, WIDTH)
    k_b = _matmul(xb, w_k, "proj_sbk_" + tag).reshape(bsz, seq, WIDTH)
    v_b = _matmul(xb, w_v, "proj_sbv_" + tag).reshape(bsz, seq, WIDTH)

    conv8 = jnp.pad(conv_buf, ((0, 0), (SUBLANES - (CONV_W - 1), 0), (0, 0)))
    o_a, s_new = _gdn(pre, ba, conv8, s0, conv_w, alog_row, dtb_row, nw_row, "gdn_" + tag)
    hist = jnp.concatenate([conv_buf, pre[:, :, :3 * WIDTH]], axis=1) if seq < CONV_W - 1 \
        else pre[:, :, :3 * WIDTH]
    conv_new = hist[:, -(CONV_W - 1):, :]

    if k_past is None:
        o_b = _sb_prompt(q_b, k_b, v_b, "sb_" + tag)
    else:
        past = k_past.shape[1]
        o_b = _sb_sample(q_b, k_b, v_b, k_past.reshape(bsz, past, WIDTH),
                         v_past.reshape(bsz, past, WIDTH), "sb_" + tag)

    x1 = _outproj_ln(o_a.reshape(m, WIDTH), o_b.reshape(m, WIDTH), x2, wo_a, wo_b,
                     ln1_g, ln1_b, alpha, "outproj_" + tag)
    y = _ffn_ln(x1, w_up, w_down, ln2_g, ln2_b, alpha, "ffn_" + tag)
    return (y.reshape(bsz, seq, d), conv_new, s_new,
            k_b.reshape(bsz, seq, N_HEADS, HEAD_DIM), v_b.reshape(bsz, seq, N_HEADS, HEAD_DIM))


def kernel(x_prompt, x_sample, state_gdn_conv, state_gdn_S, cache_sb_k, cache_sb_v, w_in, conv_w, a_log, dt_bias, gdn_norm_w, w_out, ln1_g, ln1_b, w_up, w_down, ln2_g, ln2_b):
    depth = w_in.shape[0]
    alpha = (2 * depth) ** 0.25
    off_b = 4 * WIDTH
    off_sb = off_b + 2 * N_HEADS
    yp, ys = x_prompt, x_sample
    outs = [[] for _ in range(8)]
    for l in range(depth):
        wl = w_in[l]
        w_ba = jnp.pad(wl[:, off_b:off_sb], ((0, 0), (0, LANES - 2 * N_HEADS)))
        pad_row = lambda v: jnp.pad(v.astype(F32), (N_HEADS, LANES - 2 * N_HEADS)).reshape(1, LANES)
        wts = (wl[:, :off_b].astype(BF16), w_ba.astype(BF16),
               wl[:, off_sb:off_sb + WIDTH].astype(BF16),
               wl[:, off_sb + WIDTH:off_sb + 2 * WIDTH].astype(BF16),
               wl[:, off_sb + 2 * WIDTH:].astype(BF16),
               conv_w[l], pad_row(a_log[l]), pad_row(dt_bias[l]),
               gdn_norm_w[l].astype(F32).reshape(1, HEAD_DIM),
               w_out[l, :WIDTH].astype(BF16), w_out[l, WIDTH:].astype(BF16),
               ln1_g[l].reshape(1, -1), ln1_b[l].reshape(1, -1),
               w_up[l].astype(BF16), w_down[l].astype(BF16),
               ln2_g[l].reshape(1, -1), ln2_b[l].reshape(1, -1))
        bp = yp.shape[0]
        zero_conv = jnp.zeros((bp, CONV_W - 1, 3 * WIDTH), F32)
        zero_s = jnp.zeros((bp, N_HEADS, HEAD_DIM, HEAD_DIM), F32)
        yp, c1, s1, k1, v1 = _layer(yp, zero_conv, zero_s, None, None, wts, alpha, "p%d" % l)
        ys, c2, s2, k2, v2 = _layer(ys, state_gdn_conv[l], state_gdn_S[l],
                                    cache_sb_k[l], cache_sb_v[l], wts, alpha, "s%d" % l)
        for lst, val in zip(outs, (c1, s1, k1, v1, c2, s2, k2, v2)):
            lst.append(val)
    return (yp, ys) + tuple(jnp.stack(o) for o in outs)
```

```python
import functools

import jax
import jax.numpy as jnp
from jax import lax
from jax.experimental import pallas as pl
from jax.experimental.pallas import tpu as pltpu

HEAD_DIM = 128
N_HEADS = 8
WIDTH = N_HEADS * HEAD_DIM
CONV_W = 4
CHUNK = 64
LN_EPS = 1e-5
RMS_EPS = 1e-6
L2_EPS = 1e-6
LANES = 128
SUBLANES = 8
VMEM_LIMIT = 56 * 1024 * 1024

F32 = jnp.float32
BF16 = jnp.bfloat16


def _pick(n, pref):
    if n <= pref:
        return n
    t = pref
    while n % t:
        t //= 2
    return t


def _dot(a, b):
    return jnp.dot(a, b, preferred_element_type=F32)


def _dot_nt(a, b):
    return lax.dot_general(a, b, (((1,), (1,)), ((), ())), preferred_element_type=F32)


def _sigmoid(x):
    return 1.0 / (1.0 + jnp.exp(-x))


def _softplus(x):
    return jnp.maximum(x, 0.0) + jnp.log1p(jnp.exp(-jnp.abs(x)))


def _mm_kernel(x_ref, w_ref, o_ref):
    o_ref[...] = _dot(x_ref[...], w_ref[...]).astype(o_ref.dtype)


def _matmul(x, w, name):
    m, k = x.shape
    n = w.shape[1]
    tm = _pick(m, 512)
    tn = _pick(n, 1024)
    return pl.pallas_call(
        _mm_kernel,
        grid=(n // tn, m // tm),
        in_specs=[pl.BlockSpec((tm, k), lambda j, i: (i, 0)),
                  pl.BlockSpec((k, tn), lambda j, i: (0, j))],
        out_specs=pl.BlockSpec((tm, tn), lambda j, i: (i, j)),
        out_shape=jax.ShapeDtypeStruct((m, n), F32),
        compiler_params=pltpu.CompilerParams(
            dimension_semantics=("parallel", "parallel"), vmem_limit_bytes=VMEM_LIMIT),
        name=name,
    )(x, w)


NN = (((1,), (0,)), ((), ()))
NT = (((1,), (1,)), ((), ()))
TN = (((0,), (0,)), ((), ()))

GDN_ROWS_PER_STEP = 512
GDN_PIECES = dict(qk=1, inv=1, uw=1, scan=1)


def _pieces(x, n):
    out = []
    r = x
    for i in range(n):
        p = r.astype(BF16)
        out.append(p)
        if i + 1 < n:
            r = r - p.astype(F32)
    return out


def _mm(a, b, dims=NN):
    order = max(len(a), len(b))
    acc = None
    for i, ai in enumerate(a):
        for j, bj in enumerate(b):
            if i + j < order:
                t = lax.dot_general(ai, bj, dims, preferred_element_type=F32)
                acc = t if acc is None else acc + t
    return acc


def _gdn_kernel(q_ref, k_ref, v_ref, z_ref, ba_ref, cq_ref, ck_ref, cv_ref,
                wq_ref, wk_ref, wv_ref, s0_ref, alog_ref, dtb_ref, nw_ref,
                o_ref, s_ref, tail_ref, *, rows, chunk):
    C = chunk
    heads = range(N_HEADS)
    row = lax.broadcasted_iota(jnp.int32, (C, C), 0)
    col = lax.broadcasted_iota(jnp.int32, (C, C), 1)
    incl = row >= col
    strict = row > col
    upper = row <= col
    l_incl = [incl.astype(BF16)]
    ones = [jnp.ones((C, C), BF16)]
    eye = (row == col).astype(F32)
    lane = lax.broadcasted_iota(jnp.int32, (C, LANES), 1)
    level_masks = []
    b = 1
    while b < C:
        s = b.bit_length() - 1
        same = (row >> (s + 1)) == (col >> (s + 1))
        level_masks.append(same & (((row >> s) & 1) == 1) & (((col >> s) & 1) == 0))
        b *= 2
    n_qk, n_inv, n_uw, n_scan = (GDN_PIECES[c] for c in ("qk", "inv", "uw", "scan"))

    neg_a = -jnp.exp(alog_ref[...])
    dtb = dtb_ref[...]
    nw = nw_ref[...]
    wq = wq_ref[...]
    wk = wk_ref[...]
    wv = wv_ref[...]

    @pl.when(pl.program_id(1) == 0)
    def _():
        s_ref[...] = s0_ref[...]
        tail_ref[0] = cq_ref[...]
        tail_ref[1] = ck_ref[...]
        tail_ref[2] = cv_ref[...]

    def conv(e, w):
        acc = e[5:5 + C] * w[0:1]
        for i in range(1, CONV_W):
            acc = acc + e[5 + i:5 + i + C] * w[i:i + 1]
        return acc * _sigmoid(acc)

    def head(x, i):
        return x[:, i * HEAD_DIM:(i + 1) * HEAD_DIM]

    def body(n, carry):
        tq, tk, tv = carry
        c0 = pl.multiple_of(n * C, C)
        xq = q_ref[pl.ds(c0, C), :]
        xk = k_ref[pl.ds(c0, C), :]
        xv = v_ref[pl.ds(c0, C), :]
        q_all = conv(jnp.concatenate([tq, xq], axis=0), wq)
        k_all = conv(jnp.concatenate([tk, xk], axis=0), wk)
        v_all = conv(jnp.concatenate([tv, xv], axis=0), wv)
        zc = z_ref[pl.ds(c0, C), :]
        gate_all = zc * _sigmoid(zc)
        ba = ba_ref[pl.ds(c0, C), :]
        beta_all = _sigmoid(ba)
        g_all = neg_a * _softplus(ba + dtb)

        q = [head(q_all, i) for i in heads]
        k = [head(k_all, i) for i in heads]
        q = [x * lax.rsqrt(jnp.sum(x * x, axis=-1, keepdims=True) + L2_EPS) * (HEAD_DIM ** -0.5)
             for x in q]
        k = [x * lax.rsqrt(jnp.sum(x * x, axis=-1, keepdims=True) + L2_EPS) for x in k]
        beta = [jnp.sum(jnp.where(lane == i, beta_all, 0.0), axis=-1, keepdims=True) for i in heads]
        g = [jnp.sum(jnp.where(lane == i + N_HEADS, g_all, 0.0), axis=-1, keepdims=True)
             for i in heads]
        k_beta = [k[i] * beta[i] for i in heads]
        kp = [_pieces(k[i], n_qk) for i in heads]

        gcb = [_mm(l_incl, _pieces(jnp.broadcast_to(g[i], (C, LANES)), 3)) for i in heads]
        gc_row = [_mm(ones, _pieces(jnp.where(upper, jnp.broadcast_to(g[i], (C, C)), 0.0), 3))
                  for i in heads]
        kk = [_mm(_pieces(k_beta[i], n_qk), kp[i], NT) for i in heads]
        qk = [_mm(_pieces(q[i], n_qk), kp[i], NT) for i in heads]

        decay = [jnp.where(incl, jnp.exp(gcb[i][:, :C] - gc_row[i]), 0.0) for i in heads]
        egc = [jnp.exp(gcb[i]) for i in heads]
        g_last = [gcb[i][C - 1:C, :] for i in heads]
        m = [jnp.where(strict, kk[i] * decay[i], 0.0) for i in heads]
        attn = [qk[i] * decay[i] for i in heads]

        t_inv = [eye - jnp.where(level_masks[0], m[i], 0.0) for i in heads]
        for mask in level_masks[1:]:
            tp = [_pieces(t_inv[i], n_inv) for i in heads]
            x = [_mm(tp[i], _pieces(jnp.where(mask, m[i], 0.0), n_inv)) for i in heads]
            y = [_mm(_pieces(x[i], n_inv), tp[i]) for i in heads]
            t_inv = [t_inv[i] - y[i] for i in heads]

        rhs = [jnp.concatenate([head(v_all, i) * beta[i], k_beta[i] * egc[i]], axis=1) for i in heads]
        uw = [_mm(_pieces(t_inv[i], n_uw), _pieces(rhs[i], n_uw)) for i in heads]

        s_old = [s_ref[i] for i in heads]
        lhs = [jnp.concatenate([uw[i][:, HEAD_DIM:], q[i] * egc[i]], axis=0) for i in heads]
        ws = [_mm(_pieces(lhs[i], n_scan), _pieces(s_old[i], n_scan)) for i in heads]
        v_new = [uw[i][:, :HEAD_DIM] - ws[i][:C] for i in heads]
        vp = [_pieces(v_new[i], n_scan) for i in heads]
        k_tail = [k[i] * jnp.exp(g_last[i] - gcb[i]) for i in heads]

        o = [ws[i][C:] + _mm(_pieces(attn[i], n_scan), vp[i]) for i in heads]
        s_new = [s_old[i] * jnp.exp(g_last[i]) + _mm(_pieces(k_tail[i], n_scan), vp[i], TN)
                 for i in heads]
        for i in heads:
            s_ref[i] = s_new[i]
            on = (o[i] * lax.rsqrt(jnp.mean(o[i] * o[i], axis=-1, keepdims=True) + RMS_EPS)
                  * nw * head(gate_all, i))
            o_ref[pl.ds(c0, C), i * HEAD_DIM:(i + 1) * HEAD_DIM] = on.astype(o_ref.dtype)
        return xq[C - 8:C], xk[C - 8:C], xv[C - 8:C]

    tails = lax.fori_loop(0, rows // C, body, (tail_ref[0], tail_ref[1], tail_ref[2]))
    for j in range(3):
        tail_ref[j] = tails[j]


def _gdn(pre, ba, conv8, s0, conv_w, alog_row, dtb_row, nw_row, name):
    bsz, seq, _ = pre.shape
    chunk = CHUNK if seq % CHUNK == 0 else seq
    rows = _pick(seq, GDN_ROWS_PER_STEP)

    def col(grp):
        return pl.BlockSpec((None, rows, WIDTH), lambda b, t: (b, t, grp))

    def st(grp):
        return pl.BlockSpec((None, SUBLANES, WIDTH), lambda b, t: (b, 0, grp))

    def cw(grp):
        return pl.BlockSpec((CONV_W, WIDTH), lambda b, t: (0, grp))

    row = pl.BlockSpec((1, LANES), lambda b, t: (0, 0))
    state = pl.BlockSpec((None, N_HEADS, HEAD_DIM, HEAD_DIM), lambda b, t: (b, 0, 0, 0))
    return pl.pallas_call(
        functools.partial(_gdn_kernel, rows=rows, chunk=chunk),
        grid=(bsz, seq // rows),
        in_specs=[col(0), col(1), col(2), col(3),
                  pl.BlockSpec((None, rows, LANES), lambda b, t: (b, t, 0)),
                  st(0), st(1), st(2), cw(0), cw(1), cw(2), state, row, row, row],
        out_specs=[pl.BlockSpec((None, rows, WIDTH), lambda b, t: (b, t, 0)), state],
        out_shape=[jax.ShapeDtypeStruct((bsz, seq, WIDTH), BF16),
                   jax.ShapeDtypeStruct((bsz, N_HEADS, HEAD_DIM, HEAD_DIM), F32)],
        scratch_shapes=[pltpu.VMEM((3, SUBLANES, WIDTH), F32)],
        compiler_params=pltpu.CompilerParams(
            dimension_semantics=("parallel", "arbitrary"), vmem_limit_bytes=VMEM_LIMIT),
        name=name,
    )(pre, pre, pre, pre, ba, conv8, conv8, conv8, conv_w, conv_w, conv_w, s0,
      alog_row, dtb_row, nw_row)


def _split3(x):
    hi = x.astype(BF16)
    r = x - hi.astype(F32)
    mid = r.astype(BF16)
    lo = (r - mid.astype(F32)).astype(BF16)
    return hi, mid, lo


def _sb_block(qb, kb, vb, l_incl, causal, run, scale):
    z = _dot_nt(qb, kb) * scale
    l1m = -_softplus(z)
    if causal is not None:
        l1m = jnp.where(causal, l1m, 0.0)
    hi, mid, lo = _split3(l1m)
    cs = _dot(hi, l_incl) + _dot(mid, l_incl) + _dot(lo, l_incl)
    w = jnp.exp(z + cs + run)
    if causal is not None:
        w = jnp.where(causal, w, 0.0)
    return _dot(w.astype(BF16), vb), run + cs[:, 0:1]


def _suffix_ones(n):
    r = lax.broadcasted_iota(jnp.int32, (n, n), 0)
    c = lax.broadcasted_iota(jnp.int32, (n, n), 1)
    return (r >= c).astype(BF16)


def _sb_prompt_kernel(q_ref, k_ref, v_ref, o_ref, *, blk, scale):
    qi = pl.program_id(2)
    qb = q_ref[...].astype(BF16)
    l_incl = _suffix_ones(blk)
    r = lax.broadcasted_iota(jnp.int32, (blk, blk), 0)
    c = lax.broadcasted_iota(jnp.int32, (blk, blk), 1)

    def body(jj, carry):
        acc, run = carry
        j = qi - jj
        k0 = pl.multiple_of(j * blk, blk)
        kb = k_ref[pl.ds(k0, blk), :].astype(BF16)
        vb = v_ref[pl.ds(k0, blk), :].astype(BF16)
        causal = (c + k0) < (r + qi * blk)
        d, run = _sb_block(qb, kb, vb, l_incl, causal, run, scale)
        return acc + d, run

    acc, _ = lax.fori_loop(0, qi + 1, body,
                           (jnp.zeros((blk, HEAD_DIM), F32), jnp.zeros((blk, 1), F32)))
    o_ref[...] = acc.astype(o_ref.dtype)


def _sb_prompt(q, k, v, name):
    bsz, seq, _ = q.shape
    blk = _pick(seq, 256)
    kv = pl.BlockSpec((None, seq, HEAD_DIM), lambda b, h, i: (b, 0, h))
    qo = pl.BlockSpec((None, blk, HEAD_DIM), lambda b, h, i: (b, i, h))
    return pl.pallas_call(
        functools.partial(_sb_prompt_kernel, blk=blk, scale=HEAD_DIM ** -0.5),
        grid=(bsz, N_HEADS, seq // blk),
        in_specs=[qo, kv, kv],
        out_specs=qo,
        out_shape=jax.ShapeDtypeStruct((bsz, seq, WIDTH), BF16),
        compiler_params=pltpu.CompilerParams(
            dimension_semantics=("parallel", "parallel", "arbitrary"),
            vmem_limit_bytes=VMEM_LIMIT),
        name=name,
    )(q, k, v)


def _sb_sample_kernel(q_ref, kn_ref, vn_ref, kc_ref, vc_ref, o_ref, *, seq, past, blk, scale):
    qb = q_ref[...].astype(BF16)
    r = lax.broadcasted_iota(jnp.int32, (seq, seq), 0)
    c = lax.broadcasted_iota(jnp.int32, (seq, seq), 1)
    run = jnp.zeros((seq, 1), F32)
    acc, run = _sb_block(qb, kn_ref[...].astype(BF16), vn_ref[...].astype(BF16),
                         _suffix_ones(seq), c < r, run, scale)
    l_incl = _suffix_ones(blk)
    for j in reversed(range(past // blk)):
        kb = kc_ref[j * blk:(j + 1) * blk, :].astype(BF16)
        vb = vc_ref[j * blk:(j + 1) * blk, :].astype(BF16)
        d, run = _sb_block(qb, kb, vb, l_incl, None, run, scale)
        acc = acc + d
    o_ref[...] = acc.astype(o_ref.dtype)


def _sb_sample(q, kn, vn, kc, vc, name):
    bsz, seq, _ = q.shape
    past = kc.shape[1]
    blk = _pick(past, 256)
    new = pl.BlockSpec((None, seq, HEAD_DIM), lambda b, h: (b, 0, h))
    old = pl.BlockSpec((None, past, HEAD_DIM), lambda b, h: (b, 0, h))
    return pl.pallas_call(
        functools.partial(_sb_sample_kernel, seq=seq, past=past, blk=blk, scale=HEAD_DIM ** -0.5),
        grid=(bsz, N_HEADS),
        in_specs=[new, new, new, old, old],
        out_specs=new,
        out_shape=jax.ShapeDtypeStruct((bsz, seq, WIDTH), BF16),
        compiler_params=pltpu.CompilerParams(
            dimension_semantics=("parallel", "parallel"), vmem_limit_bytes=VMEM_LIMIT),
        name=name,
    )(q, kn, vn, kc, vc)


def _layer_norm(r, g, b):
    mu = jnp.mean(r, axis=-1, keepdims=True)
    d = r - mu
    var = jnp.mean(d * d, axis=-1, keepdims=True)
    return d * lax.rsqrt(var + LN_EPS) * g + b


def _outproj_kernel(oa_ref, ob_ref, x_ref, wa_ref, wb_ref, g_ref, b_ref, y_ref, *, alpha):
    acc = _dot(oa_ref[...], wa_ref[...]) + _dot(ob_ref[...], wb_ref[...])
    y_ref[...] = _layer_norm(alpha * x_ref[...] + acc, g_ref[...], b_ref[...])


def _outproj_ln(oa, ob, x, wa, wb, g, b, alpha, name):
    m, d = x.shape
    tm = _pick(m, 256)
    rows = lambda w: pl.BlockSpec((tm, w), lambda i: (i, 0))
    full = lambda a: pl.BlockSpec(a.shape, lambda i: (0, 0))
    return pl.pallas_call(
        functools.partial(_outproj_kernel, alpha=alpha),
        grid=(m // tm,),
        in_specs=[rows(WIDTH), rows(WIDTH), rows(d), full(wa), full(wb), full(g), full(b)],
        out_specs=rows(d),
        out_shape=jax.ShapeDtypeStruct((m, d), F32),
        compiler_params=pltpu.CompilerParams(
            dimension_semantics=("parallel",), vmem_limit_bytes=VMEM_LIMIT),
        name=name,
    )(oa, ob, x, wa, wb, g, b)


def _ffn_kernel(x_ref, wu_ref, wd_ref, g_ref, b_ref, y_ref, xb_ref, acc_ref, *, alpha):
    j = pl.program_id(1)

    @pl.when(j == 0)
    def _():
        xb_ref[...] = x_ref[...].astype(BF16)
        acc_ref[...] = jnp.zeros_like(acc_ref)

    hid = jnp.maximum(_dot(xb_ref[...], wu_ref[...]), 0.0)
    acc_ref[...] += _dot((hid * hid).astype(BF16), wd_ref[...])

    @pl.when(j == pl.num_programs(1) - 1)
    def _():
        y_ref[...] = _layer_norm(alpha * x_ref[...] + acc_ref[...], g_ref[...], b_ref[...])


def _ffn_ln(x, wu, wd, g, b, alpha, name):
    m, d = x.shape
    dff = wu.shape[1]
    tm = _pick(m, 512)
    tf = _pick(dff, 1024)
    return pl.pallas_call(
        functools.partial(_ffn_kernel, alpha=alpha),
        grid=(m // tm, dff // tf),
        in_specs=[pl.BlockSpec((tm, d), lambda i, j: (i, 0)),
                  pl.BlockSpec((d, tf), lambda i, j: (0, j)),
                  pl.BlockSpec((tf, d), lambda i, j: (j, 0)),
                  pl.BlockSpec((1, d), lambda i, j: (0, 0)),
                  pl.BlockSpec((1, d), lambda i, j: (0, 0))],
        out_specs=pl.BlockSpec((tm, d), lambda i, j: (i, 0)),
        out_shape=jax.ShapeDtypeStruct((m, d), F32),
        scratch_shapes=[pltpu.VMEM((tm, d), BF16), pltpu.VMEM((tm, d), F32)],
        compiler_params=pltpu.CompilerParams(
            dimension_semantics=("parallel", "arbitrary"), vmem_limit_bytes=VMEM_LIMIT),
        name=name,
    )(x, wu, wd, g, b)


def _layer(x, conv_buf, s0, k_past, v_past, wts, alpha, tag):
    (w_a, w_ba, w_q, w_k, w_v, conv_w, alog_row, dtb_row, nw_row,
     wo_a, wo_b, ln1_g, ln1_b, w_up, w_down, ln2_g, ln2_b) = wts
    bsz, seq, d = x.shape
    m = bsz * seq
    x2 = x.reshape(m, d)
    xb = x2.astype(BF16)

    pre = _matmul(xb, w_a, "proj_gdn_" + tag).reshape(bsz, seq, 4 * WIDTH)
    ba = _matmul(xb, w_ba, "proj_gate_" + tag).reshape(bsz, seq, LANES)
    q_b = _matmul(xb, w_q, "proj_sbq_" + tag).reshape(bsz, seq, WIDTH)
    k_b = _matmul(xb, w_k, "proj_sbk_" + tag).reshape(bsz, seq, WIDTH)
    v_b = _matmul(xb, w_v, "proj_sbv_" + tag).reshape(bsz, seq, WIDTH)

    conv8 = jnp.pad(conv_buf, ((0, 0), (SUBLANES - (CONV_W - 1), 0), (0, 0)))
    o_a, s_new = _gdn(pre, ba, conv8, s0, conv_w, alog_row, dtb_row, nw_row, "gdn_" + tag)
    hist = jnp.concatenate([conv_buf, pre[:, :, :3 * WIDTH]], axis=1) if seq < CONV_W - 1 \
        else pre[:, :, :3 * WIDTH]
    conv_new = hist[:, -(CONV_W - 1):, :]

    if k_past is None:
        o_b = _sb_prompt(q_b, k_b, v_b, "sb_" + tag)
    else:
        past = k_past.shape[1]
        o_b = _sb_sample(q_b, k_b, v_b, k_past.reshape(bsz, past, WIDTH),
                         v_past.reshape(bsz, past, WIDTH), "sb_" + tag)

    x1 = _outproj_ln(o_a.reshape(m, WIDTH), o_b.reshape(m, WIDTH), x2, wo_a, wo_b,
                     ln1_g, ln1_b, alpha, "outproj_" + tag)
    y = _ffn_ln(x1, w_up, w_down, ln2_g, ln2_b, alpha, "ffn_" + tag)
    return (y.reshape(bsz, seq, d), conv_new, s_new,
            k_b.reshape(bsz, seq, N_HEADS, HEAD_DIM), v_b.reshape(bsz, seq, N_HEADS, HEAD_DIM))


def kernel(x_prompt, x_sample, state_gdn_conv, state_gdn_S, cache_sb_k, cache_sb_v, w_in, conv_w, a_log, dt_bias, gdn_norm_w, w_out, ln1_g, ln1_b, w_up, w_down, ln2_g, ln2_b):
    depth = w_in.shape[0]
    alpha = (2 * depth) ** 0.25
    off_b = 4 * WIDTH
    off_sb = off_b + 2 * N_HEADS
    yp, ys = x_prompt, x_sample
    outs = [[] for _ in range(8)]
    for l in range(depth):
        wl = w_in[l]
        w_ba = jnp.pad(wl[:, off_b:off_sb], ((0, 0), (0, LANES - 2 * N_HEADS)))
        pad_row = lambda v: jnp.pad(v.astype(F32), (N_HEADS, LANES - 2 * N_HEADS)).reshape(1, LANES)
        wts = (wl[:, :off_b].astype(BF16), w_ba.astype(BF16),
               wl[:, off_sb:off_sb + WIDTH].astype(BF16),
               wl[:, off_sb + WIDTH:off_sb + 2 * WIDTH].astype(BF16),
               wl[:, off_sb + 2 * WIDTH:].astype(BF16),
               conv_w[l], pad_row(a_log[l]), pad_row(dt_bias[l]),
               gdn_norm_w[l].astype(F32).reshape(1, HEAD_DIM),
               w_out[l, :WIDTH].astype(BF16), w_out[l, WIDTH:].astype(BF16),
               ln1_g[l].reshape(1, -1), ln1_b[l].reshape(1, -1),
               w_up[l].astype(BF16), w_down[l].astype(BF16),
               ln2_g[l].reshape(1, -1), ln2_b[l].reshape(1, -1))
        bp = yp.shape[0]
        zero_conv = jnp.zeros((bp, CONV_W - 1, 3 * WIDTH), F32)
        zero_s = jnp.zeros((bp, N_HEADS, HEAD_DIM, HEAD_DIM), F32)
        yp, c1, s1, k1, v1 = _layer(yp, zero_conv, zero_s, None, None, wts, alpha, "p%d" % l)
        ys, c2, s2, k2, v2 = _layer(ys, state_gdn_conv[l], state_gdn_S[l],
                                    cache_sb_k[l], cache_sb_v[l], wts, alpha, "s%d" % l)
        for lst, val in zip(outs, (c1, s1, k1, v1, c2, s2, k2, v2)):
            lst.append(val)
    return (yp, ys) + tuple(jnp.stack(o) for o in outs)
```

```python
import functools

import jax
import jax.numpy as jnp
from jax import lax
from jax.experimental import pallas as pl
from jax.experimental.pallas import tpu as pltpu

HEAD_DIM = 128
N_HEADS = 8
WIDTH = N_HEADS * HEAD_DIM
CONV_W = 4
CHUNK = 64
LN_EPS = 1e-5
RMS_EPS = 1e-6
L2_EPS = 1e-6
LANES = 128
SUBLANES = 8
VMEM_LIMIT = 56 * 1024 * 1024

F32 = jnp.float32
BF16 = jnp.bfloat16


def _pick(n, pref):
    if n <= pref:
        return n
    t = pref
    while n % t:
        t //= 2
    return t


def _dot(a, b):
    return jnp.dot(a, b, preferred_element_type=F32)


def _dot_nt(a, b):
    return lax.dot_general(a, b, (((1,), (1,)), ((), ())), preferred_element_type=F32)


def _sigmoid(x):
    return 1.0 / (1.0 + jnp.exp(-x))


def _softplus(x):
    return jnp.maximum(x, 0.0) + jnp.log1p(jnp.exp(-jnp.abs(x)))


def _mm_kernel(x_ref, w_ref, o_ref):
    o_ref[...] = _dot(x_ref[...], w_ref[...]).astype(o_ref.dtype)


def _matmul(x, w, name):
    m, k = x.shape
    n = w.shape[1]
    tm = _pick(m, 512)
    tn = _pick(n, 1024)
    return pl.pallas_call(
        _mm_kernel,
        grid=(n // tn, m // tm),
        in_specs=[pl.BlockSpec((tm, k), lambda j, i: (i, 0)),
                  pl.BlockSpec((k, tn), lambda j, i: (0, j))],
        out_specs=pl.BlockSpec((tm, tn), lambda j, i: (i, j)),
        out_shape=jax.ShapeDtypeStruct((m, n), F32),
        compiler_params=pltpu.CompilerParams(
            dimension_semantics=("parallel", "parallel"), vmem_limit_bytes=VMEM_LIMIT),
        name=name,
    )(x, w)


NN = (((1,), (0,)), ((), ()))
NT = (((1,), (1,)), ((), ()))
TN = (((0,), (0,)), ((), ()))

GDN_ROWS_PER_STEP = 512
GDN_PIECES = dict(qk=1, inv=1, uw=1, scan=1)


def _pieces(x, n):
    out = []
    r = x
    for i in range(n):
        p = r.astype(BF16)
        out.append(p)
        if i + 1 < n:
            r = r - p.astype(F32)
    return out


def _mm(a, b, dims=NN):
    order = max(len(a), len(b))
    acc = None
    for i, ai in enumerate(a):
        for j, bj in enumerate(b):
            if i + j < order:
                t = lax.dot_general(ai, bj, dims, preferred_element_type=F32)
                acc = t if acc is None else acc + t
    return acc


def _gdn_kernel(q_ref, k_ref, v_ref, z_ref, ba_ref, cq_ref, ck_ref, cv_ref,
                wq_ref, wk_ref, wv_ref, s0_ref, alog_ref, dtb_ref, nw_ref,
                o_ref, s_ref, tail_ref, *, rows, chunk):
    C = chunk
    heads = range(N_HEADS)
    row = lax.broadcasted_iota(jnp.int32, (C, C), 0)
    col = lax.broadcasted_iota(jnp.int32, (C, C), 1)
    incl = row >= col
    strict = row > col
    upper = row <= col
    l_incl = [incl.astype(BF16)]
    ones = [jnp.ones((C, C), BF16)]
    eye = (row == col).astype(F32)
    lane = lax.broadcasted_iota(jnp.int32, (C, LANES), 1)
    level_masks = []
    b = 1
    while b < C:
        s = b.bit_length() - 1
        same = (row >> (s + 1)) == (col >> (s + 1))
        level_masks.append(same & (((row >> s) & 1) == 1) & (((col >> s) & 1) == 0))
        b *= 2
    n_qk, n_inv, n_uw, n_scan = (GDN_PIECES[c] for c in ("qk", "inv", "uw", "scan"))

    neg_a = -jnp.exp(alog_ref[...])
    dtb = dtb_ref[...]
    nw = nw_ref[...]
    wq = wq_ref[...]
    wk = wk_ref[...]
    wv = wv_ref[...]

    @pl.when(pl.program_id(1) == 0)
    def _():
        s_ref[...] = s0_ref[...]
        tail_ref[0] = cq_ref[...]
        tail_ref[1] = ck_ref[...]
        tail_ref[2] = cv_ref[...]

    def conv(e, w):
        acc = e[5:5 + C] * w[0:1]
        for i in range(1, CONV_W):
            acc = acc + e[5 + i:5 + i + C] * w[i:i + 1]
        return acc * _sigmoid(acc)

    def head(x, i):
        return x[:, i * HEAD_DIM:(i + 1) * HEAD_DIM]

    def body(n, carry):
        tq, tk, tv = carry
        c0 = pl.multiple_of(n * C, C)
        xq = q_ref[pl.ds(c0, C), :]
        xk = k_ref[pl.ds(c0, C), :]
        xv = v_ref[pl.ds(c0, C), :]
        q_all = conv(jnp.concatenate([tq, xq], axis=0), wq)
        k_all = conv(jnp.concatenate([tk, xk], axis=0), wk)
        v_all = conv(jnp.concatenate([tv, xv], axis=0), wv)
        zc = z_ref[pl.ds(c0, C), :]
        gate_all = zc * _sigmoid(zc)
        ba = ba_ref[pl.ds(c0, C), :]
        beta_all = _sigmoid(ba)
        g_all = neg_a * _softplus(ba + dtb)

        q = [head(q_all, i) for i in heads]
        k = [head(k_all, i) for i in heads]
        q = [x * lax.rsqrt(jnp.sum(x * x, axis=-1, keepdims=True) + L2_EPS) * (HEAD_DIM ** -0.5)
             for x in q]
        k = [x * lax.rsqrt(jnp.sum(x * x, axis=-1, keepdims=True) + L2_EPS) for x in k]
        beta = [jnp.sum(jnp.where(lane == i, beta_all, 0.0), axis=-1, keepdims=True) for i in heads]
        g = [jnp.sum(jnp.where(lane == i + N_HEADS, g_all, 0.0), axis=-1, keepdims=True)
             for i in heads]
        k_beta = [k[i] * beta[i] for i in heads]
        kp = [_pieces(k[i], n_qk) for i in heads]

        gcb = [_mm(l_incl, _pieces(jnp.broadcast_to(g[i], (C, LANES)), 3)) for i in heads]
        gc_row = [_mm(ones, _pieces(jnp.where(upper, jnp.broadcast_to(g[i], (C, C)), 0.0), 3))
                  for i in heads]
        kk = [_mm(_pieces(k_beta[i], n_qk), kp[i], NT) for i in heads]
        qk = [_mm(_pieces(q[i], n_qk), kp[i], NT) for i in heads]

        decay = [jnp.where(incl, jnp.exp(gcb[i][:, :C] - gc_row[i]), 0.0) for i in heads]
        egc = [jnp.exp(gcb[i]) for i in heads]
        g_last = [gcb[i][C - 1:C, :] for i in heads]
        m = [jnp.where(strict, kk[i] * decay[i], 0.0) for i in heads]
        attn = [qk[i] * decay[i] for i in heads]

        t_inv = [eye - jnp.where(level_masks[0], m[i], 0.0) for i in heads]
        for mask in level_masks[1:]:
            tp = [_pieces(t_inv[i], n_inv) for i in heads]
            x = [_mm(tp[i], _pieces(jnp.where(mask, m[i], 0.0), n_inv)) for i in heads]
            y = [_mm(_pieces(x[i], n_inv), tp[i]) for i in heads]
            t_inv = [t_inv[i] - y[i] for i in heads]

        rhs = [jnp.concatenate([head(v_all, i) * beta[i], k_beta[i] * egc[i]], axis=1) for i in heads]
        uw = [_mm(_pieces(t_inv[i], n_uw), _pieces(rhs[i], n_uw)) for i in heads]

        s_old = [s_ref[i] for i in heads]
        lhs = [jnp.concatenate([uw[i][:, HEAD_DIM:], q[i] * egc[i]], axis=0) for i in heads]
        ws = [_mm(_pieces(lhs[i], n_scan), _pieces(s_old[i], n_scan)) for i in heads]
        v_new = [uw[i][:, :HEAD_DIM] - ws[i][:C] for i in heads]
        vp = [_pieces(v_new[i], n_scan) for i in heads]
        k_tail = [k[i] * jnp.exp(g_last[i] - gcb[i]) for i in heads]

        o = [ws[i][C:] + _mm(_pieces(attn[i], n_scan), vp[i]) for i in heads]
        s_new = [s_old[i] * jnp.exp(g_last[i]) + _mm(_pieces(k_tail[i], n_scan), vp[i], TN)
                 for i in heads]
        for i in heads:
            s_ref[i] = s_new[i]
            on = (o[i] * lax.rsqrt(jnp.mean(o[i] * o[i], axis=-1, keepdims=True) + RMS_EPS)
                  * nw * head(gate_all, i))
            o_ref[pl.ds(c0, C), i * HEAD_DIM:(i + 1) * HEAD_DIM] = on.astype(o_ref.dtype)
        return xq[C - 8:C], xk[C - 8:C], xv[C - 8:C]

    tails = lax.fori_loop(0, rows // C, body, (tail_ref[0], tail_ref[1], tail_ref[2]))
    for j in range(3):
        tail_ref[j] = tails[j]


def _gdn(pre, ba, conv8, s0, conv_w, alog_row, dtb_row, nw_row, name):
    bsz, seq, _ = pre.shape
    chunk = CHUNK if seq % CHUNK == 0 else seq
    rows = _pick(seq, GDN_ROWS_PER_STEP)

    def col(grp):
        return pl.BlockSpec((None, rows, WIDTH), lambda b, t: (b, t, grp))

    def st(grp):
        return pl.BlockSpec((None, SUBLANES, WIDTH), lambda b, t: (b, 0, grp))

    def cw(grp):
        return pl.BlockSpec((CONV_W, WIDTH), lambda b, t: (0, grp))

    row = pl.BlockSpec((1, LANES), lambda b, t: (0, 0))
    state = pl.BlockSpec((None, N_HEADS, HEAD_DIM, HEAD_DIM), lambda b, t: (b, 0, 0, 0))
    return pl.pallas_call(
        functools.partial(_gdn_kernel, rows=rows, chunk=chunk),
        grid=(bsz, seq // rows),
        in_specs=[col(0), col(1), col(2), col(3),
                  pl.BlockSpec((None, rows, LANES), lambda b, t: (b, t, 0)),
                  st(0), st(1), st(2), cw(0), cw(1), cw(2), state, row, row, row],
        out_specs=[pl.BlockSpec((None, rows, WIDTH), lambda b, t: (b, t, 0)), state],
        out_shape=[jax.ShapeDtypeStruct((bsz, seq, WIDTH), BF16),
                   jax.ShapeDtypeStruct((bsz, N_HEADS, HEAD_DIM, HEAD_DIM), F32)],
        scratch_shapes=[pltpu.VMEM((3, SUBLANES, WIDTH), F32)],
        compiler_params=pltpu.CompilerParams(
            dimension_semantics=("parallel", "arbitrary"), vmem_limit_bytes=VMEM_LIMIT),
        name=name,
    )(pre, pre, pre, pre, ba, conv8, conv8, conv8, conv_w, conv_w, conv_w, s0,
      alog_row, dtb_row, nw_row)


SB_KEYS = 128
SB_DEAD = -104.0
SB_BOUND_SLACK = 1.05


def _suffix_and_total():
    r = lax.broadcasted_iota(jnp.int32, (SB_KEYS, 2 * SB_KEYS), 0)
    c = lax.broadcasted_iota(jnp.int32, (SB_KEYS, 2 * SB_KEYS), 1)
    return ((r >= c) | (c >= SB_KEYS)).astype(BF16)


def _sb_tile(z, mask, run, lr):
    l1m = -_softplus(z)
    if mask is not None:
        l1m = jnp.where(mask, l1m, 0.0)
    hi = l1m.astype(BF16)
    lo = (l1m - hi.astype(F32)).astype(BF16)
    n = z.shape[1] // SB_KEYS
    tiles = [slice(s * SB_KEYS, (s + 1) * SB_KEYS) for s in range(n)]
    ct = [_dot(hi[:, t], lr) + _dot(lo[:, t], lr) for t in tiles]
    cs = [None] * n
    for s in reversed(range(n)):
        cs[s] = ct[s][:, :SB_KEYS] + run
        run = run + ct[s][:, SB_KEYS:]
    w = jnp.exp(z + (cs[0] if n == 1 else jnp.concatenate(cs, axis=1)))
    if mask is not None:
        w = jnp.where(mask, w, 0.0)
    return w.astype(BF16), run


def _sb_prompt_kernel(q_ref, k_ref, v_ref, o_ref, kb_ref, vb_ref, kmax_ref, *, tq, scale):
    qi = pl.program_id(2)
    ones = jnp.ones((HEAD_DIM, LANES), BF16)

    @pl.when(qi == 0)
    def _():
        k = k_ref[...]
        kb_ref[...] = k.astype(BF16)
        vb_ref[...] = v_ref[...].astype(BF16)
        ksq = _dot((k * k).astype(BF16), ones)
        kmax_ref[...] = jnp.broadcast_to(jnp.max(ksq, axis=0, keepdims=True), kmax_ref.shape)

    q = q_ref[...]
    qb = q.astype(BF16)
    zbound = jnp.sqrt(_dot((q * q).astype(BF16), ones) * kmax_ref[0:1, :]) * (scale * SB_BOUND_SLACK)
    lr = _suffix_and_total()
    r = lax.broadcasted_iota(jnp.int32, (tq, tq), 0)
    c = lax.broadcasted_iota(jnp.int32, (tq, tq), 1)

    def block(j, mask, acc, run):
        k0 = pl.multiple_of(j * tq, tq)
        z = _dot_nt(qb, kb_ref[pl.ds(k0, tq), :]) * scale
        wb, run = _sb_tile(z, mask, run, lr)
        return acc + _dot(wb, vb_ref[pl.ds(k0, tq), :]), run

    def alive(run):
        return jnp.max(run + zbound) > SB_DEAD

    acc, run = block(qi, c < r, jnp.zeros((tq, HEAD_DIM), F32), jnp.zeros((tq, LANES), F32))

    def cond(carry):
        return (carry[0] >= 0) & carry[3]

    def body(carry):
        j, acc, run, _ = carry
        acc, run = block(j, None, acc, run)
        return j - 1, acc, run, alive(run)

    _, acc, _, _ = lax.while_loop(cond, body, (qi - 1, acc, run, alive(run)))
    o_ref[...] = acc.astype(o_ref.dtype)


def _sb_prompt(q, k, v, name):
    bsz, seq, _ = q.shape
    tq = _pick(seq, 256)
    kv = pl.BlockSpec((None, seq, HEAD_DIM), lambda b, h, i: (b, 0, h))
    qo = pl.BlockSpec((None, tq, HEAD_DIM), lambda b, h, i: (b, i, h))
    return pl.pallas_call(
        functools.partial(_sb_prompt_kernel, tq=tq, scale=HEAD_DIM ** -0.5),
        grid=(bsz, N_HEADS, seq // tq),
        in_specs=[qo, kv, kv],
        out_specs=qo,
        out_shape=jax.ShapeDtypeStruct((bsz, seq, WIDTH), BF16),
        scratch_shapes=[pltpu.VMEM((seq, HEAD_DIM), BF16), pltpu.VMEM((seq, HEAD_DIM), BF16),
                        pltpu.VMEM((SUBLANES, LANES), F32)],
        compiler_params=pltpu.CompilerParams(
            dimension_semantics=("parallel", "parallel", "arbitrary"),
            vmem_limit_bytes=VMEM_LIMIT),
        name=name,
    )(q, k, v)


def _sb_sample_kernel(q_ref, kn_ref, vn_ref, kc_ref, vc_ref, o_ref, qb_ref, acc_ref, run_ref,
                      *, seq, keys, scale):
    step = pl.program_id(1)
    heads = range(N_HEADS)
    rows = N_HEADS * seq
    lr = _suffix_and_total()

    def sweep(k_tiles, v_tiles, mask):
        z = jnp.concatenate([_dot_nt(qb_ref[h * seq:(h + 1) * seq, :], k_tiles[h]) for h in heads],
                            axis=0) * scale
        wb, run = _sb_tile(z, mask, run_ref[...], lr)
        run_ref[...] = run
        pv = [_dot(wb[h * seq:(h + 1) * seq, :], v_tiles[h]) for h in heads]
        acc_ref[...] += jnp.concatenate(pv, axis=0)

    @pl.when(step == 0)
    def _():
        for h in heads:
            qb_ref[h * seq:(h + 1) * seq, :] = q_ref[:, h * HEAD_DIM:(h + 1) * HEAD_DIM].astype(BF16)
        acc_ref[...] = jnp.zeros_like(acc_ref)
        run_ref[...] = jnp.zeros_like(run_ref)
        pad = jnp.zeros((SB_KEYS - seq, HEAD_DIM), BF16)
        kn = [jnp.concatenate([kn_ref[:, h * HEAD_DIM:(h + 1) * HEAD_DIM].astype(BF16), pad], axis=0)
              for h in heads]
        vn = [jnp.concatenate([vn_ref[:, h * HEAD_DIM:(h + 1) * HEAD_DIM].astype(BF16), pad], axis=0)
              for h in heads]
        r = lax.broadcasted_iota(jnp.int32, (rows, SB_KEYS), 0)
        c = lax.broadcasted_iota(jnp.int32, (rows, SB_KEYS), 1)
        sweep(kn, vn, c < (r & (seq - 1)))

    kt = [kc_ref[pl.ds(h, keys, stride=N_HEADS), :].astype(BF16) for h in heads]
    vt = [vc_ref[pl.ds(h, keys, stride=N_HEADS), :].astype(BF16) for h in heads]
    sweep(kt, vt, None)

    @pl.when(step == pl.num_programs(1) - 1)
    def _():
        for h in heads:
            o_ref[:, h * HEAD_DIM:(h + 1) * HEAD_DIM] = acc_ref[h * seq:(h + 1) * seq, :].astype(o_ref.dtype)


def _sb_sample(q, kn, vn, kc, vc, name):
    bsz, seq, _ = q.shape
    past = kc.shape[1]
    kc = kc.reshape(bsz, past * N_HEADS, HEAD_DIM)
    vc = vc.reshape(bsz, past * N_HEADS, HEAD_DIM)
    assert seq & (seq - 1) == 0 and seq <= SB_KEYS and past % SB_KEYS == 0
    keys = _pick(past, 512)
    nsteps = past // keys
    new = pl.BlockSpec((None, seq, WIDTH), lambda b, j: (b, 0, 0))
    old = pl.BlockSpec((None, keys * N_HEADS, HEAD_DIM), lambda b, j: (b, nsteps - 1 - j, 0))
    rows = N_HEADS * seq
    return pl.pallas_call(
        functools.partial(_sb_sample_kernel, seq=seq, keys=keys, scale=HEAD_DIM ** -0.5),
        grid=(bsz, nsteps),
        in_specs=[new, new, new, old, old],
        out_specs=new,
        out_shape=jax.ShapeDtypeStruct((bsz, seq, WIDTH), BF16),
        scratch_shapes=[pltpu.VMEM((rows, HEAD_DIM), BF16), pltpu.VMEM((rows, HEAD_DIM), F32),
                        pltpu.VMEM((rows, LANES), F32)],
        compiler_params=pltpu.CompilerParams(
            dimension_semantics=("parallel", "arbitrary"), vmem_limit_bytes=VMEM_LIMIT),
        name=name,
    )(q, kn, vn, kc, vc)


def _layer_norm(r, g, b):
    mu = jnp.mean(r, axis=-1, keepdims=True)
    d = r - mu
    var = jnp.mean(d * d, axis=-1, keepdims=True)
    return d * lax.rsqrt(var + LN_EPS) * g + b


def _outproj_kernel(oa_ref, ob_ref, x_ref, wa_ref, wb_ref, g_ref, b_ref, y_ref, *, alpha):
    acc = _dot(oa_ref[...], wa_ref[...]) + _dot(ob_ref[...], wb_ref[...])
    y_ref[...] = _layer_norm(alpha * x_ref[...] + acc, g_ref[...], b_ref[...])


def _outproj_ln(oa, ob, x, wa, wb, g, b, alpha, name):
    m, d = x.shape
    tm = _pick(m, 256)
    rows = lambda w: pl.BlockSpec((tm, w), lambda i: (i, 0))
    full = lambda a: pl.BlockSpec(a.shape, lambda i: (0, 0))
    return pl.pallas_call(
        functools.partial(_outproj_kernel, alpha=alpha),
        grid=(m // tm,),
        in_specs=[rows(WIDTH), rows(WIDTH), rows(d), full(wa), full(wb), full(g), full(b)],
        out_specs=rows(d),
        out_shape=jax.ShapeDtypeStruct((m, d), F32),
        compiler_params=pltpu.CompilerParams(
            dimension_semantics=("parallel",), vmem_limit_bytes=VMEM_LIMIT),
        name=name,
    )(oa, ob, x, wa, wb, g, b)


def _ffn_kernel(x_ref, wu_ref, wd_ref, g_ref, b_ref, y_ref, xb_ref, acc_ref, *, alpha):
    j = pl.program_id(1)

    @pl.when(j == 0)
    def _():
        xb_ref[...] = x_ref[...].astype(BF16)
        acc_ref[...] = jnp.zeros_like(acc_ref)

    hid = jnp.maximum(_dot(xb_ref[...], wu_ref[...]), 0.0)
    acc_ref[...] += _dot((hid * hid).astype(BF16), wd_ref[...])

    @pl.when(j == pl.num_programs(1) - 1)
    def _():
        y_ref[...] = _layer_norm(alpha * x_ref[...] + acc_ref[...], g_ref[...], b_ref[...])


def _ffn_ln(x, wu, wd, g, b, alpha, name):
    m, d = x.shape
    dff = wu.shape[1]
    tm = _pick(m, 512)
    tf = _pick(dff, 1024)
    return pl.pallas_call(
        functools.partial(_ffn_kernel, alpha=alpha),
        grid=(m // tm, dff // tf),
        in_specs=[pl.BlockSpec((tm, d), lambda i, j: (i, 0)),
                  pl.BlockSpec((d, tf), lambda i, j: (0, j)),
                  pl.BlockSpec((tf, d), lambda i, j: (j, 0)),
                  pl.BlockSpec((1, d), lambda i, j: (0, 0)),
                  pl.BlockSpec((1, d), lambda i, j: (0, 0))],
        out_specs=pl.BlockSpec((tm, d), lambda i, j: (i, 0)),
        out_shape=jax.ShapeDtypeStruct((m, d), F32),
        scratch_shapes=[pltpu.VMEM((tm, d), BF16), pltpu.VMEM((tm, d), F32)],
        compiler_params=pltpu.CompilerParams(
            dimension_semantics=("parallel", "arbitrary"), vmem_limit_bytes=VMEM_LIMIT),
        name=name,
    )(x, wu, wd, g, b)


def _layer(x, conv_buf, s0, k_past, v_past, wts, alpha, tag):
    (w_a, w_ba, w_q, w_k, w_v, conv_w, alog_row, dtb_row, nw_row,
     wo_a, wo_b, ln1_g, ln1_b, w_up, w_down, ln2_g, ln2_b) = wts
    bsz, seq, d = x.shape
    m = bsz * seq
    x2 = x.reshape(m, d)
    xb = x2.astype(BF16)

    pre = _matmul(xb, w_a, "proj_gdn_" + tag).reshape(bsz, seq, 4 * WIDTH)
    ba = _matmul(xb, w_ba, "proj_gate_" + tag).reshape(bsz, seq, LANES)
    q_b = _matmul(xb, w_q, "proj_sbq_" + tag).reshape(bsz, seq, WIDTH)
    k_b = _matmul(xb, w_k, "proj_sbk_" + tag).reshape(bsz, seq, WIDTH)
    v_b = _matmul(xb, w_v, "proj_sbv_" + tag).reshape(bsz, seq, WIDTH)

    conv8 = jnp.pad(conv_buf, ((0, 0), (SUBLANES - (CONV_W - 1), 0), (0, 0)))
    o_a, s_new = _gdn(pre, ba, conv8, s0, conv_w, alog_row, dtb_row, nw_row, "gdn_" + tag)
    hist = jnp.concatenate([conv_buf, pre[:, :, :3 * WIDTH]], axis=1) if seq < CONV_W - 1 \
        else pre[:, :, :3 * WIDTH]
    conv_new = hist[:, -(CONV_W - 1):, :]

    if k_past is None:
        o_b = _sb_prompt(q_b, k_b, v_b, "sb_" + tag)
    else:
        o_b = _sb_sample(q_b, k_b, v_b, k_past, v_past, "sb_" + tag)

    x1 = _outproj_ln(o_a.reshape(m, WIDTH), o_b.reshape(m, WIDTH), x2, wo_a, wo_b,
                     ln1_g, ln1_b, alpha, "outproj_" + tag)
    y = _ffn_ln(x1, w_up, w_down, ln2_g, ln2_b, alpha, "ffn_" + tag)
    return (y.reshape(bsz, seq, d), conv_new, s_new,
            k_b.reshape(bsz, seq, N_HEADS, HEAD_DIM), v_b.reshape(bsz, seq, N_HEADS, HEAD_DIM))


def kernel(x_prompt, x_sample, state_gdn_conv, state_gdn_S, cache_sb_k, cache_sb_v, w_in, conv_w, a_log, dt_bias, gdn_norm_w, w_out, ln1_g, ln1_b, w_up, w_down, ln2_g, ln2_b):
    depth = w_in.shape[0]
    alpha = (2 * depth) ** 0.25
    off_b = 4 * WIDTH
    off_sb = off_b + 2 * N_HEADS
    yp, ys = x_prompt, x_sample
    outs = [[] for _ in range(8)]
    for l in range(depth):
        wl = w_in[l]
        w_ba = jnp.pad(wl[:, off_b:off_sb], ((0, 0), (0, LANES - 2 * N_HEADS)))
        pad_row = lambda v: jnp.pad(v.astype(F32), (N_HEADS, LANES - 2 * N_HEADS)).reshape(1, LANES)
        wts = (wl[:, :off_b].astype(BF16), w_ba.astype(BF16),
               wl[:, off_sb:off_sb + WIDTH].astype(BF16),
               wl[:, off_sb + WIDTH:off_sb + 2 * WIDTH].astype(BF16),
               wl[:, off_sb + 2 * WIDTH:].astype(BF16),
               conv_w[l], pad_row(a_log[l]), pad_row(dt_bias[l]),
               gdn_norm_w[l].astype(F32).reshape(1, HEAD_DIM),
               w_out[l, :WIDTH].astype(BF16), w_out[l, WIDTH:].astype(BF16),
               ln1_g[l].reshape(1, -1), ln1_b[l].reshape(1, -1),
               w_up[l].astype(BF16), w_down[l].astype(BF16),
               ln2_g[l].reshape(1, -1), ln2_b[l].reshape(1, -1))
        bp = yp.shape[0]
        zero_conv = jnp.zeros((bp, CONV_W - 1, 3 * WIDTH), F32)
        zero_s = jnp.zeros((bp, N_HEADS, HEAD_DIM, HEAD_DIM), F32)
        yp, c1, s1, k1, v1 = _layer(yp, zero_conv, zero_s, None, None, wts, alpha, "p%d" % l)
        ys, c2, s2, k2, v2 = _layer(ys, state_gdn_conv[l], state_gdn_S[l],
                                    cache_sb_k[l], cache_sb_v[l], wts, alpha, "s%d" % l)
        for lst, val in zip(outs, (c1, s1, k1, v1, c2, s2, k2, v2)):
            lst.append(val)
    return (yp, ys) + tuple(jnp.stack(o) for o in outs)
```

```python
import functools

import jax
import jax.numpy as jnp
from jax import lax
from jax.experimental import pallas as pl
from jax.experimental.pallas import tpu as pltpu

HEAD_DIM = 128
N_HEADS = 8
WIDTH = N_HEADS * HEAD_DIM
CONV_W = 4
CHUNK = 128
LN_EPS = 1e-5
RMS_EPS = 1e-6
L2_EPS = 1e-6
LANES = 128
SUBLANES = 8
VMEM_LIMIT = 56 * 1024 * 1024

F32 = jnp.float32
BF16 = jnp.bfloat16


def _pick(n, pref):
    if n <= pref:
        return n
    t = pref
    while n % t:
        t //= 2
    return t


def _dot(a, b):
    return jnp.dot(a, b, preferred_element_type=F32)


def _dot_nt(a, b):
    return lax.dot_general(a, b, (((1,), (1,)), ((), ())), preferred_element_type=F32)


def _sigmoid(x):
    return 1.0 / (1.0 + jnp.exp(-x))


def _softplus(x):
    return jnp.maximum(x, 0.0) + jnp.log1p(jnp.exp(-jnp.abs(x)))


def _mm_kernel(x_ref, w_ref, o_ref):
    o_ref[...] = _dot(x_ref[...], w_ref[...]).astype(o_ref.dtype)


def _matmul(x, w, name):
    m, k = x.shape
    n = w.shape[1]
    tm = _pick(m, 512)
    tn = _pick(n, 1024)
    return pl.pallas_call(
        _mm_kernel,
        grid=(n // tn, m // tm),
        in_specs=[pl.BlockSpec((tm, k), lambda j, i: (i, 0)),
                  pl.BlockSpec((k, tn), lambda j, i: (0, j))],
        out_specs=pl.BlockSpec((tm, tn), lambda j, i: (i, j)),
        out_shape=jax.ShapeDtypeStruct((m, n), F32),
        compiler_params=pltpu.CompilerParams(
            dimension_semantics=("parallel", "parallel"), vmem_limit_bytes=VMEM_LIMIT),
        name=name,
    )(x, w)


NN = (((1,), (0,)), ((), ()))
NT = (((1,), (1,)), ((), ()))
TN = (((0,), (0,)), ((), ()))

GDN_ROWS_PER_STEP = 512
GDN_PIECES = dict(qk=1, inv=1, uw=1, scan=1)


def _pieces(x, n):
    out = []
    r = x
    for i in range(n):
        p = r.astype(BF16)
        out.append(p)
        if i + 1 < n:
            r = r - p.astype(F32)
    return out


def _mm(a, b, dims=NN):
    order = max(len(a), len(b))
    acc = None
    for i, ai in enumerate(a):
        for j, bj in enumerate(b):
            if i + j < order:
                t = lax.dot_general(ai, bj, dims, preferred_element_type=F32)
                acc = t if acc is None else acc + t
    return acc


def _gdn_kernel(q_ref, k_ref, v_ref, z_ref, ba_ref, cq_ref, ck_ref, cv_ref,
                wq_ref, wk_ref, wv_ref, s0_ref, alog_ref, dtb_ref, nw_ref,
                o_ref, s_ref, tail_ref, *, rows, chunk):
    C = chunk
    heads = range(N_HEADS)
    row = lax.broadcasted_iota(jnp.int32, (C, C), 0)
    col = lax.broadcasted_iota(jnp.int32, (C, C), 1)
    incl = row >= col
    strict = row > col
    upper = row <= col
    l_incl = [incl.astype(BF16)]
    ones = [jnp.ones((C, C), BF16)]
    eye = (row == col).astype(F32)
    lane = lax.broadcasted_iota(jnp.int32, (C, LANES), 1)
    level_masks = []
    b = 1
    while b < C:
        s = b.bit_length() - 1
        same = (row >> (s + 1)) == (col >> (s + 1))
        level_masks.append(same & (((row >> s) & 1) == 1) & (((col >> s) & 1) == 0))
        b *= 2
    n_qk, n_inv, n_uw, n_scan = (GDN_PIECES[c] for c in ("qk", "inv", "uw", "scan"))

    neg_a = -jnp.exp(alog_ref[...])
    dtb = dtb_ref[...]
    nw = nw_ref[...]
    wq = wq_ref[...]
    wk = wk_ref[...]
    wv = wv_ref[...]

    @pl.when(pl.program_id(1) == 0)
    def _():
        s_ref[...] = s0_ref[...]
        tail_ref[0] = cq_ref[...]
        tail_ref[1] = ck_ref[...]
        tail_ref[2] = cv_ref[...]

    def conv(e, w):
        acc = e[5:5 + C] * w[0:1]
        for i in range(1, CONV_W):
            acc = acc + e[5 + i:5 + i + C] * w[i:i + 1]
        return acc * _sigmoid(acc)

    def head(x, i):
        return x[:, i * HEAD_DIM:(i + 1) * HEAD_DIM]

    def body(n, carry):
        tq, tk, tv = carry
        c0 = pl.multiple_of(n * C, C)
        xq = q_ref[pl.ds(c0, C), :]
        xk = k_ref[pl.ds(c0, C), :]
        xv = v_ref[pl.ds(c0, C), :]
        q_all = conv(jnp.concatenate([tq, xq], axis=0), wq)
        k_all = conv(jnp.concatenate([tk, xk], axis=0), wk)
        v_all = conv(jnp.concatenate([tv, xv], axis=0), wv)
        zc = z_ref[pl.ds(c0, C), :]
        gate_all = zc * _sigmoid(zc)
        ba = ba_ref[pl.ds(c0, C), :]
        beta_all = _sigmoid(ba)
        g_all = neg_a * _softplus(ba + dtb)

        q = [head(q_all, i) for i in heads]
        k = [head(k_all, i) for i in heads]
        q = [x * lax.rsqrt(jnp.sum(x * x, axis=-1, keepdims=True) + L2_EPS) * (HEAD_DIM ** -0.5)
             for x in q]
        k = [x * lax.rsqrt(jnp.sum(x * x, axis=-1, keepdims=True) + L2_EPS) for x in k]
        beta = [jnp.sum(jnp.where(lane == i, beta_all, 0.0), axis=-1, keepdims=True) for i in heads]
        g = [jnp.sum(jnp.where(lane == i + N_HEADS, g_all, 0.0), axis=-1, keepdims=True)
             for i in heads]
        k_beta = [k[i] * beta[i] for i in heads]
        kp = [_pieces(k[i], n_qk) for i in heads]

        gcb = [_mm(l_incl, _pieces(jnp.broadcast_to(g[i], (C, LANES)), 3)) for i in heads]
        gc_row = [_mm(ones, _pieces(jnp.where(upper, jnp.broadcast_to(g[i], (C, C)), 0.0), 3))
                  for i in heads]
        kk = [_mm(_pieces(k_beta[i], n_qk), kp[i], NT) for i in heads]
        qk = [_mm(_pieces(q[i], n_qk), kp[i], NT) for i in heads]

        decay = [jnp.where(incl, jnp.exp(gcb[i][:, :C] - gc_row[i]), 0.0) for i in heads]
        egc = [jnp.exp(gcb[i]) for i in heads]
        g_last = [gcb[i][C - 1:C, :] for i in heads]
        m = [jnp.where(strict, kk[i] * decay[i], 0.0) for i in heads]
        attn = [qk[i] * decay[i] for i in heads]

        t_inv = [eye - jnp.where(level_masks[0], m[i], 0.0) for i in heads]
        for mask in level_masks[1:]:
            tp = [_pieces(t_inv[i], n_inv) for i in heads]
            x = [_mm(tp[i], _pieces(jnp.where(mask, m[i], 0.0), n_inv)) for i in heads]
            y = [_mm(_pieces(x[i], n_inv), tp[i]) for i in heads]
            t_inv = [t_inv[i] - y[i] for i in heads]

        rhs = [jnp.concatenate([head(v_all, i) * beta[i], k_beta[i] * egc[i]], axis=1) for i in heads]
        uw = [_mm(_pieces(t_inv[i], n_uw), _pieces(rhs[i], n_uw)) for i in heads]

        s_old = [s_ref[i] for i in heads]
        lhs = [jnp.concatenate([uw[i][:, HEAD_DIM:], q[i] * egc[i]], axis=0) for i in heads]
        ws = [_mm(_pieces(lhs[i], n_scan), _pieces(s_old[i], n_scan)) for i in heads]
        v_new = [uw[i][:, :HEAD_DIM] - ws[i][:C] for i in heads]
        vp = [_pieces(v_new[i], n_scan) for i in heads]
        k_tail = [k[i] * jnp.exp(g_last[i] - gcb[i]) for i in heads]

        o = [ws[i][C:] + _mm(_pieces(attn[i], n_scan), vp[i]) for i in heads]
        s_new = [s_old[i] * jnp.exp(g_last[i]) + _mm(_pieces(k_tail[i], n_scan), vp[i], TN)
                 for i in heads]
        for i in heads:
            s_ref[i] = s_new[i]
            on = (o[i] * lax.rsqrt(jnp.mean(o[i] * o[i], axis=-1, keepdims=True) + RMS_EPS)
                  * nw * head(gate_all, i))
            o_ref[pl.ds(c0, C), i * HEAD_DIM:(i + 1) * HEAD_DIM] = on.astype(o_ref.dtype)
        return xq[C - 8:C], xk[C - 8:C], xv[C - 8:C]

    tails = lax.fori_loop(0, rows // C, body, (tail_ref[0], tail_ref[1], tail_ref[2]))
    for j in range(3):
        tail_ref[j] = tails[j]


def _gdn(pre, ba, conv8, s0, conv_w, alog_row, dtb_row, nw_row, name):
    bsz, seq, _ = pre.shape
    chunk = CHUNK if seq % CHUNK == 0 else seq
    rows = _pick(seq, GDN_ROWS_PER_STEP)

    def col(grp):
        return pl.BlockSpec((None, rows, WIDTH), lambda b, t: (b, t, grp))

    def st(grp):
        return pl.BlockSpec((None, SUBLANES, WIDTH), lambda b, t: (b, 0, grp))

    def cw(grp):
        return pl.BlockSpec((CONV_W, WIDTH), lambda b, t: (0, grp))

    row = pl.BlockSpec((1, LANES), lambda b, t: (0, 0))
    state = pl.BlockSpec((None, N_HEADS, HEAD_DIM, HEAD_DIM), lambda b, t: (b, 0, 0, 0))
    return pl.pallas_call(
        functools.partial(_gdn_kernel, rows=rows, chunk=chunk),
        grid=(bsz, seq // rows),
        in_specs=[col(0), col(1), col(2), col(3),
                  pl.BlockSpec((None, rows, LANES), lambda b, t: (b, t, 0)),
                  st(0), st(1), st(2), cw(0), cw(1), cw(2), state, row, row, row],
        out_specs=[pl.BlockSpec((None, rows, WIDTH), lambda b, t: (b, t, 0)), state],
        out_shape=[jax.ShapeDtypeStruct((bsz, seq, WIDTH), BF16),
                   jax.ShapeDtypeStruct((bsz, N_HEADS, HEAD_DIM, HEAD_DIM), F32)],
        scratch_shapes=[pltpu.VMEM((3, SUBLANES, WIDTH), F32)],
        compiler_params=pltpu.CompilerParams(
            dimension_semantics=("parallel", "arbitrary"), vmem_limit_bytes=VMEM_LIMIT),
        name=name,
    )(pre, pre, pre, pre, ba, conv8, conv8, conv8, conv_w, conv_w, conv_w, s0,
      alog_row, dtb_row, nw_row)


SB_KEYS = 256
SB_DEAD = -104.0
SB_BOUND_SLACK = 1.05


def _suffix_ones(n):
    r = lax.broadcasted_iota(jnp.int32, (n, n), 0)
    c = lax.broadcasted_iota(jnp.int32, (n, n), 1)
    return (r >= c).astype(BF16)


def _sb_split(z, mask):
    l1m = -(jnp.maximum(z, 0.0) + jnp.log(1.0 + jnp.exp(-jnp.abs(z))))
    if mask is not None:
        l1m = jnp.where(mask, l1m, 0.0)
    hi = l1m.astype(BF16)
    return hi, (l1m - hi.astype(F32)).astype(BF16)


def _sb_sums(hi, lo, suffix):
    tk = suffix.shape[0]
    tiles = [slice(s * tk, (s + 1) * tk) for s in range(hi.shape[1] // tk)]
    return [_dot(hi[:, t], suffix) + _dot(lo[:, t], suffix) for t in tiles]


def _sb_weights(z, cs, mask, run):
    n = len(cs)
    reps = cs[0].shape[1] // LANES
    tot = [None] * n
    for s in reversed(range(n)):
        tot[s] = run if reps == 1 else jnp.concatenate([run] * reps, axis=1)
        run = run + jnp.broadcast_to(cs[s][:, 0:1], run.shape)
    off = [cs[s] + tot[s] for s in range(n)]
    w = jnp.exp(z + (off[0] if n == 1 else jnp.concatenate(off, axis=1)))
    if mask is not None:
        w = jnp.where(mask, w, 0.0)
    return w.astype(BF16), run


def _sb_prompt_kernel(q_ref, k_ref, v_ref, o_ref, kb_ref, vb_ref, kmax_ref, *, tq, hb, scale):
    qi = pl.program_id(2)
    heads = range(hb)
    ones = jnp.ones((HEAD_DIM, LANES), BF16)

    def head(x, h):
        return x[:, h * HEAD_DIM:(h + 1) * HEAD_DIM]

    @pl.when(qi == 0)
    def _():
        k = k_ref[...]
        v = v_ref[...]
        zeros = jnp.zeros((tq, HEAD_DIM), BF16)
        for h in heads:
            kh = head(k, h)
            kb_ref[h, 0:tq, :] = zeros
            vb_ref[h, 0:tq, :] = zeros
            kb_ref[h, tq:, :] = kh.astype(BF16)
            vb_ref[h, tq:, :] = head(v, h).astype(BF16)
            ksq = _dot((kh * kh).astype(BF16), ones)
            kmax_ref[h] = jnp.broadcast_to(jnp.max(ksq, axis=0, keepdims=True), kmax_ref.shape[1:])

    q = q_ref[...]
    qb = [head(q, h).astype(BF16) for h in heads]
    zbound = [jnp.sqrt(_dot((head(q, h) * head(q, h)).astype(BF16), ones) * kmax_ref[h, 0:1, :])
              * (scale * SB_BOUND_SLACK) for h in heads]
    lr = _suffix_ones(SB_KEYS)

    def sweep(row0, nkeys, mask, acc, run):
        z = [_dot_nt(qb[h], kb_ref[h, pl.ds(row0, nkeys), :]) * scale for h in heads]
        pieces = [_sb_split(z[h], mask) for h in heads]
        ct = [_sb_sums(*pieces[h], lr) for h in heads]
        wr = [_sb_weights(z[h], ct[h], mask, run[h]) for h in heads]
        acc = [acc[h] + _dot(wr[h][0], vb_ref[h, pl.ds(row0, nkeys), :]) for h in heads]
        return acc, [wr[h][1] for h in heads]

    def alive(run):
        live = jnp.max(run[0] + zbound[0]) > SB_DEAD
        for h in heads[1:]:
            live = live | (jnp.max(run[h] + zbound[h]) > SB_DEAD)
        return live

    r = lax.broadcasted_iota(jnp.int32, (tq, 2 * tq), 0)
    c = lax.broadcasted_iota(jnp.int32, (tq, 2 * tq), 1)
    mask = (c < r + tq) & (c + qi * tq >= tq)
    acc, run = sweep(pl.multiple_of(qi * tq, tq), 2 * tq, mask,
                     [jnp.zeros((tq, HEAD_DIM), F32) for _ in heads],
                     [jnp.zeros((tq, LANES), F32) for _ in heads])

    def cond(carry):
        return (carry[0] >= 0) & carry[3]

    def body(carry):
        j, acc, run, _ = carry
        acc, run = sweep(pl.multiple_of((j + 1) * tq, tq), tq, None, acc, run)
        return j - 1, acc, run, alive(run)

    _, acc, _, _ = lax.while_loop(cond, body, (qi - 2, acc, run, alive(run)))
    o_ref[...] = jnp.concatenate(acc, axis=1).astype(o_ref.dtype)


SB_HEADS_PER_STEP = 2


def _sb_prompt(q, k, v, name):
    bsz, seq, _ = q.shape
    tq = _pick(seq, 256)
    hb = SB_HEADS_PER_STEP
    wid = hb * HEAD_DIM
    kv = pl.BlockSpec((None, seq, wid), lambda b, h, i: (b, 0, h))
    qo = pl.BlockSpec((None, tq, wid), lambda b, h, i: (b, i, h))
    return pl.pallas_call(
        functools.partial(_sb_prompt_kernel, tq=tq, hb=hb, scale=HEAD_DIM ** -0.5),
        grid=(bsz, N_HEADS // hb, seq // tq),
        in_specs=[qo, kv, kv],
        out_specs=qo,
        out_shape=jax.ShapeDtypeStruct((bsz, seq, WIDTH), BF16),
        scratch_shapes=[pltpu.VMEM((hb, seq + tq, HEAD_DIM), BF16),
                        pltpu.VMEM((hb, seq + tq, HEAD_DIM), BF16),
                        pltpu.VMEM((hb, SUBLANES, LANES), F32)],
        compiler_params=pltpu.CompilerParams(
            dimension_semantics=("parallel", "parallel", "arbitrary"),
            vmem_limit_bytes=VMEM_LIMIT),
        name=name,
    )(q, k, v)


def _sb_sample_kernel(q_ref, kn_ref, vn_ref, kc_ref, vc_ref, o_ref, qb_ref, acc_ref, run_ref,
                      *, seq, keys, scale):
    step = pl.program_id(1)
    heads = range(N_HEADS)
    rows = N_HEADS * seq
    def sweep(k_tiles, v_tiles, mask, lr):
        z = jnp.concatenate([_dot_nt(qb_ref[h * seq:(h + 1) * seq, :], k_tiles[h]) for h in heads],
                            axis=0) * scale
        wb, run = _sb_weights(z, _sb_sums(*_sb_split(z, mask), lr), mask, run_ref[...])
        run_ref[...] = run
        pv = [_dot(wb[h * seq:(h + 1) * seq, :], v_tiles[h]) for h in heads]
        acc_ref[...] += jnp.concatenate(pv, axis=0)

    @pl.when(step == 0)
    def _():
        for h in heads:
            qb_ref[h * seq:(h + 1) * seq, :] = q_ref[:, h * HEAD_DIM:(h + 1) * HEAD_DIM].astype(BF16)
        acc_ref[...] = jnp.zeros_like(acc_ref)
        run_ref[...] = jnp.zeros_like(run_ref)
        pad = jnp.zeros((LANES - seq, HEAD_DIM), BF16)
        kn = [jnp.concatenate([kn_ref[:, h * HEAD_DIM:(h + 1) * HEAD_DIM].astype(BF16), pad], axis=0)
              for h in heads]
        vn = [jnp.concatenate([vn_ref[:, h * HEAD_DIM:(h + 1) * HEAD_DIM].astype(BF16), pad], axis=0)
              for h in heads]
        r = lax.broadcasted_iota(jnp.int32, (rows, LANES), 0)
        c = lax.broadcasted_iota(jnp.int32, (rows, LANES), 1)
        sweep(kn, vn, c < (r & (seq - 1)), _suffix_ones(LANES))

    kt = [kc_ref[pl.ds(h, keys, stride=N_HEADS), :].astype(BF16) for h in heads]
    vt = [vc_ref[pl.ds(h, keys, stride=N_HEADS), :].astype(BF16) for h in heads]
    sweep(kt, vt, None, _suffix_ones(SB_KEYS))

    @pl.when(step == pl.num_programs(1) - 1)
    def _():
        for h in heads:
            o_ref[:, h * HEAD_DIM:(h + 1) * HEAD_DIM] = acc_ref[h * seq:(h + 1) * seq, :].astype(o_ref.dtype)


def _sb_sample(q, kn, vn, kc, vc, name):
    bsz, seq, _ = q.shape
    past = kc.shape[1]
    kc = kc.reshape(bsz, past * N_HEADS, HEAD_DIM)
    vc = vc.reshape(bsz, past * N_HEADS, HEAD_DIM)
    assert seq & (seq - 1) == 0 and seq <= LANES and past % SB_KEYS == 0
    keys = _pick(past, 512)
    nsteps = past // keys
    new = pl.BlockSpec((None, seq, WIDTH), lambda b, j: (b, 0, 0))
    old = pl.BlockSpec((None, keys * N_HEADS, HEAD_DIM), lambda b, j: (b, nsteps - 1 - j, 0))
    rows = N_HEADS * seq
    return pl.pallas_call(
        functools.partial(_sb_sample_kernel, seq=seq, keys=keys, scale=HEAD_DIM ** -0.5),
        grid=(bsz, nsteps),
        in_specs=[new, new, new, old, old],
        out_specs=new,
        out_shape=jax.ShapeDtypeStruct((bsz, seq, WIDTH), BF16),
        scratch_shapes=[pltpu.VMEM((rows, HEAD_DIM), BF16), pltpu.VMEM((rows, HEAD_DIM), F32),
                        pltpu.VMEM((rows, LANES), F32)],
        compiler_params=pltpu.CompilerParams(
            dimension_semantics=("parallel", "arbitrary"), vmem_limit_bytes=VMEM_LIMIT),
        name=name,
    )(q, kn, vn, kc, vc)


def _layer_norm(r, g, b):
    mu = jnp.mean(r, axis=-1, keepdims=True)
    d = r - mu
    var = jnp.mean(d * d, axis=-1, keepdims=True)
    return d * lax.rsqrt(var + LN_EPS) * g + b


def _outproj_kernel(oa_ref, ob_ref, x_ref, wa_ref, wb_ref, g_ref, b_ref, y_ref, *, alpha):
    acc = _dot(oa_ref[...], wa_ref[...]) + _dot(ob_ref[...], wb_ref[...])
    y_ref[...] = _layer_norm(alpha * x_ref[...] + acc, g_ref[...], b_ref[...])


def _outproj_ln(oa, ob, x, wa, wb, g, b, alpha, name):
    m, d = x.shape
    tm = _pick(m, 512)
    rows = lambda w: pl.BlockSpec((tm, w), lambda i: (i, 0))
    full = lambda a: pl.BlockSpec(a.shape, lambda i: (0, 0))
    return pl.pallas_call(
        functools.partial(_outproj_kernel, alpha=alpha),
        grid=(m // tm,),
        in_specs=[rows(WIDTH), rows(WIDTH), rows(d), full(wa), full(wb), full(g), full(b)],
        out_specs=rows(d),
        out_shape=jax.ShapeDtypeStruct((m, d), F32),
        compiler_params=pltpu.CompilerParams(
            dimension_semantics=("parallel",), vmem_limit_bytes=VMEM_LIMIT),
        name=name,
    )(oa, ob, x, wa, wb, g, b)


def _ffn_kernel(x_ref, wu_ref, wd_ref, g_ref, b_ref, y_ref, xb_ref, acc_ref, *, alpha):
    j = pl.program_id(1)

    @pl.when(j == 0)
    def _():
        xb_ref[...] = x_ref[...].astype(BF16)
        acc_ref[...] = jnp.zeros_like(acc_ref)

    hid = jnp.maximum(_dot(xb_ref[...], wu_ref[...]), 0.0)
    acc_ref[...] += _dot((hid * hid).astype(BF16), wd_ref[...])

    @pl.when(j == pl.num_programs(1) - 1)
    def _():
        y_ref[...] = _layer_norm(alpha * x_ref[...] + acc_ref[...], g_ref[...], b_ref[...])


def _ffn_ln(x, wu, wd, g, b, alpha, name):
    m, d = x.shape
    dff = wu.shape[1]
    tm = _pick(m, 512)
    tf = _pick(dff, 1024)
    return pl.pallas_call(
        functools.partial(_ffn_kernel, alpha=alpha),
        grid=(m // tm, dff // tf),
        in_specs=[pl.BlockSpec((tm, d), lambda i, j: (i, 0)),
                  pl.BlockSpec((d, tf), lambda i, j: (0, j)),
                  pl.BlockSpec((tf, d), lambda i, j: (j, 0)),
                  pl.BlockSpec((1, d), lambda i, j: (0, 0)),
                  pl.BlockSpec((1, d), lambda i, j: (0, 0))],
        out_specs=pl.BlockSpec((tm, d), lambda i, j: (i, 0)),
        out_shape=jax.ShapeDtypeStruct((m, d), F32),
        scratch_shapes=[pltpu.VMEM((tm, d), BF16), pltpu.VMEM((tm, d), F32)],
        compiler_params=pltpu.CompilerParams(
            dimension_semantics=("parallel", "arbitrary"), vmem_limit_bytes=VMEM_LIMIT),
        name=name,
    )(x, wu, wd, g, b)


def _layer(x, conv_buf, s0, k_past, v_past, wts, alpha, tag):
    (w_a, w_ba, w_q, w_k, w_v, conv_w, alog_row, dtb_row, nw_row,
     wo_a, wo_b, ln1_g, ln1_b, w_up, w_down, ln2_g, ln2_b) = wts
    bsz, seq, d = x.shape
    m = bsz * seq
    x2 = x.reshape(m, d)
    xb = x2.astype(BF16)

    pre = _matmul(xb, w_a, "proj_gdn_" + tag).reshape(bsz, seq, 4 * WIDTH)
    ba = _matmul(xb, w_ba, "proj_gate_" + tag).reshape(bsz, seq, LANES)
    q_b = _matmul(xb, w_q, "proj_sbq_" + tag).reshape(bsz, seq, WIDTH)
    k_b = _matmul(xb, w_k, "proj_sbk_" + tag).reshape(bsz, seq, WIDTH)
    v_b = _matmul(xb, w_v, "proj_sbv_" + tag).reshape(bsz, seq, WIDTH)

    conv8 = jnp.pad(conv_buf, ((0, 0), (SUBLANES - (CONV_W - 1), 0), (0, 0)))
    o_a, s_new = _gdn(pre, ba, conv8, s0, conv_w, alog_row, dtb_row, nw_row, "gdn_" + tag)
    hist = jnp.concatenate([conv_buf, pre[:, :, :3 * WIDTH]], axis=1) if seq < CONV_W - 1 \
        else pre[:, :, :3 * WIDTH]
    conv_new = hist[:, -(CONV_W - 1):, :]

    if k_past is None:
        o_b = _sb_prompt(q_b, k_b, v_b, "sb_" + tag)
    else:
        o_b = _sb_sample(q_b, k_b, v_b, k_past, v_past, "sb_" + tag)

    x1 = _outproj_ln(o_a.reshape(m, WIDTH), o_b.reshape(m, WIDTH), x2, wo_a, wo_b,
                     ln1_g, ln1_b, alpha, "outproj_" + tag)
    y = _ffn_ln(x1, w_up, w_down, ln2_g, ln2_b, alpha, "ffn_" + tag)
    return (y.reshape(bsz, seq, d), conv_new, s_new,
            k_b.reshape(bsz, seq, N_HEADS, HEAD_DIM), v_b.reshape(bsz, seq, N_HEADS, HEAD_DIM))


def kernel(x_prompt, x_sample, state_gdn_conv, state_gdn_S, cache_sb_k, cache_sb_v, w_in, conv_w, a_log, dt_bias, gdn_norm_w, w_out, ln1_g, ln1_b, w_up, w_down, ln2_g, ln2_b):
    depth = w_in.shape[0]
    alpha = (2 * depth) ** 0.25
    off_b = 4 * WIDTH
    off_sb = off_b + 2 * N_HEADS
    yp, ys = x_prompt, x_sample
    outs = [[] for _ in range(8)]
    for l in range(depth):
        wl = w_in[l]
        w_ba = jnp.pad(wl[:, off_b:off_sb], ((0, 0), (0, LANES - 2 * N_HEADS)))
        pad_row = lambda v: jnp.pad(v.astype(F32), (N_HEADS, LANES - 2 * N_HEADS)).reshape(1, LANES)
        wts = (wl[:, :off_b].astype(BF16), w_ba.astype(BF16),
               wl[:, off_sb:off_sb + WIDTH].astype(BF16),
               wl[:, off_sb + WIDTH:off_sb + 2 * WIDTH].astype(BF16),
               wl[:, off_sb + 2 * WIDTH:].astype(BF16),
               conv_w[l], pad_row(a_log[l]), pad_row(dt_bias[l]),
               gdn_norm_w[l].astype(F32).reshape(1, HEAD_DIM),
               w_out[l, :WIDTH].astype(BF16), w_out[l, WIDTH:].astype(BF16),
               ln1_g[l].reshape(1, -1), ln1_b[l].reshape(1, -1),
               w_up[l].astype(BF16), w_down[l].astype(BF16),
               ln2_g[l].reshape(1, -1), ln2_b[l].reshape(1, -1))
        bp = yp.shape[0]
        zero_conv = jnp.zeros((bp, CONV_W - 1, 3 * WIDTH), F32)
        zero_s = jnp.zeros((bp, N_HEADS, HEAD_DIM, HEAD_DIM), F32)
        yp, c1, s1, k1, v1 = _layer(yp, zero_conv, zero_s, None, None, wts, alpha, "p%d" % l)
        ys, c2, s2, k2, v2 = _layer(ys, state_gdn_conv[l], state_gdn_S[l],
                                    cache_sb_k[l], cache_sb_v[l], wts, alpha, "s%d" % l)
        for lst, val in zip(outs, (c1, s1, k1, v1, c2, s2, k2, v2)):
            lst.append(val)
    return (yp, ys) + tuple(jnp.stack(o) for o in outs)
```

```python
import functools

import jax
import jax.numpy as jnp
from jax import lax
from jax.experimental import pallas as pl
from jax.experimental.pallas import tpu as pltpu

HEAD_DIM = 128
N_HEADS = 8
WIDTH = N_HEADS * HEAD_DIM
CONV_W = 4
CHUNK = 128
LN_EPS = 1e-5
RMS_EPS = 1e-6
L2_EPS = 1e-6
LANES = 128
SUBLANES = 8
VMEM_LIMIT = 56 * 1024 * 1024

F32 = jnp.float32
BF16 = jnp.bfloat16


def _pick(n, pref):
    if n <= pref:
        return n
    t = pref
    while n % t:
        t //= 2
    return t


def _dot(a, b):
    return jnp.dot(a, b, preferred_element_type=F32)


def _dot_nt(a, b):
    return lax.dot_general(a, b, (((1,), (1,)), ((), ())), preferred_element_type=F32)


def _sigmoid(x):
    return 1.0 / (1.0 + jnp.exp(-x))


def _softplus(x):
    return jnp.maximum(x, 0.0) + jnp.log1p(jnp.exp(-jnp.abs(x)))


PROJ_TN = 1024
N_PRE_TILES = 4 * WIDTH // PROJ_TN


def _proj_kernel(x_ref, w_ref, wg_ref, pre_ref, ba_ref, q_ref, k_ref, v_ref, kout_ref, vout_ref,
                 xb_ref, *, tm):
    j = pl.program_id(1)

    @pl.when(j == 0)
    def _():
        xb_ref[...] = x_ref[...].astype(BF16)
        ba_ref[...] = _dot(xb_ref[...], wg_ref[...])

    res = _dot(xb_ref[...], w_ref[...])

    @pl.when(j < N_PRE_TILES)
    def _():
        pre_ref[...] = res

    @pl.when(j == N_PRE_TILES)
    def _():
        q_ref[...] = res.astype(BF16)

    def keep(bf_ref, out_ref):
        bf_ref[...] = res.astype(BF16)
        for h in range(N_HEADS):
            out_ref[pl.ds(h, tm, stride=N_HEADS), :] = res[:, h * HEAD_DIM:(h + 1) * HEAD_DIM]

    @pl.when(j == N_PRE_TILES + 1)
    def _():
        keep(k_ref, kout_ref)

    @pl.when(j == N_PRE_TILES + 2)
    def _():
        keep(v_ref, vout_ref)


def _proj(x, w_main, w_gate, name):
    m, k = x.shape
    tm = _pick(m, 512)
    nt = w_main.shape[1] // PROJ_TN
    row = lambda w: pl.BlockSpec((tm, w), lambda i, j: (i, 0))
    return pl.pallas_call(
        functools.partial(_proj_kernel, tm=tm),
        grid=(m // tm, nt),
        in_specs=[row(k),
                  pl.BlockSpec((k, PROJ_TN), lambda i, j: (0, j)),
                  pl.BlockSpec((k, LANES), lambda i, j: (0, 0))],
        out_specs=[pl.BlockSpec((tm, PROJ_TN), lambda i, j: (i, jnp.minimum(j, N_PRE_TILES - 1))),
                   row(LANES), row(WIDTH), row(WIDTH), row(WIDTH),
                   pl.BlockSpec((tm * N_HEADS, HEAD_DIM), lambda i, j: (i, 0)),
                   pl.BlockSpec((tm * N_HEADS, HEAD_DIM), lambda i, j: (i, 0))],
        out_shape=[jax.ShapeDtypeStruct((m, 4 * WIDTH), F32),
                   jax.ShapeDtypeStruct((m, LANES), F32),
                   jax.ShapeDtypeStruct((m, WIDTH), BF16),
                   jax.ShapeDtypeStruct((m, WIDTH), BF16),
                   jax.ShapeDtypeStruct((m, WIDTH), BF16),
                   jax.ShapeDtypeStruct((m * N_HEADS, HEAD_DIM), F32),
                   jax.ShapeDtypeStruct((m * N_HEADS, HEAD_DIM), F32)],
        scratch_shapes=[pltpu.VMEM((tm, k), BF16)],
        compiler_params=pltpu.CompilerParams(
            dimension_semantics=("parallel", "arbitrary"), vmem_limit_bytes=VMEM_LIMIT),
        name=name,
    )(x, w_main, w_gate)


NN = (((1,), (0,)), ((), ()))
NT = (((1,), (1,)), ((), ()))
TN = (((0,), (0,)), ((), ()))

GDN_ROWS_PER_STEP = 512
GDN_PIECES = dict(qk=1, inv=1, uw=1, scan=1)


def _pieces(x, n):
    out = []
    r = x
    for i in range(n):
        p = r.astype(BF16)
        out.append(p)
        if i + 1 < n:
            r = r - p.astype(F32)
    return out


def _mm(a, b, dims=NN):
    order = max(len(a), len(b))
    acc = None
    for i, ai in enumerate(a):
        for j, bj in enumerate(b):
            if i + j < order:
                t = lax.dot_general(ai, bj, dims, preferred_element_type=F32)
                acc = t if acc is None else acc + t
    return acc


def _gdn_kernel(q_ref, k_ref, v_ref, z_ref, ba_ref, cq_ref, ck_ref, cv_ref,
                wq_ref, wk_ref, wv_ref, s0_ref, alog_ref, dtb_ref, nw_ref,
                o_ref, s_ref, tail_ref, *, rows, chunk):
    C = chunk
    heads = range(N_HEADS)
    row = lax.broadcasted_iota(jnp.int32, (C, C), 0)
    col = lax.broadcasted_iota(jnp.int32, (C, C), 1)
    incl = row >= col
    strict = row > col
    upper = row <= col
    l_incl = [incl.astype(BF16)]
    ones = [jnp.ones((C, C), BF16)]
    eye = (row == col).astype(F32)
    lane = lax.broadcasted_iota(jnp.int32, (C, LANES), 1)
    level_masks = []
    b = 1
    while b < C:
        s = b.bit_length() - 1
        same = (row >> (s + 1)) == (col >> (s + 1))
        level_masks.append(same & (((row >> s) & 1) == 1) & (((col >> s) & 1) == 0))
        b *= 2
    n_qk, n_inv, n_uw, n_scan = (GDN_PIECES[c] for c in ("qk", "inv", "uw", "scan"))

    neg_a = -jnp.exp(alog_ref[...])
    dtb = dtb_ref[...]
    nw = nw_ref[...]
    wq = wq_ref[...]
    wk = wk_ref[...]
    wv = wv_ref[...]

    @pl.when(pl.program_id(1) == 0)
    def _():
        s_ref[...] = s0_ref[...]
        tail_ref[0] = cq_ref[...]
        tail_ref[1] = ck_ref[...]
        tail_ref[2] = cv_ref[...]

    def conv(e, w):
        acc = e[5:5 + C] * w[0:1]
        for i in range(1, CONV_W):
            acc = acc + e[5 + i:5 + i + C] * w[i:i + 1]
        return acc * _sigmoid(acc)

    def head(x, i):
        return x[:, i * HEAD_DIM:(i + 1) * HEAD_DIM]

    def body(n, carry):
        tq, tk, tv = carry
        c0 = pl.multiple_of(n * C, C)
        xq = q_ref[pl.ds(c0, C), :]
        xk = k_ref[pl.ds(c0, C), :]
        xv = v_ref[pl.ds(c0, C), :]
        q_all = conv(jnp.concatenate([tq, xq], axis=0), wq)
        k_all = conv(jnp.concatenate([tk, xk], axis=0), wk)
        v_all = conv(jnp.concatenate([tv, xv], axis=0), wv)
        zc = z_ref[pl.ds(c0, C), :]
        gate_all = zc * _sigmoid(zc)
        ba = ba_ref[pl.ds(c0, C), :]
        beta_all = _sigmoid(ba)
        g_all = neg_a * _softplus(ba + dtb)

        q = [head(q_all, i) for i in heads]
        k = [head(k_all, i) for i in heads]
        q = [x * lax.rsqrt(jnp.sum(x * x, axis=-1, keepdims=True) + L2_EPS) * (HEAD_DIM ** -0.5)
             for x in q]
        k = [x * lax.rsqrt(jnp.sum(x * x, axis=-1, keepdims=True) + L2_EPS) for x in k]
        beta = [jnp.sum(jnp.where(lane == i, beta_all, 0.0), axis=-1, keepdims=True) for i in heads]
        g = [jnp.sum(jnp.where(lane == i + N_HEADS, g_all, 0.0), axis=-1, keepdims=True)
             for i in heads]
        k_beta = [k[i] * beta[i] for i in heads]
        kp = [_pieces(k[i], n_qk) for i in heads]

        gcb = [_mm(l_incl, _pieces(jnp.broadcast_to(g[i], (C, LANES)), 3)) for i in heads]
        gc_row = [_mm(ones, _pieces(jnp.where(upper, jnp.broadcast_to(g[i], (C, C)), 0.0), 3))
                  for i in heads]
        kk = [_mm(_pieces(k_beta[i], n_qk), kp[i], NT) for i in heads]
        qk = [_mm(_pieces(q[i], n_qk), kp[i], NT) for i in heads]

        decay = [jnp.where(incl, jnp.exp(gcb[i][:, :C] - gc_row[i]), 0.0) for i in heads]
        egc = [jnp.exp(gcb[i]) for i in heads]
        g_last = [gcb[i][C - 1:C, :] for i in heads]
        m = [jnp.where(strict, kk[i] * decay[i], 0.0) for i in heads]
        attn = [qk[i] * decay[i] for i in heads]

        t_inv = [eye - jnp.where(level_masks[0], m[i], 0.0) for i in heads]
        for mask in level_masks[1:]:
            tp = [_pieces(t_inv[i], n_inv) for i in heads]
            x = [_mm(tp[i], _pieces(jnp.where(mask, m[i], 0.0), n_inv)) for i in heads]
            y = [_mm(_pieces(x[i], n_inv), tp[i]) for i in heads]
            t_inv = [t_inv[i] - y[i] for i in heads]

        rhs = [jnp.concatenate([head(v_all, i) * beta[i], k_beta[i] * egc[i]], axis=1) for i in heads]
        uw = [_mm(_pieces(t_inv[i], n_uw), _pieces(rhs[i], n_uw)) for i in heads]

        s_old = [s_ref[i] for i in heads]
        lhs = [jnp.concatenate([uw[i][:, HEAD_DIM:], q[i] * egc[i]], axis=0) for i in heads]
        ws = [_mm(_pieces(lhs[i], n_scan), _pieces(s_old[i], n_scan)) for i in heads]
        v_new = [uw[i][:, :HEAD_DIM] - ws[i][:C] for i in heads]
        vp = [_pieces(v_new[i], n_scan) for i in heads]
        k_tail = [k[i] * jnp.exp(g_last[i] - gcb[i]) for i in heads]

        o = [ws[i][C:] + _mm(_pieces(attn[i], n_scan), vp[i]) for i in heads]
        s_new = [s_old[i] * jnp.exp(g_last[i]) + _mm(_pieces(k_tail[i], n_scan), vp[i], TN)
                 for i in heads]
        for i in heads:
            s_ref[i] = s_new[i]
            on = (o[i] * lax.rsqrt(jnp.mean(o[i] * o[i], axis=-1, keepdims=True) + RMS_EPS)
                  * nw * head(gate_all, i))
            o_ref[pl.ds(c0, C), i * HEAD_DIM:(i + 1) * HEAD_DIM] = on.astype(o_ref.dtype)
        return xq[C - 8:C], xk[C - 8:C], xv[C - 8:C]

    tails = lax.fori_loop(0, rows // C, body, (tail_ref[0], tail_ref[1], tail_ref[2]))
    for j in range(3):
        tail_ref[j] = tails[j]


def _gdn(pre, ba, conv8, s0, conv_w, alog_row, dtb_row, nw_row, name):
    bsz, seq, _ = pre.shape
    chunk = CHUNK if seq % CHUNK == 0 else seq
    rows = _pick(seq, GDN_ROWS_PER_STEP)

    def col(grp):
        return pl.BlockSpec((None, rows, WIDTH), lambda b, t: (b, t, grp))

    def st(grp):
        return pl.BlockSpec((None, SUBLANES, WIDTH), lambda b, t: (b, 0, grp))

    def cw(grp):
        return pl.BlockSpec((CONV_W, WIDTH), lambda b, t: (0, grp))

    row = pl.BlockSpec((1, LANES), lambda b, t: (0, 0))
    state = pl.BlockSpec((None, N_HEADS, HEAD_DIM, HEAD_DIM), lambda b, t: (b, 0, 0, 0))
    return pl.pallas_call(
        functools.partial(_gdn_kernel, rows=rows, chunk=chunk),
        grid=(bsz, seq // rows),
        in_specs=[col(0), col(1), col(2), col(3),
                  pl.BlockSpec((None, rows, LANES), lambda b, t: (b, t, 0)),
                  st(0), st(1), st(2), cw(0), cw(1), cw(2), state, row, row, row],
        out_specs=[pl.BlockSpec((None, rows, WIDTH), lambda b, t: (b, t, 0)), state],
        out_shape=[jax.ShapeDtypeStruct((bsz, seq, WIDTH), BF16),
                   jax.ShapeDtypeStruct((bsz, N_HEADS, HEAD_DIM, HEAD_DIM), F32)],
        scratch_shapes=[pltpu.VMEM((3, SUBLANES, WIDTH), F32)],
        compiler_params=pltpu.CompilerParams(
            dimension_semantics=("parallel", "arbitrary"), vmem_limit_bytes=VMEM_LIMIT),
        name=name,
    )(pre, pre, pre, pre, ba, conv8, conv8, conv8, conv_w, conv_w, conv_w, s0,
      alog_row, dtb_row, nw_row)


SB_KEYS = 256
SB_DEAD = -104.0
SB_BOUND_SLACK = 1.05


def _suffix_ones(n):
    r = lax.broadcasted_iota(jnp.int32, (n, n), 0)
    c = lax.broadcasted_iota(jnp.int32, (n, n), 1)
    return (r >= c).astype(BF16)


def _sb_split(z, mask):
    l1m = -(jnp.maximum(z, 0.0) + jnp.log(1.0 + jnp.exp(-jnp.abs(z))))
    if mask is not None:
        l1m = jnp.where(mask, l1m, 0.0)
    hi = l1m.astype(BF16)
    return hi, (l1m - hi.astype(F32)).astype(BF16)


def _sb_sums(hi, lo, suffix):
    tk = suffix.shape[0]
    tiles = [slice(s * tk, (s + 1) * tk) for s in range(hi.shape[1] // tk)]
    return [_dot(hi[:, t], suffix) + _dot(lo[:, t], suffix) for t in tiles]


def _sb_weights(z, cs, mask, run):
    n = len(cs)
    reps = cs[0].shape[1] // LANES
    tot = [None] * n
    for s in reversed(range(n)):
        tot[s] = run if reps == 1 else jnp.concatenate([run] * reps, axis=1)
        run = run + jnp.broadcast_to(cs[s][:, 0:1], run.shape)
    off = [cs[s] + tot[s] for s in range(n)]
    w = jnp.exp(z + (off[0] if n == 1 else jnp.concatenate(off, axis=1)))
    if mask is not None:
        w = jnp.where(mask, w, 0.0)
    return w.astype(BF16), run


def _sb_prompt_kernel(q_ref, k_ref, v_ref, o_ref, kb_ref, vb_ref, kmax_ref, *, tq, hb, scale):
    qi = pl.program_id(2)
    heads = range(hb)
    ones = jnp.ones((HEAD_DIM, LANES), BF16)

    def head(x, h):
        return x[:, h * HEAD_DIM:(h + 1) * HEAD_DIM]

    @pl.when(qi == 0)
    def _():
        k = k_ref[...]
        v = v_ref[...]
        zeros = jnp.zeros((tq, HEAD_DIM), BF16)
        for h in heads:
            kh = head(k, h)
            kb_ref[h, 0:tq, :] = zeros
            vb_ref[h, 0:tq, :] = zeros
            kb_ref[h, tq:, :] = kh.astype(BF16)
            vb_ref[h, tq:, :] = head(v, h).astype(BF16)
            ksq = _dot((kh * kh).astype(BF16), ones)
            kmax_ref[h] = jnp.broadcast_to(jnp.max(ksq, axis=0, keepdims=True), kmax_ref.shape[1:])

    q = q_ref[...]
    qb = [head(q, h).astype(BF16) for h in heads]
    zbound = [jnp.sqrt(_dot((head(q, h) * head(q, h)).astype(BF16), ones) * kmax_ref[h, 0:1, :])
              * (scale * SB_BOUND_SLACK) for h in heads]
    lr = _suffix_ones(SB_KEYS)

    def sweep(row0, nkeys, mask, acc, run):
        z = [_dot_nt(qb[h], kb_ref[h, pl.ds(row0, nkeys), :]) * scale for h in heads]
        pieces = [_sb_split(z[h], mask) for h in heads]
        ct = [_sb_sums(*pieces[h], lr) for h in heads]
        wr = [_sb_weights(z[h], ct[h], mask, run[h]) for h in heads]
        acc = [acc[h] + _dot(wr[h][0], vb_ref[h, pl.ds(row0, nkeys), :]) for h in heads]
        return acc, [wr[h][1] for h in heads]

    def alive(run):
        live = jnp.max(run[0] + zbound[0]) > SB_DEAD
        for h in heads[1:]:
            live = live | (jnp.max(run[h] + zbound[h]) > SB_DEAD)
        return live

    r = lax.broadcasted_iota(jnp.int32, (tq, 2 * tq), 0)
    c = lax.broadcasted_iota(jnp.int32, (tq, 2 * tq), 1)
    mask = (c < r + tq) & (c + qi * tq >= tq)
    acc, run = sweep(pl.multiple_of(qi * tq, tq), 2 * tq, mask,
                     [jnp.zeros((tq, HEAD_DIM), F32) for _ in heads],
                     [jnp.zeros((tq, LANES), F32) for _ in heads])

    def cond(carry):
        return (carry[0] >= 0) & carry[3]

    def body(carry):
        j, acc, run, _ = carry
        acc, run = sweep(pl.multiple_of((j + 1) * tq, tq), tq, None, acc, run)
        return j - 1, acc, run, alive(run)

    _, acc, _, _ = lax.while_loop(cond, body, (qi - 2, acc, run, alive(run)))
    o_ref[...] = jnp.concatenate(acc, axis=1).astype(o_ref.dtype)


SB_HEADS_PER_STEP = 2


def _sb_prompt(q, k, v, name):
    bsz, seq, _ = q.shape
    tq = _pick(seq, 256)
    hb = SB_HEADS_PER_STEP
    wid = hb * HEAD_DIM
    kv = pl.BlockSpec((None, seq, wid), lambda b, h, i: (b, 0, h))
    qo = pl.BlockSpec((None, tq, wid), lambda b, h, i: (b, i, h))
    return pl.pallas_call(
        functools.partial(_sb_prompt_kernel, tq=tq, hb=hb, scale=HEAD_DIM ** -0.5),
        grid=(bsz, N_HEADS // hb, seq // tq),
        in_specs=[qo, kv, kv],
        out_specs=qo,
        out_shape=jax.ShapeDtypeStruct((bsz, seq, WIDTH), BF16),
        scratch_shapes=[pltpu.VMEM((hb, seq + tq, HEAD_DIM), BF16),
                        pltpu.VMEM((hb, seq + tq, HEAD_DIM), BF16),
                        pltpu.VMEM((hb, SUBLANES, LANES), F32)],
        compiler_params=pltpu.CompilerParams(
            dimension_semantics=("parallel", "parallel", "arbitrary"),
            vmem_limit_bytes=VMEM_LIMIT),
        name=name,
    )(q, k, v)


def _sb_sample_kernel(q_ref, kn_ref, vn_ref, kc_ref, vc_ref, o_ref, qb_ref, acc_ref, run_ref,
                      *, seq, keys, scale):
    step = pl.program_id(1)
    heads = range(N_HEADS)
    rows = N_HEADS * seq
    def sweep(k_tiles, v_tiles, mask, lr):
        z = jnp.concatenate([_dot_nt(qb_ref[h * seq:(h + 1) * seq, :], k_tiles[h]) for h in heads],
                            axis=0) * scale
        wb, run = _sb_weights(z, _sb_sums(*_sb_split(z, mask), lr), mask, run_ref[...])
        run_ref[...] = run
        pv = [_dot(wb[h * seq:(h + 1) * seq, :], v_tiles[h]) for h in heads]
        acc_ref[...] += jnp.concatenate(pv, axis=0)

    @pl.when(step == 0)
    def _():
        for h in heads:
            qb_ref[h * seq:(h + 1) * seq, :] = q_ref[:, h * HEAD_DIM:(h + 1) * HEAD_DIM].astype(BF16)
        acc_ref[...] = jnp.zeros_like(acc_ref)
        run_ref[...] = jnp.zeros_like(run_ref)
        pad = jnp.zeros((LANES - seq, HEAD_DIM), BF16)
        kn = [jnp.concatenate([kn_ref[:, h * HEAD_DIM:(h + 1) * HEAD_DIM].astype(BF16), pad], axis=0)
              for h in heads]
        vn = [jnp.concatenate([vn_ref[:, h * HEAD_DIM:(h + 1) * HEAD_DIM].astype(BF16), pad], axis=0)
              for h in heads]
        r = lax.broadcasted_iota(jnp.int32, (rows, LANES), 0)
        c = lax.broadcasted_iota(jnp.int32, (rows, LANES), 1)
        sweep(kn, vn, c < (r & (seq - 1)), _suffix_ones(LANES))

    kt = [kc_ref[pl.ds(h, keys, stride=N_HEADS), :].astype(BF16) for h in heads]
    vt = [vc_ref[pl.ds(h, keys, stride=N_HEADS), :].astype(BF16) for h in heads]
    sweep(kt, vt, None, _suffix_ones(SB_KEYS))

    @pl.when(step == pl.num_programs(1) - 1)
    def _():
        for h in heads:
            o_ref[:, h * HEAD_DIM:(h + 1) * HEAD_DIM] = acc_ref[h * seq:(h + 1) * seq, :].astype(o_ref.dtype)


def _sb_sample(q, kn, vn, kc, vc, name):
    bsz, seq, _ = q.shape
    past = kc.shape[1]
    kc = kc.reshape(bsz, past * N_HEADS, HEAD_DIM)
    vc = vc.reshape(bsz, past * N_HEADS, HEAD_DIM)
    assert seq & (seq - 1) == 0 and seq <= LANES and past % SB_KEYS == 0
    keys = _pick(past, 512)
    nsteps = past // keys
    new = pl.BlockSpec((None, seq, WIDTH), lambda b, j: (b, 0, 0))
    old = pl.BlockSpec((None, keys * N_HEADS, HEAD_DIM), lambda b, j: (b, nsteps - 1 - j, 0))
    rows = N_HEADS * seq
    return pl.pallas_call(
        functools.partial(_sb_sample_kernel, seq=seq, keys=keys, scale=HEAD_DIM ** -0.5),
        grid=(bsz, nsteps),
        in_specs=[new, new, new, old, old],
        out_specs=new,
        out_shape=jax.ShapeDtypeStruct((bsz, seq, WIDTH), BF16),
        scratch_shapes=[pltpu.VMEM((rows, HEAD_DIM), BF16), pltpu.VMEM((rows, HEAD_DIM), F32),
                        pltpu.VMEM((rows, LANES), F32)],
        compiler_params=pltpu.CompilerParams(
            dimension_semantics=("parallel", "arbitrary"), vmem_limit_bytes=VMEM_LIMIT),
        name=name,
    )(q, kn, vn, kc, vc)


def _layer_norm(r, g, b):
    mu = jnp.mean(r, axis=-1, keepdims=True)
    d = r - mu
    var = jnp.mean(d * d, axis=-1, keepdims=True)
    return d * lax.rsqrt(var + LN_EPS) * g + b


def _outproj_kernel(oa_ref, ob_ref, x_ref, wa_ref, wb_ref, g_ref, b_ref, y_ref, *, alpha, sub):
    for s in range(y_ref.shape[0] // sub):
        rs = slice(s * sub, (s + 1) * sub)
        acc = _dot(oa_ref[rs, :], wa_ref[...]) + _dot(ob_ref[rs, :], wb_ref[...])
        y_ref[rs, :] = _layer_norm(alpha * x_ref[rs, :] + acc, g_ref[...], b_ref[...])


def _outproj_ln(oa, ob, x, wa, wb, g, b, alpha, name):
    m, d = x.shape
    tm = _pick(m, 512)
    rows = lambda w: pl.BlockSpec((tm, w), lambda i: (i, 0))
    full = lambda a: pl.BlockSpec(a.shape, lambda i: (0, 0))
    return pl.pallas_call(
        functools.partial(_outproj_kernel, alpha=alpha, sub=_pick(tm, 256)),
        grid=(m // tm,),
        in_specs=[rows(WIDTH), rows(WIDTH), rows(d), full(wa), full(wb), full(g), full(b)],
        out_specs=rows(d),
        out_shape=jax.ShapeDtypeStruct((m, d), F32),
        compiler_params=pltpu.CompilerParams(
            dimension_semantics=("parallel",), vmem_limit_bytes=VMEM_LIMIT),
        name=name,
    )(oa, ob, x, wa, wb, g, b)


def _ffn_kernel(x_ref, wu_ref, wd_ref, g_ref, b_ref, y_ref, xb_ref, acc_ref, *, alpha):
    j = pl.program_id(1)

    @pl.when(j == 0)
    def _():
        xb_ref[...] = x_ref[...].astype(BF16)
        acc_ref[...] = jnp.zeros_like(acc_ref)

    hid = jnp.maximum(_dot(xb_ref[...], wu_ref[...]), 0.0)
    acc_ref[...] += _dot((hid * hid).astype(BF16), wd_ref[...])

    @pl.when(j == pl.num_programs(1) - 1)
    def _():
        y_ref[...] = _layer_norm(alpha * x_ref[...] + acc_ref[...], g_ref[...], b_ref[...])


def _ffn_ln(x, wu, wd, g, b, alpha, name):
    m, d = x.shape
    dff = wu.shape[1]
    tm = _pick(m, 512)
    tf = _pick(dff, 1024)
    return pl.pallas_call(
        functools.partial(_ffn_kernel, alpha=alpha),
        grid=(m // tm, dff // tf),
        in_specs=[pl.BlockSpec((tm, d), lambda i, j: (i, 0)),
                  pl.BlockSpec((d, tf), lambda i, j: (0, j)),
                  pl.BlockSpec((tf, d), lambda i, j: (j, 0)),
                  pl.BlockSpec((1, d), lambda i, j: (0, 0)),
                  pl.BlockSpec((1, d), lambda i, j: (0, 0))],
        out_specs=pl.BlockSpec((tm, d), lambda i, j: (i, 0)),
        out_shape=jax.ShapeDtypeStruct((m, d), F32),
        scratch_shapes=[pltpu.VMEM((tm, d), BF16), pltpu.VMEM((tm, d), F32)],
        compiler_params=pltpu.CompilerParams(
            dimension_semantics=("parallel", "arbitrary"), vmem_limit_bytes=VMEM_LIMIT),
        name=name,
    )(x, wu, wd, g, b)


def _layer(x, conv_buf, s0, k_past, v_past, wts, alpha, tag):
    (w_main, w_gate, conv_w, alog_row, dtb_row, nw_row,
     wo_a, wo_b, ln1_g, ln1_b, w_up, w_down, ln2_g, ln2_b) = wts
    bsz, seq, d = x.shape
    m = bsz * seq
    x2 = x.reshape(m, d)

    pre, ba, q_b, k_b, v_b, k_out, v_out = _proj(x2, w_main, w_gate, "proj_" + tag)
    pre = pre.reshape(bsz, seq, 4 * WIDTH)
    ba = ba.reshape(bsz, seq, LANES)
    q_b, k_b, v_b = (t.reshape(bsz, seq, WIDTH) for t in (q_b, k_b, v_b))

    conv8 = jnp.pad(conv_buf, ((0, 0), (SUBLANES - (CONV_W - 1), 0), (0, 0)))
    o_a, s_new = _gdn(pre, ba, conv8, s0, conv_w, alog_row, dtb_row, nw_row, "gdn_" + tag)
    keep = CONV_W - 1
    conv_new = pre[:, seq - keep:, :3 * WIDTH] if seq >= keep else jnp.concatenate(
        [conv_buf, pre[:, :, :3 * WIDTH]], axis=1)[:, -keep:, :]

    if k_past is None:
        o_b = _sb_prompt(q_b, k_b, v_b, "sb_" + tag)
    else:
        o_b = _sb_sample(q_b, k_b, v_b, k_past, v_past, "sb_" + tag)

    x1 = _outproj_ln(o_a.reshape(m, WIDTH), o_b.reshape(m, WIDTH), x2, wo_a, wo_b,
                     ln1_g, ln1_b, alpha, "outproj_" + tag)
    y = _ffn_ln(x1, w_up, w_down, ln2_g, ln2_b, alpha, "ffn_" + tag)
    return (y.reshape(bsz, seq, d), conv_new, s_new,
            k_out.reshape(bsz, seq, N_HEADS, HEAD_DIM), v_out.reshape(bsz, seq, N_HEADS, HEAD_DIM))


def kernel(x_prompt, x_sample, state_gdn_conv, state_gdn_S, cache_sb_k, cache_sb_v, w_in, conv_w, a_log, dt_bias, gdn_norm_w, w_out, ln1_g, ln1_b, w_up, w_down, ln2_g, ln2_b):
    depth = w_in.shape[0]
    alpha = (2 * depth) ** 0.25
    off_b = 4 * WIDTH
    off_sb = off_b + 2 * N_HEADS
    yp, ys = x_prompt, x_sample
    outs = [[] for _ in range(8)]
    for l in range(depth):
        wl = w_in[l]
        w_gate = jnp.pad(wl[:, off_b:off_sb], ((0, 0), (0, LANES - 2 * N_HEADS))).astype(BF16)
        w_main = jnp.concatenate([wl[:, :off_b], wl[:, off_sb:]], axis=1).astype(BF16)
        pad_row = lambda v: jnp.pad(v.astype(F32), (N_HEADS, LANES - 2 * N_HEADS)).reshape(1, LANES)
        wts = (w_main, w_gate,
               conv_w[l], pad_row(a_log[l]), pad_row(dt_bias[l]),
               gdn_norm_w[l].astype(F32).reshape(1, HEAD_DIM),
               w_out[l, :WIDTH].astype(BF16), w_out[l, WIDTH:].astype(BF16),
               ln1_g[l].reshape(1, -1), ln1_b[l].reshape(1, -1),
               w_up[l].astype(BF16), w_down[l].astype(BF16),
               ln2_g[l].reshape(1, -1), ln2_b[l].reshape(1, -1))
        bp = yp.shape[0]
        zero_conv = jnp.zeros((bp, CONV_W - 1, 3 * WIDTH), F32)
        zero_s = jnp.zeros((bp, N_HEADS, HEAD_DIM, HEAD_DIM), F32)
        yp, c1, s1, k1, v1 = _layer(yp, zero_conv, zero_s, None, None, wts, alpha, "p%d" % l)
        ys, c2, s2, k2, v2 = _layer(ys, state_gdn_conv[l], state_gdn_S[l],
                                    cache_sb_k[l], cache_sb_v[l], wts, alpha, "s%d" % l)
        for lst, val in zip(outs, (c1, s1, k1, v1, c2, s2, k2, v2)):
            lst.append(val)
    return (yp, ys) + tuple(jnp.stack(o) for o in outs)
```

```python
import functools

import jax
import jax.numpy as jnp
from jax import lax
from jax.experimental import pallas as pl
from jax.experimental.pallas import tpu as pltpu

HEAD_DIM = 128
N_HEADS = 8
WIDTH = N_HEADS * HEAD_DIM
CONV_W = 4
CHUNK = 128
LN_EPS = 1e-5
RMS_EPS = 1e-6
L2_EPS = 1e-6
LANES = 128
SUBLANES = 8
VMEM_LIMIT = 56 * 1024 * 1024

F32 = jnp.float32
BF16 = jnp.bfloat16


def _pick(n, pref):
    if n <= pref:
        return n
    t = pref
    while n % t:
        t //= 2
    return t


def _dot(a, b):
    return jnp.dot(a, b, preferred_element_type=F32)


def _dot_nt(a, b):
    return lax.dot_general(a, b, (((1,), (1,)), ((), ())), preferred_element_type=F32)


def _sigmoid(x):
    return 1.0 / (1.0 + jnp.exp(-x))


def _softplus(x):
    return jnp.maximum(x, 0.0) + jnp.log1p(jnp.exp(-jnp.abs(x)))


PROJ_TN = 1024
N_PRE_TILES = 4 * WIDTH // PROJ_TN


def _proj_kernel(x_ref, wa_ref, wsb_ref, wg_ref, pre_ref, ba_ref, q_ref, k_ref, v_ref,
                 kout_ref, vout_ref, xb_ref, *, tm):
    j = pl.program_id(1)

    @pl.when(j == 0)
    def _():
        xb_ref[...] = x_ref[...].astype(BF16)
        ba_ref[...] = _dot(xb_ref[...], wg_ref[...])

    @pl.when(j < N_PRE_TILES)
    def _():
        pre_ref[...] = _dot(xb_ref[...], wa_ref[...])

    @pl.when(j == N_PRE_TILES)
    def _():
        q_ref[...] = _dot(xb_ref[...], wsb_ref[...]).astype(BF16)

    def keep(bf_ref, out_ref):
        res = _dot(xb_ref[...], wsb_ref[...])
        bf_ref[...] = res.astype(BF16)
        for h in range(N_HEADS):
            out_ref[pl.ds(h, tm, stride=N_HEADS), :] = res[:, h * HEAD_DIM:(h + 1) * HEAD_DIM]

    @pl.when(j == N_PRE_TILES + 1)
    def _():
        keep(k_ref, kout_ref)

    @pl.when(j == N_PRE_TILES + 2)
    def _():
        keep(v_ref, vout_ref)


def _proj(x, w_main, w_gate, name):
    w_in, w_sb = w_main
    m, k = x.shape
    tm = _pick(m, 512)
    nt = N_PRE_TILES + w_sb.shape[1] // PROJ_TN
    row = lambda w: pl.BlockSpec((tm, w), lambda i, j: (i, 0))
    return pl.pallas_call(
        functools.partial(_proj_kernel, tm=tm),
        grid=(m // tm, nt),
        in_specs=[row(k),
                  pl.BlockSpec((k, PROJ_TN), lambda i, j: (0, jnp.minimum(j, N_PRE_TILES - 1))),
                  pl.BlockSpec((k, PROJ_TN), lambda i, j: (0, jnp.maximum(j - N_PRE_TILES, 0))),
                  pl.BlockSpec((k, LANES), lambda i, j: (0, 0))],
        out_specs=[pl.BlockSpec((tm, PROJ_TN), lambda i, j: (i, jnp.minimum(j, N_PRE_TILES - 1))),
                   row(LANES), row(WIDTH), row(WIDTH), row(WIDTH),
                   pl.BlockSpec((tm * N_HEADS, HEAD_DIM), lambda i, j: (i, 0)),
                   pl.BlockSpec((tm * N_HEADS, HEAD_DIM), lambda i, j: (i, 0))],
        out_shape=[jax.ShapeDtypeStruct((m, 4 * WIDTH), F32),
                   jax.ShapeDtypeStruct((m, LANES), F32),
                   jax.ShapeDtypeStruct((m, WIDTH), BF16),
                   jax.ShapeDtypeStruct((m, WIDTH), BF16),
                   jax.ShapeDtypeStruct((m, WIDTH), BF16),
                   jax.ShapeDtypeStruct((m * N_HEADS, HEAD_DIM), F32),
                   jax.ShapeDtypeStruct((m * N_HEADS, HEAD_DIM), F32)],
        scratch_shapes=[pltpu.VMEM((tm, k), BF16)],
        compiler_params=pltpu.CompilerParams(
            dimension_semantics=("parallel", "arbitrary"), vmem_limit_bytes=VMEM_LIMIT),
        name=name,
    )(x, w_in, w_sb, w_gate)


NN = (((1,), (0,)), ((), ()))
NT = (((1,), (1,)), ((), ()))
TN = (((0,), (0,)), ((), ()))

GDN_ROWS_PER_STEP = 512
GDN_PIECES = dict(qk=1, inv=1, uw=1, scan=1)


def _pieces(x, n):
    out = []
    r = x
    for i in range(n):
        p = r.astype(BF16)
        out.append(p)
        if i + 1 < n:
            r = r - p.astype(F32)
    return out


def _mm(a, b, dims=NN):
    order = max(len(a), len(b))
    acc = None
    for i, ai in enumerate(a):
        for j, bj in enumerate(b):
            if i + j < order:
                t = lax.dot_general(ai, bj, dims, preferred_element_type=F32)
                acc = t if acc is None else acc + t
    return acc


def _gdn_kernel(q_ref, k_ref, v_ref, z_ref, ba_ref, cq_ref, ck_ref, cv_ref,
                wq_ref, wk_ref, wv_ref, s0_ref, alog_ref, dtb_ref, nw_ref,
                o_ref, s_ref, tail_ref, *, rows, chunk):
    C = chunk
    heads = range(N_HEADS)
    row = lax.broadcasted_iota(jnp.int32, (C, C), 0)
    col = lax.broadcasted_iota(jnp.int32, (C, C), 1)
    incl = row >= col
    strict = row > col
    upper = row <= col
    l_incl = [incl.astype(BF16)]
    ones = [jnp.ones((C, C), BF16)]
    eye = (row == col).astype(F32)
    lane = lax.broadcasted_iota(jnp.int32, (C, LANES), 1)
    level_masks = []
    b = 1
    while b < C:
        s = b.bit_length() - 1
        same = (row >> (s + 1)) == (col >> (s + 1))
        level_masks.append(same & (((row >> s) & 1) == 1) & (((col >> s) & 1) == 0))
        b *= 2
    n_qk, n_inv, n_uw, n_scan = (GDN_PIECES[c] for c in ("qk", "inv", "uw", "scan"))

    neg_a = -jnp.exp(alog_ref[...])
    dtb = dtb_ref[...]
    nw = nw_ref[...]
    wq = wq_ref[...]
    wk = wk_ref[...]
    wv = wv_ref[...]

    @pl.when(pl.program_id(1) == 0)
    def _():
        s_ref[...] = s0_ref[...]
        tail_ref[0] = cq_ref[...]
        tail_ref[1] = ck_ref[...]
        tail_ref[2] = cv_ref[...]

    def conv(e, w):
        acc = e[5:5 + C] * w[0:1]
        for i in range(1, CONV_W):
            acc = acc + e[5 + i:5 + i + C] * w[i:i + 1]
        return acc * _sigmoid(acc)

    def head(x, i):
        return x[:, i * HEAD_DIM:(i + 1) * HEAD_DIM]

    def body(n, carry):
        tq, tk, tv = carry
        c0 = pl.multiple_of(n * C, C)
        xq = q_ref[pl.ds(c0, C), :]
        xk = k_ref[pl.ds(c0, C), :]
        xv = v_ref[pl.ds(c0, C), :]
        q_all = conv(jnp.concatenate([tq, xq], axis=0), wq)
        k_all = conv(jnp.concatenate([tk, xk], axis=0), wk)
        v_all = conv(jnp.concatenate([tv, xv], axis=0), wv)
        zc = z_ref[pl.ds(c0, C), :]
        gate_all = zc * _sigmoid(zc)
        ba = ba_ref[pl.ds(c0, C), :]
        beta_all = _sigmoid(ba)
        g_all = neg_a * _softplus(ba + dtb)

        q = [head(q_all, i) for i in heads]
        k = [head(k_all, i) for i in heads]
        q = [x * lax.rsqrt(jnp.sum(x * x, axis=-1, keepdims=True) + L2_EPS) * (HEAD_DIM ** -0.5)
             for x in q]
        k = [x * lax.rsqrt(jnp.sum(x * x, axis=-1, keepdims=True) + L2_EPS) for x in k]
        beta = [jnp.sum(jnp.where(lane == i, beta_all, 0.0), axis=-1, keepdims=True) for i in heads]
        g = [jnp.sum(jnp.where(lane == i + N_HEADS, g_all, 0.0), axis=-1, keepdims=True)
             for i in heads]
        k_beta = [k[i] * beta[i] for i in heads]
        kp = [_pieces(k[i], n_qk) for i in heads]

        gcb = [_mm(l_incl, _pieces(jnp.broadcast_to(g[i], (C, LANES)), 3)) for i in heads]
        gc_row = [_mm(ones, _pieces(jnp.where(upper, jnp.broadcast_to(g[i], (C, C)), 0.0), 3))
                  for i in heads]
        kk = [_mm(_pieces(k_beta[i], n_qk), kp[i], NT) for i in heads]
        qk = [_mm(_pieces(q[i], n_qk), kp[i], NT) for i in heads]

        decay = [jnp.where(incl, jnp.exp(gcb[i][:, :C] - gc_row[i]), 0.0) for i in heads]
        egc = [jnp.exp(gcb[i]) for i in heads]
        g_last = [gcb[i][C - 1:C, :] for i in heads]
        m = [jnp.where(strict, kk[i] * decay[i], 0.0) for i in heads]
        attn = [qk[i] * decay[i] for i in heads]

        t_inv = [eye - jnp.where(level_masks[0], m[i], 0.0) for i in heads]
        for mask in level_masks[1:]:
            tp = [_pieces(t_inv[i], n_inv) for i in heads]
            x = [_mm(tp[i], _pieces(jnp.where(mask, m[i], 0.0), n_inv)) for i in heads]
            y = [_mm(_pieces(x[i], n_inv), tp[i]) for i in heads]
            t_inv = [t_inv[i] - y[i] for i in heads]

        rhs = [jnp.concatenate([head(v_all, i) * beta[i], k_beta[i] * egc[i]], axis=1) for i in heads]
        uw = [_mm(_pieces(t_inv[i], n_uw), _pieces(rhs[i], n_uw)) for i in heads]

        s_old = [s_ref[i] for i in heads]
        lhs = [jnp.concatenate([uw[i][:, HEAD_DIM:], q[i] * egc[i]], axis=0) for i in heads]
        ws = [_mm(_pieces(lhs[i], n_scan), _pieces(s_old[i], n_scan)) for i in heads]
        v_new = [uw[i][:, :HEAD_DIM] - ws[i][:C] for i in heads]
        vp = [_pieces(v_new[i], n_scan) for i in heads]
        k_tail = [k[i] * jnp.exp(g_last[i] - gcb[i]) for i in heads]

        o = [ws[i][C:] + _mm(_pieces(attn[i], n_scan), vp[i]) for i in heads]
        s_new = [s_old[i] * jnp.exp(g_last[i]) + _mm(_pieces(k_tail[i], n_scan), vp[i], TN)
                 for i in heads]
        for i in heads:
            s_ref[i] = s_new[i]
            on = (o[i] * lax.rsqrt(jnp.mean(o[i] * o[i], axis=-1, keepdims=True) + RMS_EPS)
                  * nw * head(gate_all, i))
            o_ref[pl.ds(c0, C), i * HEAD_DIM:(i + 1) * HEAD_DIM] = on.astype(o_ref.dtype)
        return xq[C - 8:C], xk[C - 8:C], xv[C - 8:C]

    tails = lax.fori_loop(0, rows // C, body, (tail_ref[0], tail_ref[1], tail_ref[2]))
    for j in range(3):
        tail_ref[j] = tails[j]


def _gdn(pre, ba, conv8, s0, conv_w, alog_row, dtb_row, nw_row, name):
    bsz, seq, _ = pre.shape
    chunk = CHUNK if seq % CHUNK == 0 else seq
    rows = _pick(seq, GDN_ROWS_PER_STEP)

    def col(grp):
        return pl.BlockSpec((None, rows, WIDTH), lambda b, t: (b, t, grp))

    def st(grp):
        return pl.BlockSpec((None, SUBLANES, WIDTH), lambda b, t: (b, 0, grp))

    def cw(grp):
        return pl.BlockSpec((CONV_W, WIDTH), lambda b, t: (0, grp))

    row = pl.BlockSpec((1, LANES), lambda b, t: (0, 0))
    state = pl.BlockSpec((None, N_HEADS, HEAD_DIM, HEAD_DIM), lambda b, t: (b, 0, 0, 0))
    return pl.pallas_call(
        functools.partial(_gdn_kernel, rows=rows, chunk=chunk),
        grid=(bsz, seq // rows),
        in_specs=[col(0), col(1), col(2), col(3),
                  pl.BlockSpec((None, rows, LANES), lambda b, t: (b, t, 0)),
                  st(0), st(1), st(2), cw(0), cw(1), cw(2), state, row, row, row],
        out_specs=[pl.BlockSpec((None, rows, WIDTH), lambda b, t: (b, t, 0)), state],
        out_shape=[jax.ShapeDtypeStruct((bsz, seq, WIDTH), BF16),
                   jax.ShapeDtypeStruct((bsz, N_HEADS, HEAD_DIM, HEAD_DIM), F32)],
        scratch_shapes=[pltpu.VMEM((3, SUBLANES, WIDTH), F32)],
        compiler_params=pltpu.CompilerParams(
            dimension_semantics=("parallel", "arbitrary"), vmem_limit_bytes=VMEM_LIMIT),
        name=name,
    )(pre, pre, pre, pre, ba, conv8, conv8, conv8, conv_w, conv_w, conv_w, s0,
      alog_row, dtb_row, nw_row)


SB_KEYS = 256
SB_DEAD = -104.0
SB_BOUND_SLACK = 1.05


def _suffix_ones(n):
    r = lax.broadcasted_iota(jnp.int32, (n, n), 0)
    c = lax.broadcasted_iota(jnp.int32, (n, n), 1)
    return (r >= c).astype(BF16)


def _sb_split(z, mask):
    l1m = -(jnp.maximum(z, 0.0) + jnp.log(1.0 + jnp.exp(-jnp.abs(z))))
    if mask is not None:
        l1m = jnp.where(mask, l1m, 0.0)
    hi = l1m.astype(BF16)
    return hi, (l1m - hi.astype(F32)).astype(BF16)


def _sb_sums(hi, lo, suffix):
    tk = suffix.shape[0]
    tiles = [slice(s * tk, (s + 1) * tk) for s in range(hi.shape[1] // tk)]
    return [_dot(hi[:, t], suffix) + _dot(lo[:, t], suffix) for t in tiles]


def _sb_weights(z, cs, mask, run):
    n = len(cs)
    reps = cs[0].shape[1] // LANES
    tot = [None] * n
    for s in reversed(range(n)):
        tot[s] = run if reps == 1 else jnp.concatenate([run] * reps, axis=1)
        run = run + jnp.broadcast_to(cs[s][:, 0:1], run.shape)
    off = [cs[s] + tot[s] for s in range(n)]
    w = jnp.exp(z + (off[0] if n == 1 else jnp.concatenate(off, axis=1)))
    if mask is not None:
        w = jnp.where(mask, w, 0.0)
    return w.astype(BF16), run


def _sb_prompt_kernel(q_ref, k_ref, v_ref, o_ref, kb_ref, vb_ref, kmax_ref, *, tq, hb, scale):
    qi = pl.program_id(2)
    heads = range(hb)
    ones = jnp.ones((HEAD_DIM, LANES), BF16)

    def head(x, h):
        return x[:, h * HEAD_DIM:(h + 1) * HEAD_DIM]

    @pl.when(qi == 0)
    def _():
        k = k_ref[...]
        v = v_ref[...]
        zeros = jnp.zeros((tq, HEAD_DIM), BF16)
        for h in heads:
            kh = head(k, h)
            kb_ref[h, 0:tq, :] = zeros
            vb_ref[h, 0:tq, :] = zeros
            kb_ref[h, tq:, :] = kh.astype(BF16)
            vb_ref[h, tq:, :] = head(v, h).astype(BF16)
            ksq = _dot((kh * kh).astype(BF16), ones)
            kmax_ref[h] = jnp.broadcast_to(jnp.max(ksq, axis=0, keepdims=True), kmax_ref.shape[1:])

    q = q_ref[...]
    qb = [head(q, h).astype(BF16) for h in heads]
    zbound = [jnp.sqrt(_dot((head(q, h) * head(q, h)).astype(BF16), ones) * kmax_ref[h, 0:1, :])
              * (scale * SB_BOUND_SLACK) for h in heads]
    lr = _suffix_ones(SB_KEYS)

    def sweep(row0, nkeys, mask, acc, run):
        z = [_dot_nt(qb[h], kb_ref[h, pl.ds(row0, nkeys), :]) * scale for h in heads]
        pieces = [_sb_split(z[h], mask) for h in heads]
        ct = [_sb_sums(*pieces[h], lr) for h in heads]
        wr = [_sb_weights(z[h], ct[h], mask, run[h]) for h in heads]
        acc = [acc[h] + _dot(wr[h][0], vb_ref[h, pl.ds(row0, nkeys), :]) for h in heads]
        return acc, [wr[h][1] for h in heads]

    def alive(run):
        live = jnp.max(run[0] + zbound[0]) > SB_DEAD
        for h in heads[1:]:
            live = live | (jnp.max(run[h] + zbound[h]) > SB_DEAD)
        return live

    r = lax.broadcasted_iota(jnp.int32, (tq, 2 * tq), 0)
    c = lax.broadcasted_iota(jnp.int32, (tq, 2 * tq), 1)
    mask = (c < r + tq) & (c + qi * tq >= tq)
    acc, run = sweep(pl.multiple_of(qi * tq, tq), 2 * tq, mask,
                     [jnp.zeros((tq, HEAD_DIM), F32) for _ in heads],
                     [jnp.zeros((tq, LANES), F32) for _ in heads])

    def cond(carry):
        return (carry[0] >= 0) & carry[3]

    def body(carry):
        j, acc, run, _ = carry
        acc, run = sweep(pl.multiple_of((j + 1) * tq, tq), tq, None, acc, run)
        return j - 1, acc, run, alive(run)

    _, acc, _, _ = lax.while_loop(cond, body, (qi - 2, acc, run, alive(run)))
    o_ref[...] = jnp.concatenate(acc, axis=1).astype(o_ref.dtype)


SB_HEADS_PER_STEP = 2
SB_SAMPLE_KEYS = 2048


def _sb_prompt(q, k, v, name):
    bsz, seq, _ = q.shape
    tq = _pick(seq, 256)
    hb = SB_HEADS_PER_STEP
    wid = hb * HEAD_DIM
    kv = pl.BlockSpec((None, seq, wid), lambda b, h, i: (b, 0, h))
    qo = pl.BlockSpec((None, tq, wid), lambda b, h, i: (b, i, h))
    return pl.pallas_call(
        functools.partial(_sb_prompt_kernel, tq=tq, hb=hb, scale=HEAD_DIM ** -0.5),
        grid=(bsz, N_HEADS // hb, seq // tq),
        in_specs=[qo, kv, kv],
        out_specs=qo,
        out_shape=jax.ShapeDtypeStruct((bsz, seq, WIDTH), BF16),
        scratch_shapes=[pltpu.VMEM((hb, seq + tq, HEAD_DIM), BF16),
                        pltpu.VMEM((hb, seq + tq, HEAD_DIM), BF16),
                        pltpu.VMEM((hb, SUBLANES, LANES), F32)],
        compiler_params=pltpu.CompilerParams(
            dimension_semantics=("parallel", "parallel", "arbitrary"),
            vmem_limit_bytes=VMEM_LIMIT),
        name=name,
    )(q, k, v)


def _sb_sample_kernel(q_ref, kn_ref, vn_ref, kc_ref, vc_ref, o_ref, qb_ref, acc_ref, run_ref,
                      *, seq, keys, scale):
    step = pl.program_id(1)
    heads = range(N_HEADS)
    rows = N_HEADS * seq
    def sweep(k_tiles, v_tiles, mask, lr):
        z = jnp.concatenate([_dot_nt(qb_ref[h * seq:(h + 1) * seq, :], k_tiles[h]) for h in heads],
                            axis=0) * scale
        wb, run = _sb_weights(z, _sb_sums(*_sb_split(z, mask), lr), mask, run_ref[...])
        run_ref[...] = run
        pv = [_dot(wb[h * seq:(h + 1) * seq, :], v_tiles[h]) for h in heads]
        acc_ref[...] += jnp.concatenate(pv, axis=0)

    @pl.when(step == 0)
    def _():
        for h in heads:
            qb_ref[h * seq:(h + 1) * seq, :] = q_ref[:, h * HEAD_DIM:(h + 1) * HEAD_DIM].astype(BF16)
        acc_ref[...] = jnp.zeros_like(acc_ref)
        run_ref[...] = jnp.zeros_like(run_ref)
        pad = jnp.zeros((LANES - seq, HEAD_DIM), BF16)
        kn = [jnp.concatenate([kn_ref[:, h * HEAD_DIM:(h + 1) * HEAD_DIM].astype(BF16), pad], axis=0)
              for h in heads]
        vn = [jnp.concatenate([vn_ref[:, h * HEAD_DIM:(h + 1) * HEAD_DIM].astype(BF16), pad], axis=0)
              for h in heads]
        r = lax.broadcasted_iota(jnp.int32, (rows, LANES), 0)
        c = lax.broadcasted_iota(jnp.int32, (rows, LANES), 1)
        sweep(kn, vn, c < (r & (seq - 1)), _suffix_ones(LANES))

    kt = [kc_ref[pl.ds(h, keys, stride=N_HEADS), :].astype(BF16) for h in heads]
    vt = [vc_ref[pl.ds(h, keys, stride=N_HEADS), :].astype(BF16) for h in heads]
    sweep(kt, vt, None, _suffix_ones(SB_KEYS))

    @pl.when(step == pl.num_programs(1) - 1)
    def _():
        for h in heads:
            o_ref[:, h * HEAD_DIM:(h + 1) * HEAD_DIM] = acc_ref[h * seq:(h + 1) * seq, :].astype(o_ref.dtype)


def _sb_sample(q, kn, vn, kc, vc, name):
    bsz, seq, _ = q.shape
    past = kc.shape[1]
    kc = kc.reshape(bsz, past * N_HEADS, HEAD_DIM)
    vc = vc.reshape(bsz, past * N_HEADS, HEAD_DIM)
    assert seq & (seq - 1) == 0 and seq <= LANES and past % SB_KEYS == 0
    keys = _pick(past, SB_SAMPLE_KEYS)
    nsteps = past // keys
    new = pl.BlockSpec((None, seq, WIDTH), lambda b, j: (b, 0, 0))
    old = pl.BlockSpec((None, keys * N_HEADS, HEAD_DIM), lambda b, j: (b, nsteps - 1 - j, 0))
    rows = N_HEADS * seq
    return pl.pallas_call(
        functools.partial(_sb_sample_kernel, seq=seq, keys=keys, scale=HEAD_DIM ** -0.5),
        grid=(bsz, nsteps),
        in_specs=[new, new, new, old, old],
        out_specs=new,
        out_shape=jax.ShapeDtypeStruct((bsz, seq, WIDTH), BF16),
        scratch_shapes=[pltpu.VMEM((rows, HEAD_DIM), BF16), pltpu.VMEM((rows, HEAD_DIM), F32),
                        pltpu.VMEM((rows, LANES), F32)],
        compiler_params=pltpu.CompilerParams(
            dimension_semantics=("parallel", "arbitrary"), vmem_limit_bytes=VMEM_LIMIT),
        name=name,
    )(q, kn, vn, kc, vc)


def _layer_norm(r, g, b):
    mu = jnp.mean(r, axis=-1, keepdims=True)
    d = r - mu
    var = jnp.mean(d * d, axis=-1, keepdims=True)
    return d * lax.rsqrt(var + LN_EPS) * g + b


def _outproj_kernel(oa_ref, ob_ref, x_ref, wa_ref, wb_ref, g_ref, b_ref, y_ref, *, alpha, sub):
    for s in range(y_ref.shape[0] // sub):
        rs = slice(s * sub, (s + 1) * sub)
        acc = _dot(oa_ref[rs, :], wa_ref[...]) + _dot(ob_ref[rs, :], wb_ref[...])
        y_ref[rs, :] = _layer_norm(alpha * x_ref[rs, :] + acc, g_ref[...], b_ref[...])


def _outproj_ln(oa, ob, x, wa, wb, g, b, alpha, name):
    m, d = x.shape
    tm = _pick(m, 512)
    rows = lambda w: pl.BlockSpec((tm, w), lambda i: (i, 0))
    full = lambda a: pl.BlockSpec(a.shape, lambda i: (0, 0))
    return pl.pallas_call(
        functools.partial(_outproj_kernel, alpha=alpha, sub=_pick(tm, 256)),
        grid=(m // tm,),
        in_specs=[rows(WIDTH), rows(WIDTH), rows(d), full(wa), full(wb), full(g), full(b)],
        out_specs=rows(d),
        out_shape=jax.ShapeDtypeStruct((m, d), F32),
        compiler_params=pltpu.CompilerParams(
            dimension_semantics=("parallel",), vmem_limit_bytes=VMEM_LIMIT),
        name=name,
    )(oa, ob, x, wa, wb, g, b)


FFN_TM = 1024
FFN_TF = 512


def _ffn_kernel(x_ref, wu_ref, wd_ref, g_ref, b_ref, y_ref, xb_ref, *, alpha):
    j = pl.program_id(1)

    @pl.when(j == 0)
    def _():
        xb_ref[...] = x_ref[...].astype(BF16)
        y_ref[...] = jnp.zeros_like(y_ref)

    hid = jnp.maximum(_dot(xb_ref[...], wu_ref[...]), 0.0)
    y_ref[...] += _dot((hid * hid).astype(BF16), wd_ref[...])

    @pl.when(j == pl.num_programs(1) - 1)
    def _():
        y_ref[...] = _layer_norm(alpha * x_ref[...] + y_ref[...], g_ref[...], b_ref[...])


def _ffn_ln(x, wu, wd, g, b, alpha, name):
    m, d = x.shape
    dff = wu.shape[1]
    tm = _pick(m, FFN_TM)
    tf = _pick(dff, FFN_TF)
    return pl.pallas_call(
        functools.partial(_ffn_kernel, alpha=alpha),
        grid=(m // tm, dff // tf),
        in_specs=[pl.BlockSpec((tm, d), lambda i, j: (i, 0)),
                  pl.BlockSpec((d, tf), lambda i, j: (0, j)),
                  pl.BlockSpec((tf, d), lambda i, j: (j, 0)),
                  pl.BlockSpec((1, d), lambda i, j: (0, 0)),
                  pl.BlockSpec((1, d), lambda i, j: (0, 0))],
        out_specs=pl.BlockSpec((tm, d), lambda i, j: (i, 0)),
        out_shape=jax.ShapeDtypeStruct((m, d), F32),
        scratch_shapes=[pltpu.VMEM((tm, d), BF16)],
        compiler_params=pltpu.CompilerParams(
            dimension_semantics=("parallel", "arbitrary"), vmem_limit_bytes=VMEM_LIMIT),
        name=name,
    )(x, wu, wd, g, b)


def _layer(x, conv_buf, s0, k_past, v_past, wts, alpha, tag):
    (w_main, w_gate, conv_w, alog_row, dtb_row, nw_row,
     wo_a, wo_b, ln1_g, ln1_b, w_up, w_down, ln2_g, ln2_b) = wts
    bsz, seq, d = x.shape
    m = bsz * seq
    x2 = x.reshape(m, d)

    pre, ba, q_b, k_b, v_b, k_out, v_out = _proj(x2, w_main, w_gate, "proj_" + tag)
    pre = pre.reshape(bsz, seq, 4 * WIDTH)
    ba = ba.reshape(bsz, seq, LANES)
    q_b, k_b, v_b = (t.reshape(bsz, seq, WIDTH) for t in (q_b, k_b, v_b))

    conv8 = jnp.pad(conv_buf, ((0, 0), (SUBLANES - (CONV_W - 1), 0), (0, 0)))
    o_a, s_new = _gdn(pre, ba, conv8, s0, conv_w, alog_row, dtb_row, nw_row, "gdn_" + tag)
    keep = CONV_W - 1
    conv_new = pre[:, seq - keep:, :3 * WIDTH] if seq >= keep else jnp.concatenate(
        [conv_buf, pre[:, :, :3 * WIDTH]], axis=1)[:, -keep:, :]

    if k_past is None:
        o_b = _sb_prompt(q_b, k_b, v_b, "sb_" + tag)
    else:
        o_b = _sb_sample(q_b, k_b, v_b, k_past, v_past, "sb_" + tag)

    x1 = _outproj_ln(o_a.reshape(m, WIDTH), o_b.reshape(m, WIDTH), x2, wo_a, wo_b,
                     ln1_g, ln1_b, alpha, "outproj_" + tag)
    y = _ffn_ln(x1, w_up, w_down, ln2_g, ln2_b, alpha, "ffn_" + tag)
    return (y.reshape(bsz, seq, d), conv_new, s_new,
            k_out.reshape(bsz, seq, N_HEADS, HEAD_DIM), v_out.reshape(bsz, seq, N_HEADS, HEAD_DIM))


def kernel(x_prompt, x_sample, state_gdn_conv, state_gdn_S, cache_sb_k, cache_sb_v, w_in, conv_w, a_log, dt_bias, gdn_norm_w, w_out, ln1_g, ln1_b, w_up, w_down, ln2_g, ln2_b):
    depth = w_in.shape[0]
    alpha = (2 * depth) ** 0.25
    off_b = 4 * WIDTH
    off_sb = off_b + 2 * N_HEADS
    yp, ys = x_prompt, x_sample
    outs = [[] for _ in range(8)]
    for l in range(depth):
        wl = w_in[l]
        wl = wl.astype(BF16)
        w_gate = jnp.pad(wl[:, off_b:off_sb], ((0, 0), (0, LANES - 2 * N_HEADS)))
        w_main = (wl, wl[:, off_sb:])
        pad_row = lambda v: jnp.pad(v.astype(F32), (N_HEADS, LANES - 2 * N_HEADS)).reshape(1, LANES)
        wts = (w_main, w_gate,
               conv_w[l], pad_row(a_log[l]), pad_row(dt_bias[l]),
               gdn_norm_w[l].astype(F32).reshape(1, HEAD_DIM),
               w_out[l, :WIDTH].astype(BF16), w_out[l, WIDTH:].astype(BF16),
               ln1_g[l].reshape(1, -1), ln1_b[l].reshape(1, -1),
               w_up[l].astype(BF16), w_down[l].astype(BF16),
               ln2_g[l].reshape(1, -1), ln2_b[l].reshape(1, -1))
        bp = yp.shape[0]
        zero_conv = jnp.zeros((bp, CONV_W - 1, 3 * WIDTH), F32)
        zero_s = jnp.zeros((bp, N_HEADS, HEAD_DIM, HEAD_DIM), F32)
        yp, c1, s1, k1, v1 = _layer(yp, zero_conv, zero_s, None, None, wts, alpha, "p%d" % l)
        ys, c2, s2, k2, v2 = _layer(ys, state_gdn_conv[l], state_gdn_S[l],
                                    cache_sb_k[l], cache_sb_v[l], wts, alpha, "s%d" % l)
        for lst, val in zip(outs, (c1, s1, k1, v1, c2, s2, k2, v2)):
            lst.append(val)
    return (yp, ys) + tuple(jnp.stack(o) for o in outs)
```

```python
import functools

import jax
import jax.numpy as jnp
from jax import lax
from jax.experimental import pallas as pl
from jax.experimental.pallas import tpu as pltpu

HEAD_DIM = 128
N_HEADS = 8
WIDTH = N_HEADS * HEAD_DIM
CONV_W = 4
CHUNK = 128
LN_EPS = 1e-5
RMS_EPS = 1e-6
L2_EPS = 1e-6
LANES = 128
SUBLANES = 8
VMEM_LIMIT = 56 * 1024 * 1024

F32 = jnp.float32
BF16 = jnp.bfloat16


def _pick(n, pref):
    if n <= pref:
        return n
    t = pref
    while n % t:
        t //= 2
    return t


def _dot(a, b):
    return jnp.dot(a, b, preferred_element_type=F32)


def _dot_nt(a, b):
    return lax.dot_general(a, b, (((1,), (1,)), ((), ())), preferred_element_type=F32)


def _sigmoid(x):
    return 1.0 / (1.0 + jnp.exp(-x))


def _softplus(x):
    return jnp.maximum(x, 0.0) + jnp.log1p(jnp.exp(-jnp.abs(x)))


PROJ_TM = 1024
PROJ_TN = 512
N_PRE_TILES = 4 * WIDTH // PROJ_TN
N_SB_TILES = WIDTH // PROJ_TN
HEADS_PER_TILE = PROJ_TN // HEAD_DIM


def _proj_kernel(x_ref, wa_ref, wsb_ref, wg_ref, pre_ref, ba_ref, q_ref, kout_ref, vout_ref,
                 xb_ref, *, tm):
    j = pl.program_id(1)

    @pl.when(j == 0)
    def _():
        xb_ref[...] = x_ref[...].astype(BF16)
        ba_ref[...] = _dot(xb_ref[...], wg_ref[...])

    @pl.when(j < N_PRE_TILES)
    def _():
        pre_ref[...] = _dot(xb_ref[...], wa_ref[...])

    @pl.when((j >= N_PRE_TILES) & (j < N_PRE_TILES + N_SB_TILES))
    def _():
        q_ref[...] = _dot(xb_ref[...], wsb_ref[...]).astype(BF16)

    def keep(out_ref, first_head):
        res = _dot(xb_ref[...], wsb_ref[...])
        for h in range(HEADS_PER_TILE):
            out_ref[pl.ds(first_head + h, tm, stride=N_HEADS), :] = res[:, h * HEAD_DIM:(h + 1) * HEAD_DIM]

    for t in range(N_SB_TILES):
        pl.when(j == N_PRE_TILES + N_SB_TILES + t)(
            functools.partial(keep, kout_ref, t * HEADS_PER_TILE))
        pl.when(j == N_PRE_TILES + 2 * N_SB_TILES + t)(
            functools.partial(keep, vout_ref, t * HEADS_PER_TILE))


def _proj(x, w_main, w_gate, name):
    w_in, w_sb = w_main
    m, k = x.shape
    tm = _pick(m, PROJ_TM)
    nt = N_PRE_TILES + 3 * N_SB_TILES
    row = lambda w: pl.BlockSpec((tm, w), lambda i, j: (i, 0))
    return pl.pallas_call(
        functools.partial(_proj_kernel, tm=tm),
        grid=(m // tm, nt),
        in_specs=[row(k),
                  pl.BlockSpec((k, PROJ_TN), lambda i, j: (0, jnp.minimum(j, N_PRE_TILES - 1))),
                  pl.BlockSpec((k, PROJ_TN), lambda i, j: (0, jnp.maximum(j - N_PRE_TILES, 0))),
                  pl.BlockSpec((k, LANES), lambda i, j: (0, 0))],
        out_specs=[pl.BlockSpec((tm, PROJ_TN), lambda i, j: (i, jnp.minimum(j, N_PRE_TILES - 1))),
                   row(LANES),
                   pl.BlockSpec((tm, PROJ_TN),
                                lambda i, j: (i, jnp.clip(j - N_PRE_TILES, 0, N_SB_TILES - 1))),
                   pl.BlockSpec((tm * N_HEADS, HEAD_DIM), lambda i, j: (i, 0)),
                   pl.BlockSpec((tm * N_HEADS, HEAD_DIM), lambda i, j: (i, 0))],
        out_shape=[jax.ShapeDtypeStruct((m, 4 * WIDTH), F32),
                   jax.ShapeDtypeStruct((m, LANES), F32),
                   jax.ShapeDtypeStruct((m, WIDTH), BF16),
                   jax.ShapeDtypeStruct((m * N_HEADS, HEAD_DIM), F32),
                   jax.ShapeDtypeStruct((m * N_HEADS, HEAD_DIM), F32)],
        scratch_shapes=[pltpu.VMEM((tm, k), BF16)],
        compiler_params=pltpu.CompilerParams(
            dimension_semantics=("parallel", "arbitrary"), vmem_limit_bytes=VMEM_LIMIT),
        name=name,
    )(x, w_in, w_sb, w_gate)


NN = (((1,), (0,)), ((), ()))
NT = (((1,), (1,)), ((), ()))
TN = (((0,), (0,)), ((), ()))

GDN_ROWS_PER_STEP = 512
GDN_PIECES = dict(qk=1, inv=1, uw=1, scan=1)


def _pieces(x, n):
    out = []
    r = x
    for i in range(n):
        p = r.astype(BF16)
        out.append(p)
        if i + 1 < n:
            r = r - p.astype(F32)
    return out


def _mm(a, b, dims=NN):
    order = max(len(a), len(b))
    acc = None
    for i, ai in enumerate(a):
        for j, bj in enumerate(b):
            if i + j < order:
                t = lax.dot_general(ai, bj, dims, preferred_element_type=F32)
                acc = t if acc is None else acc + t
    return acc


def _gdn_kernel(q_ref, k_ref, v_ref, z_ref, ba_ref, cq_ref, ck_ref, cv_ref,
                wq_ref, wk_ref, wv_ref, s0_ref, alog_ref, dtb_ref, nw_ref,
                o_ref, s_ref, tail_ref, *, rows, chunk):
    C = chunk
    heads = range(N_HEADS)
    row = lax.broadcasted_iota(jnp.int32, (C, C), 0)
    col = lax.broadcasted_iota(jnp.int32, (C, C), 1)
    incl = row >= col
    strict = row > col
    upper = row <= col
    l_incl = [incl.astype(BF16)]
    ones = [jnp.ones((C, C), BF16)]
    eye = (row == col).astype(F32)
    lane = lax.broadcasted_iota(jnp.int32, (C, LANES), 1)
    level_masks = []
    b = 1
    while b < C:
        s = b.bit_length() - 1
        same = (row >> (s + 1)) == (col >> (s + 1))
        level_masks.append(same & (((row >> s) & 1) == 1) & (((col >> s) & 1) == 0))
        b *= 2
    n_qk, n_inv, n_uw, n_scan = (GDN_PIECES[c] for c in ("qk", "inv", "uw", "scan"))

    neg_a = -jnp.exp(alog_ref[...])
    dtb = dtb_ref[...]
    nw = nw_ref[...]
    wq = wq_ref[...]
    wk = wk_ref[...]
    wv = wv_ref[...]

    @pl.when(pl.program_id(1) == 0)
    def _():
        s_ref[...] = s0_ref[...]
        tail_ref[0] = cq_ref[...]
        tail_ref[1] = ck_ref[...]
        tail_ref[2] = cv_ref[...]

    def conv(e, w):
        acc = e[5:5 + C] * w[0:1]
        for i in range(1, CONV_W):
            acc = acc + e[5 + i:5 + i + C] * w[i:i + 1]
        return acc * _sigmoid(acc)

    def head(x, i):
        return x[:, i * HEAD_DIM:(i + 1) * HEAD_DIM]

    def body(n, carry):
        tq, tk, tv = carry
        c0 = pl.multiple_of(n * C, C)
        xq = q_ref[pl.ds(c0, C), :]
        xk = k_ref[pl.ds(c0, C), :]
        xv = v_ref[pl.ds(c0, C), :]
        q_all = conv(jnp.concatenate([tq, xq], axis=0), wq)
        k_all = conv(jnp.concatenate([tk, xk], axis=0), wk)
        v_all = conv(jnp.concatenate([tv, xv], axis=0), wv)
        zc = z_ref[pl.ds(c0, C), :]
        gate_all = zc * _sigmoid(zc)
        ba = ba_ref[pl.ds(c0, C), :]
        beta_all = _sigmoid(ba)
        g_all = neg_a * _softplus(ba + dtb)

        q = [head(q_all, i) for i in heads]
        k = [head(k_all, i) for i in heads]
        q = [x * lax.rsqrt(jnp.sum(x * x, axis=-1, keepdims=True) + L2_EPS) * (HEAD_DIM ** -0.5)
             for x in q]
        k = [x * lax.rsqrt(jnp.sum(x * x, axis=-1, keepdims=True) + L2_EPS) for x in k]
        beta = [jnp.sum(jnp.where(lane == i, beta_all, 0.0), axis=-1, keepdims=True) for i in heads]
        g = [jnp.sum(jnp.where(lane == i + N_HEADS, g_all, 0.0), axis=-1, keepdims=True)
             for i in heads]
        k_beta = [k[i] * beta[i] for i in heads]
        kp = [_pieces(k[i], n_qk) for i in heads]

        gcb = [_mm(l_incl, _pieces(jnp.broadcast_to(g[i], (C, LANES)), 3)) for i in heads]
        gc_row = [_mm(ones, _pieces(jnp.where(upper, jnp.broadcast_to(g[i], (C, C)), 0.0), 3))
                  for i in heads]
        kk = [_mm(_pieces(k_beta[i], n_qk), kp[i], NT) for i in heads]
        qk = [_mm(_pieces(q[i], n_qk), kp[i], NT) for i in heads]

        decay = [jnp.where(incl, jnp.exp(gcb[i][:, :C] - gc_row[i]), 0.0) for i in heads]
        egc = [jnp.exp(gcb[i]) for i in heads]
        g_last = [gcb[i][C - 1:C, :] for i in heads]
        m = [jnp.where(strict, kk[i] * decay[i], 0.0) for i in heads]
        attn = [qk[i] * decay[i] for i in heads]

        t_inv = [eye - jnp.where(level_masks[0], m[i], 0.0) for i in heads]
        for mask in level_masks[1:]:
            tp = [_pieces(t_inv[i], n_inv) for i in heads]
            x = [_mm(tp[i], _pieces(jnp.where(mask, m[i], 0.0), n_inv)) for i in heads]
            y = [_mm(_pieces(x[i], n_inv), tp[i]) for i in heads]
            t_inv = [t_inv[i] - y[i] for i in heads]

        rhs = [jnp.concatenate([head(v_all, i) * beta[i], k_beta[i] * egc[i]], axis=1) for i in heads]
        uw = [_mm(_pieces(t_inv[i], n_uw), _pieces(rhs[i], n_uw)) for i in heads]

        s_old = [s_ref[i] for i in heads]
        lhs = [jnp.concatenate([uw[i][:, HEAD_DIM:], q[i] * egc[i]], axis=0) for i in heads]
        ws = [_mm(_pieces(lhs[i], n_scan), _pieces(s_old[i], n_scan)) for i in heads]
        v_new = [uw[i][:, :HEAD_DIM] - ws[i][:C] for i in heads]
        vp = [_pieces(v_new[i], n_scan) for i in heads]
        k_tail = [k[i] * jnp.exp(g_last[i] - gcb[i]) for i in heads]

        o = [ws[i][C:] + _mm(_pieces(attn[i], n_scan), vp[i]) for i in heads]
        s_new = [s_old[i] * jnp.exp(g_last[i]) + _mm(_pieces(k_tail[i], n_scan), vp[i], TN)
                 for i in heads]
        for i in heads:
            s_ref[i] = s_new[i]
            on = (o[i] * lax.rsqrt(jnp.mean(o[i] * o[i], axis=-1, keepdims=True) + RMS_EPS)
                  * nw * head(gate_all, i))
            o_ref[pl.ds(c0, C), i * HEAD_DIM:(i + 1) * HEAD_DIM] = on.astype(o_ref.dtype)
        return xq[C - 8:C], xk[C - 8:C], xv[C - 8:C]

    tails = lax.fori_loop(0, rows // C, body, (tail_ref[0], tail_ref[1], tail_ref[2]))
    for j in range(3):
        tail_ref[j] = tails[j]


def _gdn(pre, ba, conv8, s0, conv_w, alog_row, dtb_row, nw_row, name):
    bsz, seq, _ = pre.shape
    chunk = CHUNK if seq % CHUNK == 0 else seq
    rows = _pick(seq, GDN_ROWS_PER_STEP)

    def col(grp):
        return pl.BlockSpec((None, rows, WIDTH), lambda b, t: (b, t, grp))

    def st(grp):
        return pl.BlockSpec((None, SUBLANES, WIDTH), lambda b, t: (b, 0, grp))

    def cw(grp):
        return pl.BlockSpec((CONV_W, WIDTH), lambda b, t: (0, grp))

    row = pl.BlockSpec((1, LANES), lambda b, t: (0, 0))
    state = pl.BlockSpec((None, N_HEADS, HEAD_DIM, HEAD_DIM), lambda b, t: (b, 0, 0, 0))
    return pl.pallas_call(
        functools.partial(_gdn_kernel, rows=rows, chunk=chunk),
        grid=(bsz, seq // rows),
        in_specs=[col(0), col(1), col(2), col(3),
                  pl.BlockSpec((None, rows, LANES), lambda b, t: (b, t, 0)),
                  st(0), st(1), st(2), cw(0), cw(1), cw(2), state, row, row, row],
        out_specs=[pl.BlockSpec((None, rows, WIDTH), lambda b, t: (b, t, 0)), state],
        out_shape=[jax.ShapeDtypeStruct((bsz, seq, WIDTH), BF16),
                   jax.ShapeDtypeStruct((bsz, N_HEADS, HEAD_DIM, HEAD_DIM), F32)],
        scratch_shapes=[pltpu.VMEM((3, SUBLANES, WIDTH), F32)],
        compiler_params=pltpu.CompilerParams(
            dimension_semantics=("parallel", "arbitrary"), vmem_limit_bytes=VMEM_LIMIT),
        name=name,
    )(pre, pre, pre, pre, ba, conv8, conv8, conv8, conv_w, conv_w, conv_w, s0,
      alog_row, dtb_row, nw_row)


SB_KEYS = 256
SB_DEAD = -104.0
SB_BOUND_SLACK = 1.05


def _suffix_ones(n):
    r = lax.broadcasted_iota(jnp.int32, (n, n), 0)
    c = lax.broadcasted_iota(jnp.int32, (n, n), 1)
    return (r >= c).astype(BF16)


def _sb_split(z, mask):
    l1m = -(jnp.maximum(z, 0.0) + jnp.log(1.0 + jnp.exp(-jnp.abs(z))))
    if mask is not None:
        l1m = jnp.where(mask, l1m, 0.0)
    hi = l1m.astype(BF16)
    return hi, (l1m - hi.astype(F32)).astype(BF16)


def _sb_sums(hi, lo, suffix):
    tk = suffix.shape[0]
    tiles = [slice(s * tk, (s + 1) * tk) for s in range(hi.shape[1] // tk)]
    return [_dot(hi[:, t], suffix) + _dot(lo[:, t], suffix) for t in tiles]


def _sb_weights(z, cs, mask, run):
    n = len(cs)
    reps = cs[0].shape[1] // LANES
    tot = [None] * n
    for s in reversed(range(n)):
        tot[s] = run if reps == 1 else jnp.concatenate([run] * reps, axis=1)
        run = run + jnp.broadcast_to(cs[s][:, 0:1], run.shape)
    off = [cs[s] + tot[s] for s in range(n)]
    w = jnp.exp(z + (off[0] if n == 1 else jnp.concatenate(off, axis=1)))
    if mask is not None:
        w = jnp.where(mask, w, 0.0)
    return w.astype(BF16), run


def _sb_prompt_kernel(q_ref, k_ref, v_ref, o_ref, kb_ref, vb_ref, kmax_ref, *, tq, hb, scale):
    qi = pl.program_id(2)
    heads = range(hb)
    ones = jnp.ones((HEAD_DIM, LANES), BF16)

    def head(x, h):
        return x[:, h * HEAD_DIM:(h + 1) * HEAD_DIM]

    @pl.when(qi == 0)
    def _():
        seq = k_ref.shape[0] // N_HEADS
        zeros = jnp.zeros((tq, HEAD_DIM), BF16)
        for h in heads:
            first = pl.program_id(1) * hb + h
            kh = k_ref[pl.ds(first, seq, stride=N_HEADS), :]
            kb_ref[h, 0:tq, :] = zeros
            vb_ref[h, 0:tq, :] = zeros
            kb_ref[h, tq:, :] = kh.astype(BF16)
            vb_ref[h, tq:, :] = v_ref[pl.ds(first, seq, stride=N_HEADS), :].astype(BF16)
            ksq = _dot((kh * kh).astype(BF16), ones)
            kmax_ref[h] = jnp.broadcast_to(jnp.max(ksq, axis=0, keepdims=True), kmax_ref.shape[1:])

    q = q_ref[...]
    qb = [head(q, h).astype(BF16) for h in heads]
    zbound = [jnp.sqrt(_dot((head(q, h) * head(q, h)).astype(BF16), ones) * kmax_ref[h, 0:1, :])
              * (scale * SB_BOUND_SLACK) for h in heads]
    lr = _suffix_ones(SB_KEYS)

    def sweep(row0, nkeys, mask, acc, run):
        z = [_dot_nt(qb[h], kb_ref[h, pl.ds(row0, nkeys), :]) * scale for h in heads]
        pieces = [_sb_split(z[h], mask) for h in heads]
        ct = [_sb_sums(*pieces[h], lr) for h in heads]
        wr = [_sb_weights(z[h], ct[h], mask, run[h]) for h in heads]
        acc = [acc[h] + _dot(wr[h][0], vb_ref[h, pl.ds(row0, nkeys), :]) for h in heads]
        return acc, [wr[h][1] for h in heads]

    def alive(run):
        live = jnp.max(run[0] + zbound[0]) > SB_DEAD
        for h in heads[1:]:
            live = live | (jnp.max(run[h] + zbound[h]) > SB_DEAD)
        return live

    r = lax.broadcasted_iota(jnp.int32, (tq, 2 * tq), 0)
    c = lax.broadcasted_iota(jnp.int32, (tq, 2 * tq), 1)
    mask = (c < r + tq) & (c + qi * tq >= tq)
    acc, run = sweep(pl.multiple_of(qi * tq, tq), 2 * tq, mask,
                     [jnp.zeros((tq, HEAD_DIM), F32) for _ in heads],
                     [jnp.zeros((tq, LANES), F32) for _ in heads])

    def cond(carry):
        return (carry[0] >= 0) & carry[3]

    def body(carry):
        j, acc, run, _ = carry
        acc, run = sweep(pl.multiple_of((j + 1) * tq, tq), tq, None, acc, run)
        return j - 1, acc, run, alive(run)

    _, acc, _, _ = lax.while_loop(cond, body, (qi - 2, acc, run, alive(run)))
    o_ref[...] = jnp.concatenate(acc, axis=1).astype(o_ref.dtype)


SB_HEADS_PER_STEP = 2
SB_SAMPLE_KEYS = 2048


def _sb_prompt(q, k, v, name):
    bsz, seq, _ = q.shape
    tq = _pick(seq, 256)
    hb = SB_HEADS_PER_STEP
    wid = hb * HEAD_DIM
    kv = pl.BlockSpec((None, seq * N_HEADS, HEAD_DIM), lambda b, h, i: (b, 0, 0))
    qo = pl.BlockSpec((None, tq, wid), lambda b, h, i: (b, i, h))
    return pl.pallas_call(
        functools.partial(_sb_prompt_kernel, tq=tq, hb=hb, scale=HEAD_DIM ** -0.5),
        grid=(bsz, N_HEADS // hb, seq // tq),
        in_specs=[qo, kv, kv],
        out_specs=qo,
        out_shape=jax.ShapeDtypeStruct((bsz, seq, WIDTH), BF16),
        scratch_shapes=[pltpu.VMEM((hb, seq + tq, HEAD_DIM), BF16),
                        pltpu.VMEM((hb, seq + tq, HEAD_DIM), BF16),
                        pltpu.VMEM((hb, SUBLANES, LANES), F32)],
        compiler_params=pltpu.CompilerParams(
            dimension_semantics=("parallel", "parallel", "arbitrary"),
            vmem_limit_bytes=VMEM_LIMIT),
        name=name,
    )(q, k, v)


def _sb_sample_kernel(q_ref, kn_ref, vn_ref, kc_ref, vc_ref, o_ref, qb_ref, acc_ref, run_ref,
                      *, seq, keys, scale):
    step = pl.program_id(1)
    heads = range(N_HEADS)
    rows = N_HEADS * seq
    def sweep(k_tiles, v_tiles, mask, lr):
        z = jnp.concatenate([_dot_nt(qb_ref[h * seq:(h + 1) * seq, :], k_tiles[h]) for h in heads],
                            axis=0) * scale
        wb, run = _sb_weights(z, _sb_sums(*_sb_split(z, mask), lr), mask, run_ref[...])
        run_ref[...] = run
        pv = [_dot(wb[h * seq:(h + 1) * seq, :], v_tiles[h]) for h in heads]
        acc_ref[...] += jnp.concatenate(pv, axis=0)

    @pl.when(step == 0)
    def _():
        for h in heads:
            qb_ref[h * seq:(h + 1) * seq, :] = q_ref[:, h * HEAD_DIM:(h + 1) * HEAD_DIM].astype(BF16)
        acc_ref[...] = jnp.zeros_like(acc_ref)
        run_ref[...] = jnp.zeros_like(run_ref)
        pad = jnp.zeros((LANES - seq, HEAD_DIM), BF16)
        kn = [jnp.concatenate([kn_ref[pl.ds(h, seq, stride=N_HEADS), :].astype(BF16), pad], axis=0)
              for h in heads]
        vn = [jnp.concatenate([vn_ref[pl.ds(h, seq, stride=N_HEADS), :].astype(BF16), pad], axis=0)
              for h in heads]
        r = lax.broadcasted_iota(jnp.int32, (rows, LANES), 0)
        c = lax.broadcasted_iota(jnp.int32, (rows, LANES), 1)
        sweep(kn, vn, c < (r & (seq - 1)), _suffix_ones(LANES))

    kt = [kc_ref[pl.ds(h, keys, stride=N_HEADS), :].astype(BF16) for h in heads]
    vt = [vc_ref[pl.ds(h, keys, stride=N_HEADS), :].astype(BF16) for h in heads]
    sweep(kt, vt, None, _suffix_ones(SB_KEYS))

    @pl.when(step == pl.num_programs(1) - 1)
    def _():
        for h in heads:
            o_ref[:, h * HEAD_DIM:(h + 1) * HEAD_DIM] = acc_ref[h * seq:(h + 1) * seq, :].astype(o_ref.dtype)


def _sb_sample(q, kn, vn, kc, vc, name):
    bsz, seq, _ = q.shape
    past = kc.shape[1]
    kc = kc.reshape(bsz, past * N_HEADS, HEAD_DIM)
    vc = vc.reshape(bsz, past * N_HEADS, HEAD_DIM)
    assert seq & (seq - 1) == 0 and seq <= LANES and past % SB_KEYS == 0
    keys = _pick(past, SB_SAMPLE_KEYS)
    nsteps = past // keys
    new = pl.BlockSpec((None, seq, WIDTH), lambda b, j: (b, 0, 0))
    newkv = pl.BlockSpec((None, seq * N_HEADS, HEAD_DIM), lambda b, j: (b, 0, 0))
    old = pl.BlockSpec((None, keys * N_HEADS, HEAD_DIM), lambda b, j: (b, nsteps - 1 - j, 0))
    rows = N_HEADS * seq
    return pl.pallas_call(
        functools.partial(_sb_sample_kernel, seq=seq, keys=keys, scale=HEAD_DIM ** -0.5),
        grid=(bsz, nsteps),
        in_specs=[new, newkv, newkv, old, old],
        out_specs=new,
        out_shape=jax.ShapeDtypeStruct((bsz, seq, WIDTH), BF16),
        scratch_shapes=[pltpu.VMEM((rows, HEAD_DIM), BF16), pltpu.VMEM((rows, HEAD_DIM), F32),
                        pltpu.VMEM((rows, LANES), F32)],
        compiler_params=pltpu.CompilerParams(
            dimension_semantics=("parallel", "arbitrary"), vmem_limit_bytes=VMEM_LIMIT),
        name=name,
    )(q, kn, vn, kc, vc)


def _layer_norm(r, g, b):
    mu = jnp.mean(r, axis=-1, keepdims=True)
    d = r - mu
    var = jnp.mean(d * d, axis=-1, keepdims=True)
    return d * lax.rsqrt(var + LN_EPS) * g + b


def _outproj_kernel(oa_ref, ob_ref, x_ref, wa_ref, wb_ref, g_ref, b_ref, y_ref, *, alpha, sub):
    for s in range(y_ref.shape[0] // sub):
        rs = slice(s * sub, (s + 1) * sub)
        acc = _dot(oa_ref[rs, :], wa_ref[...]) + _dot(ob_ref[rs, :], wb_ref[...])
        y_ref[rs, :] = _layer_norm(alpha * x_ref[rs, :] + acc, g_ref[...], b_ref[...])


def _outproj_ln(oa, ob, x, wa, wb, g, b, alpha, name):
    m, d = x.shape
    tm = _pick(m, 512)
    rows = lambda w: pl.BlockSpec((tm, w), lambda i: (i, 0))
    full = lambda a: pl.BlockSpec(a.shape, lambda i: (0, 0))
    return pl.pallas_call(
        functools.partial(_outproj_kernel, alpha=alpha, sub=_pick(tm, 256)),
        grid=(m // tm,),
        in_specs=[rows(WIDTH), rows(WIDTH), rows(d), full(wa), full(wb), full(g), full(b)],
        out_specs=rows(d),
        out_shape=jax.ShapeDtypeStruct((m, d), F32),
        compiler_params=pltpu.CompilerParams(
            dimension_semantics=("parallel",), vmem_limit_bytes=VMEM_LIMIT),
        name=name,
    )(oa, ob, x, wa, wb, g, b)


FFN_TM = 512
FFN_TF = 1024


def _ffn_kernel(x_ref, wu_ref, wd_ref, g_ref, b_ref, y_ref, xb_ref, *, alpha):
    j = pl.program_id(1)

    @pl.when(j == 0)
    def _():
        xb_ref[...] = x_ref[...].astype(BF16)
        y_ref[...] = jnp.zeros_like(y_ref)

    hid = jnp.maximum(_dot(xb_ref[...], wu_ref[...]), 0.0)
    y_ref[...] += _dot((hid * hid).astype(BF16), wd_ref[...])

    @pl.when(j == pl.num_programs(1) - 1)
    def _():
        y_ref[...] = _layer_norm(alpha * x_ref[...] + y_ref[...], g_ref[...], b_ref[...])


def _ffn_ln(x, wu, wd, g, b, alpha, name):
    m, d = x.shape
    dff = wu.shape[1]
    tm = _pick(m, FFN_TM)
    tf = _pick(dff, FFN_TF)
    return pl.pallas_call(
        functools.partial(_ffn_kernel, alpha=alpha),
        grid=(m // tm, dff // tf),
        in_specs=[pl.BlockSpec((tm, d), lambda i, j: (i, 0)),
                  pl.BlockSpec((d, tf), lambda i, j: (0, j)),
                  pl.BlockSpec((tf, d), lambda i, j: (j, 0)),
                  pl.BlockSpec((1, d), lambda i, j: (0, 0)),
                  pl.BlockSpec((1, d), lambda i, j: (0, 0))],
        out_specs=pl.BlockSpec((tm, d), lambda i, j: (i, 0)),
        out_shape=jax.ShapeDtypeStruct((m, d), F32),
        scratch_shapes=[pltpu.VMEM((tm, d), BF16)],
        compiler_params=pltpu.CompilerParams(
            dimension_semantics=("parallel", "arbitrary"), vmem_limit_bytes=VMEM_LIMIT),
        name=name,
    )(x, wu, wd, g, b)


def _layer(x, conv_buf, s0, k_past, v_past, wts, alpha, tag):
    (w_main, w_gate, conv_w, alog_row, dtb_row, nw_row,
     wo_a, wo_b, ln1_g, ln1_b, w_up, w_down, ln2_g, ln2_b) = wts
    bsz, seq, d = x.shape
    m = bsz * seq
    x2 = x.reshape(m, d)

    pre, ba, q_b, k_out, v_out = _proj(x2, w_main, w_gate, "proj_" + tag)
    pre = pre.reshape(bsz, seq, 4 * WIDTH)
    ba = ba.reshape(bsz, seq, LANES)
    q_b = q_b.reshape(bsz, seq, WIDTH)
    k_b = k_out.reshape(bsz, seq * N_HEADS, HEAD_DIM)
    v_b = v_out.reshape(bsz, seq * N_HEADS, HEAD_DIM)

    conv8 = jnp.pad(conv_buf, ((0, 0), (SUBLANES - (CONV_W - 1), 0), (0, 0)))
    o_a, s_new = _gdn(pre, ba, conv8, s0, conv_w, alog_row, dtb_row, nw_row, "gdn_" + tag)
    keep = CONV_W - 1
    conv_new = pre[:, seq - keep:, :3 * WIDTH] if seq >= keep else jnp.concatenate(
        [conv_buf, pre[:, :, :3 * WIDTH]], axis=1)[:, -keep:, :]

    if k_past is None:
        o_b = _sb_prompt(q_b, k_b, v_b, "sb_" + tag)
    else:
        o_b = _sb_sample(q_b, k_b, v_b, k_past, v_past, "sb_" + tag)

    x1 = _outproj_ln(o_a.reshape(m, WIDTH), o_b.reshape(m, WIDTH), x2, wo_a, wo_b,
                     ln1_g, ln1_b, alpha, "outproj_" + tag)
    y = _ffn_ln(x1, w_up, w_down, ln2_g, ln2_b, alpha, "ffn_" + tag)
    return (y.reshape(bsz, seq, d), conv_new, s_new,
            k_out.reshape(bsz, seq, N_HEADS, HEAD_DIM), v_out.reshape(bsz, seq, N_HEADS, HEAD_DIM))


def kernel(x_prompt, x_sample, state_gdn_conv, state_gdn_S, cache_sb_k, cache_sb_v, w_in, conv_w, a_log, dt_bias, gdn_norm_w, w_out, ln1_g, ln1_b, w_up, w_down, ln2_g, ln2_b):
    depth = w_in.shape[0]
    alpha = (2 * depth) ** 0.25
    off_b = 4 * WIDTH
    off_sb = off_b + 2 * N_HEADS
    yp, ys = x_prompt, x_sample
    outs = [[] for _ in range(8)]
    for l in range(depth):
        wl = w_in[l]
        wl = wl.astype(BF16)
        w_gate = jnp.pad(wl[:, off_b:off_sb], ((0, 0), (0, LANES - 2 * N_HEADS)))
        w_main = (wl, wl[:, off_sb:])
        pad_row = lambda v: jnp.pad(v.astype(F32), (N_HEADS, LANES - 2 * N_HEADS)).reshape(1, LANES)
        wts = (w_main, w_gate,
               conv_w[l], pad_row(a_log[l]), pad_row(dt_bias[l]),
               gdn_norm_w[l].astype(F32).reshape(1, HEAD_DIM),
               w_out[l, :WIDTH].astype(BF16), w_out[l, WIDTH:].astype(BF16),
               ln1_g[l].reshape(1, -1), ln1_b[l].reshape(1, -1),
               w_up[l].astype(BF16), w_down[l].astype(BF16),
               ln2_g[l].reshape(1, -1), ln2_b[l].reshape(1, -1))
        bp = yp.shape[0]
        zero_conv = jnp.zeros((bp, CONV_W - 1, 3 * WIDTH), F32)
        zero_s = jnp.zeros((bp, N_HEADS, HEAD_DIM, HEAD_DIM), F32)
        yp, c1, s1, k1, v1 = _layer(yp, zero_conv, zero_s, None, None, wts, alpha, "p%d" % l)
        ys, c2, s2, k2, v2 = _layer(ys, state_gdn_conv[l], state_gdn_S[l],
                                    cache_sb_k[l], cache_sb_v[l], wts, alpha, "s%d" % l)
        for lst, val in zip(outs, (c1, s1, k1, v1, c2, s2, k2, v2)):
            lst.append(val)
    return (yp, ys) + tuple(jnp.stack(o) for o in outs)
```

```python
import functools

import jax
import jax.numpy as jnp
from jax import lax
from jax.experimental import pallas as pl
from jax.experimental.pallas import tpu as pltpu

HEAD_DIM = 128
N_HEADS = 8
WIDTH = N_HEADS * HEAD_DIM
CONV_W = 4
CHUNK = 128
LN_EPS = 1e-5
RMS_EPS = 1e-6
L2_EPS = 1e-6
LANES = 128
SUBLANES = 8
VMEM_LIMIT = 56 * 1024 * 1024

F32 = jnp.float32
BF16 = jnp.bfloat16


def _pick(n, pref):
    if n <= pref:
        return n
    t = pref
    while n % t:
        t //= 2
    return t


def _dot(a, b):
    return jnp.dot(a, b, preferred_element_type=F32)


def _dot_nt(a, b):
    return lax.dot_general(a, b, (((1,), (1,)), ((), ())), preferred_element_type=F32)


def _sigmoid(x):
    return 1.0 / (1.0 + jnp.exp(-x))


def _softplus(x):
    return jnp.maximum(x, 0.0) + jnp.log1p(jnp.exp(-jnp.abs(x)))


PROJ_TM = 1024
PROJ_TN = 512
N_PRE_TILES = 4 * WIDTH // PROJ_TN
N_SB_TILES = WIDTH // PROJ_TN
HEADS_PER_TILE = PROJ_TN // HEAD_DIM


def _proj_kernel(x_ref, wa_ref, wsb_ref, wg_ref, pre_ref, ba_ref, q_ref, kout_ref, vout_ref,
                 xb_ref, *, tm):
    j = pl.program_id(1)

    @pl.when(j == 0)
    def _():
        xb_ref[...] = x_ref[...].astype(BF16)
        ba_ref[...] = _dot(xb_ref[...], wg_ref[...])

    @pl.when(j < N_PRE_TILES)
    def _():
        pre_ref[...] = _dot(xb_ref[...], wa_ref[...])

    @pl.when((j >= N_PRE_TILES) & (j < N_PRE_TILES + N_SB_TILES))
    def _():
        q_ref[...] = _dot(xb_ref[...], wsb_ref[...]).astype(BF16)

    def keep(out_ref, first_head):
        res = _dot(xb_ref[...], wsb_ref[...])
        for h in range(HEADS_PER_TILE):
            out_ref[pl.ds(first_head + h, tm, stride=N_HEADS), :] = res[:, h * HEAD_DIM:(h + 1) * HEAD_DIM]

    for t in range(N_SB_TILES):
        pl.when(j == N_PRE_TILES + N_SB_TILES + t)(
            functools.partial(keep, kout_ref, t * HEADS_PER_TILE))
        pl.when(j == N_PRE_TILES + 2 * N_SB_TILES + t)(
            functools.partial(keep, vout_ref, t * HEADS_PER_TILE))


def _proj(x, w_main, w_gate, name):
    w_in, w_sb = w_main
    m, k = x.shape
    tm = _pick(m, PROJ_TM)
    nt = N_PRE_TILES + 3 * N_SB_TILES
    row = lambda w: pl.BlockSpec((tm, w), lambda i, j: (i, 0))
    return pl.pallas_call(
        functools.partial(_proj_kernel, tm=tm),
        grid=(m // tm, nt),
        in_specs=[row(k),
                  pl.BlockSpec((k, PROJ_TN), lambda i, j: (0, jnp.minimum(j, N_PRE_TILES - 1))),
                  pl.BlockSpec((k, PROJ_TN), lambda i, j: (0, jnp.maximum(j - N_PRE_TILES, 0))),
                  pl.BlockSpec((k, LANES), lambda i, j: (0, 0))],
        out_specs=[pl.BlockSpec((tm, PROJ_TN), lambda i, j: (i, jnp.minimum(j, N_PRE_TILES - 1))),
                   row(LANES),
                   pl.BlockSpec((tm, PROJ_TN),
                                lambda i, j: (i, jnp.clip(j - N_PRE_TILES, 0, N_SB_TILES - 1))),
                   pl.BlockSpec((tm * N_HEADS, HEAD_DIM), lambda i, j: (i, 0)),
                   pl.BlockSpec((tm * N_HEADS, HEAD_DIM), lambda i, j: (i, 0))],
        out_shape=[jax.ShapeDtypeStruct((m, 4 * WIDTH), F32),
                   jax.ShapeDtypeStruct((m, LANES), F32),
                   jax.ShapeDtypeStruct((m, WIDTH), BF16),
                   jax.ShapeDtypeStruct((m * N_HEADS, HEAD_DIM), F32),
                   jax.ShapeDtypeStruct((m * N_HEADS, HEAD_DIM), F32)],
        scratch_shapes=[pltpu.VMEM((tm, k), BF16)],
        compiler_params=pltpu.CompilerParams(
            dimension_semantics=("parallel", "arbitrary"), vmem_limit_bytes=VMEM_LIMIT),
        name=name,
    )(x, w_in, w_sb, w_gate)


NN = (((1,), (0,)), ((), ()))
NT = (((1,), (1,)), ((), ()))
TN = (((0,), (0,)), ((), ()))

GDN_ROWS_PER_STEP = 512
GDN_PIECES = dict(qk=1, inv=1, uw=1, scan=1)


def _pieces(x, n):
    out = []
    r = x
    for i in range(n):
        p = r.astype(BF16)
        out.append(p)
        if i + 1 < n:
            r = r - p.astype(F32)
    return out


def _mm(a, b, dims=NN):
    order = max(len(a), len(b))
    acc = None
    for i, ai in enumerate(a):
        for j, bj in enumerate(b):
            if i + j < order:
                t = lax.dot_general(ai, bj, dims, preferred_element_type=F32)
                acc = t if acc is None else acc + t
    return acc


def _gdn_kernel(q_ref, k_ref, v_ref, z_ref, ba_ref, cq_ref, ck_ref, cv_ref,
                wq_ref, wk_ref, wv_ref, s0_ref, alog_ref, dtb_ref, nw_ref,
                o_ref, s_ref, tail_ref, *, nb, rows, chunk):
    C = chunk
    probs = [(bi, h) for bi in range(nb) for h in range(N_HEADS)]
    P = range(len(probs))
    row = lax.broadcasted_iota(jnp.int32, (C, C), 0)
    col = lax.broadcasted_iota(jnp.int32, (C, C), 1)
    incl = row >= col
    strict = row > col
    upper = row <= col
    l_incl = [incl.astype(BF16)]
    ones = [jnp.ones((C, C), BF16)]
    eye = (row == col).astype(F32)
    lane = lax.broadcasted_iota(jnp.int32, (C, LANES), 1)
    level_masks = []
    b = 1
    while b < C:
        s = b.bit_length() - 1
        same = (row >> (s + 1)) == (col >> (s + 1))
        level_masks.append(same & (((row >> s) & 1) == 1) & (((col >> s) & 1) == 0))
        b *= 2
    n_qk, n_inv, n_uw, n_scan = (GDN_PIECES[c] for c in ("qk", "inv", "uw", "scan"))

    neg_a = -jnp.exp(alog_ref[...])
    dtb = dtb_ref[...]
    nw = nw_ref[...]
    wq = wq_ref[...]
    wk = wk_ref[...]
    wv = wv_ref[...]

    @pl.when(pl.program_id(1) == 0)
    def _():
        s_ref[...] = s0_ref[...]
        tail_ref[0] = cq_ref[...]
        tail_ref[1] = ck_ref[...]
        tail_ref[2] = cv_ref[...]

    def conv(e, w):
        acc = e[5:5 + C] * w[0:1]
        for i in range(1, CONV_W):
            acc = acc + e[5 + i:5 + i + C] * w[i:i + 1]
        return acc * _sigmoid(acc)

    def head(x, i):
        return x[:, i * HEAD_DIM:(i + 1) * HEAD_DIM]

    def body(n, carry):
        tq, tk, tv = carry
        c0 = pl.multiple_of(n * C, C)
        xq = [q_ref[bi, pl.ds(c0, C), :] for bi in range(nb)]
        xk = [k_ref[bi, pl.ds(c0, C), :] for bi in range(nb)]
        xv = [v_ref[bi, pl.ds(c0, C), :] for bi in range(nb)]
        q_all = [conv(jnp.concatenate([tq[bi], xq[bi]], axis=0), wq) for bi in range(nb)]
        k_all = [conv(jnp.concatenate([tk[bi], xk[bi]], axis=0), wk) for bi in range(nb)]
        v_all = [conv(jnp.concatenate([tv[bi], xv[bi]], axis=0), wv) for bi in range(nb)]
        gate_all = []
        beta_all = []
        g_all = []
        for bi in range(nb):
            zc = z_ref[bi, pl.ds(c0, C), :]
            gate_all.append(zc * _sigmoid(zc))
            ba = ba_ref[bi, pl.ds(c0, C), :]
            beta_all.append(_sigmoid(ba))
            g_all.append(neg_a * _softplus(ba + dtb))

        q = [head(q_all[bi], h) for bi, h in probs]
        k = [head(k_all[bi], h) for bi, h in probs]
        q = [x * lax.rsqrt(jnp.sum(x * x, axis=-1, keepdims=True) + L2_EPS) * (HEAD_DIM ** -0.5)
             for x in q]
        k = [x * lax.rsqrt(jnp.sum(x * x, axis=-1, keepdims=True) + L2_EPS) for x in k]
        beta = [jnp.sum(jnp.where(lane == h, beta_all[bi], 0.0), axis=-1, keepdims=True)
                for bi, h in probs]
        g = [jnp.sum(jnp.where(lane == h + N_HEADS, g_all[bi], 0.0), axis=-1, keepdims=True)
             for bi, h in probs]
        k_beta = [k[i] * beta[i] for i in P]
        kp = [_pieces(k[i], n_qk) for i in P]

        gcb = [_mm(l_incl, _pieces(jnp.broadcast_to(g[i], (C, LANES)), 3)) for i in P]
        gc_row = [_mm(ones, _pieces(jnp.where(upper, jnp.broadcast_to(g[i], (C, C)), 0.0), 3))
                  for i in P]
        kk = [_mm(_pieces(k_beta[i], n_qk), kp[i], NT) for i in P]
        qk = [_mm(_pieces(q[i], n_qk), kp[i], NT) for i in P]

        decay = [jnp.where(incl, jnp.exp(gcb[i][:, :C] - gc_row[i]), 0.0) for i in P]
        egc = [jnp.exp(gcb[i]) for i in P]
        g_last = [gcb[i][C - 1:C, :] for i in P]
        m = [jnp.where(strict, kk[i] * decay[i], 0.0) for i in P]
        attn = [qk[i] * decay[i] for i in P]

        t_inv = [eye - jnp.where(level_masks[0], m[i], 0.0) for i in P]
        for mask in level_masks[1:]:
            tp = [_pieces(t_inv[i], n_inv) for i in P]
            x = [_mm(tp[i], _pieces(jnp.where(mask, m[i], 0.0), n_inv)) for i in P]
            y = [_mm(_pieces(x[i], n_inv), tp[i]) for i in P]
            t_inv = [t_inv[i] - y[i] for i in P]

        rhs = [jnp.concatenate([head(v_all[probs[i][0]], probs[i][1]) * beta[i], k_beta[i] * egc[i]],
                               axis=1) for i in P]
        uw = [_mm(_pieces(t_inv[i], n_uw), _pieces(rhs[i], n_uw)) for i in P]

        s_old = [s_ref[bi, h] for bi, h in probs]
        lhs = [jnp.concatenate([uw[i][:, HEAD_DIM:], q[i] * egc[i]], axis=0) for i in P]
        ws = [_mm(_pieces(lhs[i], n_scan), _pieces(s_old[i], n_scan)) for i in P]
        v_new = [uw[i][:, :HEAD_DIM] - ws[i][:C] for i in P]
        vp = [_pieces(v_new[i], n_scan) for i in P]
        k_tail = [k[i] * jnp.exp(g_last[i] - gcb[i]) for i in P]

        o = [ws[i][C:] + _mm(_pieces(attn[i], n_scan), vp[i]) for i in P]
        s_new = [s_old[i] * jnp.exp(g_last[i]) + _mm(_pieces(k_tail[i], n_scan), vp[i], TN)
                 for i in P]
        for i, (bi, h) in enumerate(probs):
            s_ref[bi, h] = s_new[i]
            on = (o[i] * lax.rsqrt(jnp.mean(o[i] * o[i], axis=-1, keepdims=True) + RMS_EPS)
                  * nw * head(gate_all[bi], h))
            o_ref[bi, pl.ds(c0, C), h * HEAD_DIM:(h + 1) * HEAD_DIM] = on.astype(o_ref.dtype)
        last = lambda xs: jnp.stack([x[C - 8:C] for x in xs])
        return last(xq), last(xk), last(xv)

    tails = lax.fori_loop(0, rows // C, body, (tail_ref[0], tail_ref[1], tail_ref[2]))
    for j in range(3):
        tail_ref[j] = tails[j]


GDN_BATCH_PER_STEP = 2


def _gdn(pre, ba, conv8, s0, conv_w, alog_row, dtb_row, nw_row, name):
    bsz, seq, _ = pre.shape
    chunk = CHUNK if seq % CHUNK == 0 else seq
    rows = _pick(seq, GDN_ROWS_PER_STEP)
    nb = _pick(bsz, GDN_BATCH_PER_STEP)

    def col(grp):
        return pl.BlockSpec((nb, rows, WIDTH), lambda b, t: (b, t, grp))

    def st(grp):
        return pl.BlockSpec((nb, SUBLANES, WIDTH), lambda b, t: (b, 0, grp))

    def cw(grp):
        return pl.BlockSpec((CONV_W, WIDTH), lambda b, t: (0, grp))

    row = pl.BlockSpec((1, LANES), lambda b, t: (0, 0))
    state = pl.BlockSpec((nb, N_HEADS, HEAD_DIM, HEAD_DIM), lambda b, t: (b, 0, 0, 0))
    return pl.pallas_call(
        functools.partial(_gdn_kernel, nb=nb, rows=rows, chunk=chunk),
        grid=(bsz // nb, seq // rows),
        in_specs=[col(0), col(1), col(2), col(3),
                  pl.BlockSpec((nb, rows, LANES), lambda b, t: (b, t, 0)),
                  st(0), st(1), st(2), cw(0), cw(1), cw(2), state, row, row, row],
        out_specs=[pl.BlockSpec((nb, rows, WIDTH), lambda b, t: (b, t, 0)), state],
        out_shape=[jax.ShapeDtypeStruct((bsz, seq, WIDTH), BF16),
                   jax.ShapeDtypeStruct((bsz, N_HEADS, HEAD_DIM, HEAD_DIM), F32)],
        scratch_shapes=[pltpu.VMEM((3, nb, SUBLANES, WIDTH), F32)],
        compiler_params=pltpu.CompilerParams(
            dimension_semantics=("parallel", "arbitrary"), vmem_limit_bytes=VMEM_LIMIT),
        name=name,
    )(pre, pre, pre, pre, ba, conv8, conv8, conv8, conv_w, conv_w, conv_w, s0,
      alog_row, dtb_row, nw_row)


SB_KEYS = 256
SB_DEAD = -104.0
SB_BOUND_SLACK = 1.05


def _suffix_ones(n):
    r = lax.broadcasted_iota(jnp.int32, (n, n), 0)
    c = lax.broadcasted_iota(jnp.int32, (n, n), 1)
    return (r >= c).astype(BF16)


def _sb_split(z, mask):
    l1m = -(jnp.maximum(z, 0.0) + jnp.log(1.0 + jnp.exp(-jnp.abs(z))))
    if mask is not None:
        l1m = jnp.where(mask, l1m, 0.0)
    hi = l1m.astype(BF16)
    return hi, (l1m - hi.astype(F32)).astype(BF16)


def _sb_sums(hi, lo, suffix):
    tk = suffix.shape[0]
    tiles = [slice(s * tk, (s + 1) * tk) for s in range(hi.shape[1] // tk)]
    return [_dot(hi[:, t], suffix) + _dot(lo[:, t], suffix) for t in tiles]


def _sb_weights(z, cs, mask, run):
    n = len(cs)
    reps = cs[0].shape[1] // LANES
    tot = [None] * n
    for s in reversed(range(n)):
        tot[s] = run if reps == 1 else jnp.concatenate([run] * reps, axis=1)
        run = run + jnp.broadcast_to(cs[s][:, 0:1], run.shape)
    off = [cs[s] + tot[s] for s in range(n)]
    w = jnp.exp(z + (off[0] if n == 1 else jnp.concatenate(off, axis=1)))
    if mask is not None:
        w = jnp.where(mask, w, 0.0)
    return w.astype(BF16), run


def _sb_prompt_kernel(q_ref, k_ref, v_ref, o_ref, kb_ref, vb_ref, kmax_ref, *, tq, hb, scale):
    qi = pl.program_id(2)
    heads = range(hb)
    ones = jnp.ones((HEAD_DIM, LANES), BF16)

    def head(x, h):
        return x[:, h * HEAD_DIM:(h + 1) * HEAD_DIM]

    @pl.when(qi == 0)
    def _():
        seq = k_ref.shape[0] // N_HEADS
        zeros = jnp.zeros((tq, HEAD_DIM), BF16)
        for h in heads:
            first = pl.program_id(1) * hb + h
            kh = k_ref[pl.ds(first, seq, stride=N_HEADS), :]
            kb_ref[h, 0:tq, :] = zeros
            vb_ref[h, 0:tq, :] = zeros
            kb_ref[h, tq:, :] = kh.astype(BF16)
            vb_ref[h, tq:, :] = v_ref[pl.ds(first, seq, stride=N_HEADS), :].astype(BF16)
            ksq = _dot((kh * kh).astype(BF16), ones)
            kmax_ref[h] = jnp.broadcast_to(jnp.max(ksq, axis=0, keepdims=True), kmax_ref.shape[1:])

    q = q_ref[...]
    qb = [head(q, h).astype(BF16) for h in heads]
    zbound = [jnp.sqrt(_dot((head(q, h) * head(q, h)).astype(BF16), ones) * kmax_ref[h, 0:1, :])
              * (scale * SB_BOUND_SLACK) for h in heads]
    lr = _suffix_ones(SB_KEYS)

    def sweep(row0, nkeys, mask, acc, run):
        z = [_dot_nt(qb[h], kb_ref[h, pl.ds(row0, nkeys), :]) * scale for h in heads]
        pieces = [_sb_split(z[h], mask) for h in heads]
        ct = [_sb_sums(*pieces[h], lr) for h in heads]
        wr = [_sb_weights(z[h], ct[h], mask, run[h]) for h in heads]
        acc = [acc[h] + _dot(wr[h][0], vb_ref[h, pl.ds(row0, nkeys), :]) for h in heads]
        return acc, [wr[h][1] for h in heads]

    def alive(run):
        live = jnp.max(run[0] + zbound[0]) > SB_DEAD
        for h in heads[1:]:
            live = live | (jnp.max(run[h] + zbound[h]) > SB_DEAD)
        return live

    r = lax.broadcasted_iota(jnp.int32, (tq, 2 * tq), 0)
    c = lax.broadcasted_iota(jnp.int32, (tq, 2 * tq), 1)
    mask = (c < r + tq) & (c + qi * tq >= tq)
    acc, run = sweep(pl.multiple_of(qi * tq, tq), 2 * tq, mask,
                     [jnp.zeros((tq, HEAD_DIM), F32) for _ in heads],
                     [jnp.zeros((tq, LANES), F32) for _ in heads])

    def cond(carry):
        return (carry[0] >= 0) & carry[3]

    def body(carry):
        j, acc, run, _ = carry
        acc, run = sweep(pl.multiple_of((j + 1) * tq, tq), tq, None, acc, run)
        return j - 1, acc, run, alive(run)

    _, acc, _, _ = lax.while_loop(cond, body, (qi - 2, acc, run, alive(run)))
    o_ref[...] = jnp.concatenate(acc, axis=1).astype(o_ref.dtype)


SB_HEADS_PER_STEP = 2
SB_SAMPLE_KEYS = 2048


def _sb_prompt(q, k, v, name):
    bsz, seq, _ = q.shape
    tq = _pick(seq, 256)
    hb = SB_HEADS_PER_STEP
    wid = hb * HEAD_DIM
    kv = pl.BlockSpec((None, seq * N_HEADS, HEAD_DIM), lambda b, h, i: (b, 0, 0))
    qo = pl.BlockSpec((None, tq, wid), lambda b, h, i: (b, i, h))
    return pl.pallas_call(
        functools.partial(_sb_prompt_kernel, tq=tq, hb=hb, scale=HEAD_DIM ** -0.5),
        grid=(bsz, N_HEADS // hb, seq // tq),
        in_specs=[qo, kv, kv],
        out_specs=qo,
        out_shape=jax.ShapeDtypeStruct((bsz, seq, WIDTH), BF16),
        scratch_shapes=[pltpu.VMEM((hb, seq + tq, HEAD_DIM), BF16),
                        pltpu.VMEM((hb, seq + tq, HEAD_DIM), BF16),
                        pltpu.VMEM((hb, SUBLANES, LANES), F32)],
        compiler_params=pltpu.CompilerParams(
            dimension_semantics=("parallel", "parallel", "arbitrary"),
            vmem_limit_bytes=VMEM_LIMIT),
        name=name,
    )(q, k, v)


def _sb_sample_kernel(q_ref, kn_ref, vn_ref, kc_ref, vc_ref, o_ref, qb_ref, acc_ref, run_ref,
                      *, seq, keys, scale):
    step = pl.program_id(1)
    heads = range(N_HEADS)
    rows = N_HEADS * seq
    def sweep(k_tiles, v_tiles, mask, lr):
        z = jnp.concatenate([_dot_nt(qb_ref[h * seq:(h + 1) * seq, :], k_tiles[h]) for h in heads],
                            axis=0) * scale
        wb, run = _sb_weights(z, _sb_sums(*_sb_split(z, mask), lr), mask, run_ref[...])
        run_ref[...] = run
        pv = [_dot(wb[h * seq:(h + 1) * seq, :], v_tiles[h]) for h in heads]
        acc_ref[...] += jnp.concatenate(pv, axis=0)

    @pl.when(step == 0)
    def _():
        for h in heads:
            qb_ref[h * seq:(h + 1) * seq, :] = q_ref[:, h * HEAD_DIM:(h + 1) * HEAD_DIM].astype(BF16)
        acc_ref[...] = jnp.zeros_like(acc_ref)
        run_ref[...] = jnp.zeros_like(run_ref)
        pad = jnp.zeros((LANES - seq, HEAD_DIM), BF16)
        kn = [jnp.concatenate([kn_ref[pl.ds(h, seq, stride=N_HEADS), :].astype(BF16), pad], axis=0)
              for h in heads]
        vn = [jnp.concatenate([vn_ref[pl.ds(h, seq, stride=N_HEADS), :].astype(BF16), pad], axis=0)
              for h in heads]
        r = lax.broadcasted_iota(jnp.int32, (rows, LANES), 0)
        c = lax.broadcasted_iota(jnp.int32, (rows, LANES), 1)
        sweep(kn, vn, c < (r & (seq - 1)), _suffix_ones(LANES))

    kt = [kc_ref[pl.ds(h, keys, stride=N_HEADS), :].astype(BF16) for h in heads]
    vt = [vc_ref[pl.ds(h, keys, stride=N_HEADS), :].astype(BF16) for h in heads]
    sweep(kt, vt, None, _suffix_ones(SB_KEYS))

    @pl.when(step == pl.num_programs(1) - 1)
    def _():
        for h in heads:
            o_ref[:, h * HEAD_DIM:(h + 1) * HEAD_DIM] = acc_ref[h * seq:(h + 1) * seq, :].astype(o_ref.dtype)


def _sb_sample(q, kn, vn, kc, vc, name):
    bsz, seq, _ = q.shape
    past = kc.shape[1]
    kc = kc.reshape(bsz, past * N_HEADS, HEAD_DIM)
    vc = vc.reshape(bsz, past * N_HEADS, HEAD_DIM)
    assert seq & (seq - 1) == 0 and seq <= LANES and past % SB_KEYS == 0
    keys = _pick(past, SB_SAMPLE_KEYS)
    nsteps = past // keys
    new = pl.BlockSpec((None, seq, WIDTH), lambda b, j: (b, 0, 0))
    newkv = pl.BlockSpec((None, seq * N_HEADS, HEAD_DIM), lambda b, j: (b, 0, 0))
    old = pl.BlockSpec((None, keys * N_HEADS, HEAD_DIM), lambda b, j: (b, nsteps - 1 - j, 0))
    rows = N_HEADS * seq
    return pl.pallas_call(
        functools.partial(_sb_sample_kernel, seq=seq, keys=keys, scale=HEAD_DIM ** -0.5),
        grid=(bsz, nsteps),
        in_specs=[new, newkv, newkv, old, old],
        out_specs=new,
        out_shape=jax.ShapeDtypeStruct((bsz, seq, WIDTH), BF16),
        scratch_shapes=[pltpu.VMEM((rows, HEAD_DIM), BF16), pltpu.VMEM((rows, HEAD_DIM), F32),
                        pltpu.VMEM((rows, LANES), F32)],
        compiler_params=pltpu.CompilerParams(
            dimension_semantics=("parallel", "arbitrary"), vmem_limit_bytes=VMEM_LIMIT),
        name=name,
    )(q, kn, vn, kc, vc)


def _layer_norm(r, g, b):
    mu = jnp.mean(r, axis=-1, keepdims=True)
    d = r - mu
    var = jnp.mean(d * d, axis=-1, keepdims=True)
    return d * lax.rsqrt(var + LN_EPS) * g + b


def _outproj_kernel(oa_ref, ob_ref, x_ref, wa_ref, wb_ref, g_ref, b_ref, y_ref, *, alpha, sub):
    for s in range(y_ref.shape[0] // sub):
        rs = slice(s * sub, (s + 1) * sub)
        acc = _dot(oa_ref[rs, :], wa_ref[...]) + _dot(ob_ref[rs, :], wb_ref[...])
        y_ref[rs, :] = _layer_norm(alpha * x_ref[rs, :] + acc, g_ref[...], b_ref[...])


def _outproj_ln(oa, ob, x, wa, wb, g, b, alpha, name):
    m, d = x.shape
    tm = _pick(m, 512)
    rows = lambda w: pl.BlockSpec((tm, w), lambda i: (i, 0))
    full = lambda a: pl.BlockSpec(a.shape, lambda i: (0, 0))
    return pl.pallas_call(
        functools.partial(_outproj_kernel, alpha=alpha, sub=_pick(tm, 256)),
        grid=(m // tm,),
        in_specs=[rows(WIDTH), rows(WIDTH), rows(d), full(wa), full(wb), full(g), full(b)],
        out_specs=rows(d),
        out_shape=jax.ShapeDtypeStruct((m, d), F32),
        compiler_params=pltpu.CompilerParams(
            dimension_semantics=("parallel",), vmem_limit_bytes=VMEM_LIMIT),
        name=name,
    )(oa, ob, x, wa, wb, g, b)


FFN_TM = 512
FFN_TF = 1024


def _ffn_kernel(x_ref, wu_ref, wd_ref, g_ref, b_ref, y_ref, xb_ref, *, alpha):
    j = pl.program_id(1)

    @pl.when(j == 0)
    def _():
        xb_ref[...] = x_ref[...].astype(BF16)
        y_ref[...] = jnp.zeros_like(y_ref)

    hid = jnp.maximum(_dot(xb_ref[...], wu_ref[...]), 0.0)
    y_ref[...] += _dot((hid * hid).astype(BF16), wd_ref[...])

    @pl.when(j == pl.num_programs(1) - 1)
    def _():
        y_ref[...] = _layer_norm(alpha * x_ref[...] + y_ref[...], g_ref[...], b_ref[...])


def _ffn_ln(x, wu, wd, g, b, alpha, name):
    m, d = x.shape
    dff = wu.shape[1]
    tm = _pick(m, FFN_TM)
    tf = _pick(dff, FFN_TF)
    return pl.pallas_call(
        functools.partial(_ffn_kernel, alpha=alpha),
        grid=(m // tm, dff // tf),
        in_specs=[pl.BlockSpec((tm, d), lambda i, j: (i, 0)),
                  pl.BlockSpec((d, tf), lambda i, j: (0, j)),
                  pl.BlockSpec((tf, d), lambda i, j: (j, 0)),
                  pl.BlockSpec((1, d), lambda i, j: (0, 0)),
                  pl.BlockSpec((1, d), lambda i, j: (0, 0))],
        out_specs=pl.BlockSpec((tm, d), lambda i, j: (i, 0)),
        out_shape=jax.ShapeDtypeStruct((m, d), F32),
        scratch_shapes=[pltpu.VMEM((tm, d), BF16)],
        compiler_params=pltpu.CompilerParams(
            dimension_semantics=("parallel", "arbitrary"), vmem_limit_bytes=VMEM_LIMIT),
        name=name,
    )(x, wu, wd, g, b)


def _layer(x, conv_buf, s0, k_past, v_past, wts, alpha, tag):
    (w_main, w_gate, conv_w, alog_row, dtb_row, nw_row,
     wo_a, wo_b, ln1_g, ln1_b, w_up, w_down, ln2_g, ln2_b) = wts
    bsz, seq, d = x.shape
    m = bsz * seq
    x2 = x.reshape(m, d)

    pre, ba, q_b, k_out, v_out = _proj(x2, w_main, w_gate, "proj_" + tag)
    pre = pre.reshape(bsz, seq, 4 * WIDTH)
    ba = ba.reshape(bsz, seq, LANES)
    q_b = q_b.reshape(bsz, seq, WIDTH)
    k_b = k_out.reshape(bsz, seq * N_HEADS, HEAD_DIM)
    v_b = v_out.reshape(bsz, seq * N_HEADS, HEAD_DIM)

    conv8 = jnp.pad(conv_buf, ((0, 0), (SUBLANES - (CONV_W - 1), 0), (0, 0)))
    o_a, s_new = _gdn(pre, ba, conv8, s0, conv_w, alog_row, dtb_row, nw_row, "gdn_" + tag)
    keep = CONV_W - 1
    conv_new = pre[:, seq - keep:, :3 * WIDTH] if seq >= keep else jnp.concatenate(
        [conv_buf, pre[:, :, :3 * WIDTH]], axis=1)[:, -keep:, :]

    if k_past is None:
        o_b = _sb_prompt(q_b, k_b, v_b, "sb_" + tag)
    else:
        o_b = _sb_sample(q_b, k_b, v_b, k_past, v_past, "sb_" + tag)

    x1 = _outproj_ln(o_a.reshape(m, WIDTH), o_b.reshape(m, WIDTH), x2, wo_a, wo_b,
                     ln1_g, ln1_b, alpha, "outproj_" + tag)
    y = _ffn_ln(x1, w_up, w_down, ln2_g, ln2_b, alpha, "ffn_" + tag)
    return (y.reshape(bsz, seq, d), conv_new, s_new,
            k_out.reshape(bsz, seq, N_HEADS, HEAD_DIM), v_out.reshape(bsz, seq, N_HEADS, HEAD_DIM))


def kernel(x_prompt, x_sample, state_gdn_conv, state_gdn_S, cache_sb_k, cache_sb_v, w_in, conv_w, a_log, dt_bias, gdn_norm_w, w_out, ln1_g, ln1_b, w_up, w_down, ln2_g, ln2_b):
    depth = w_in.shape[0]
    alpha = (2 * depth) ** 0.25
    off_b = 4 * WIDTH
    off_sb = off_b + 2 * N_HEADS
    yp, ys = x_prompt, x_sample
    outs = [[] for _ in range(8)]
    for l in range(depth):
        wl = w_in[l]
        wl = wl.astype(BF16)
        w_gate = jnp.pad(wl[:, off_b:off_sb], ((0, 0), (0, LANES - 2 * N_HEADS)))
        w_main = (wl, wl[:, off_sb:])
        pad_row = lambda v: jnp.pad(v.astype(F32), (N_HEADS, LANES - 2 * N_HEADS)).reshape(1, LANES)
        wts = (w_main, w_gate,
               conv_w[l], pad_row(a_log[l]), pad_row(dt_bias[l]),
               gdn_norm_w[l].astype(F32).reshape(1, HEAD_DIM),
               w_out[l, :WIDTH].astype(BF16), w_out[l, WIDTH:].astype(BF16),
               ln1_g[l].reshape(1, -1), ln1_b[l].reshape(1, -1),
               w_up[l].astype(BF16), w_down[l].astype(BF16),
               ln2_g[l].reshape(1, -1), ln2_b[l].reshape(1, -1))
        bp = yp.shape[0]
        zero_conv = jnp.zeros((bp, CONV_W - 1, 3 * WIDTH), F32)
        zero_s = jnp.zeros((bp, N_HEADS, HEAD_DIM, HEAD_DIM), F32)
        yp, c1, s1, k1, v1 = _layer(yp, zero_conv, zero_s, None, None, wts, alpha, "p%d" % l)
        ys, c2, s2, k2, v2 = _layer(ys, state_gdn_conv[l], state_gdn_S[l],
                                    cache_sb_k[l], cache_sb_v[l], wts, alpha, "s%d" % l)
        for lst, val in zip(outs, (c1, s1, k1, v1, c2, s2, k2, v2)):
            lst.append(val)
    return (yp, ys) + tuple(jnp.stack(o) for o in outs)
```

```python
import functools

import jax
import jax.numpy as jnp
from jax import lax
from jax.experimental import pallas as pl
from jax.experimental.pallas import tpu as pltpu

HEAD_DIM = 128
N_HEADS = 8
WIDTH = N_HEADS * HEAD_DIM
CONV_W = 4
CHUNK = 128
LN_EPS = 1e-5
RMS_EPS = 1e-6
L2_EPS = 1e-6
LANES = 128
SUBLANES = 8
VMEM_LIMIT = 56 * 1024 * 1024

F32 = jnp.float32
BF16 = jnp.bfloat16


def _pick(n, pref):
    if n <= pref:
        return n
    t = pref
    while n % t:
        t //= 2
    return t


def _dot(a, b):
    return jnp.dot(a, b, preferred_element_type=F32)


def _dot_nt(a, b):
    return lax.dot_general(a, b, (((1,), (1,)), ((), ())), preferred_element_type=F32)


def _sigmoid(x):
    return 1.0 / (1.0 + jnp.exp(-x))


def _softplus(x):
    return jnp.maximum(x, 0.0) + jnp.log1p(jnp.exp(-jnp.abs(x)))


PROJ_TM = 1024
PROJ_TN = 512
N_PRE_TILES = 4 * WIDTH // PROJ_TN
N_SB_TILES = WIDTH // PROJ_TN
HEADS_PER_TILE = PROJ_TN // HEAD_DIM


def _proj_kernel(x_ref, wa_ref, wsb_ref, wg_ref, pre_ref, ba_ref, q_ref, kout_ref, vout_ref,
                 xb_ref, *, tm):
    j = pl.program_id(1)

    @pl.when(j == 0)
    def _():
        xb_ref[...] = x_ref[...].astype(BF16)
        ba_ref[...] = _dot(xb_ref[...], wg_ref[...])

    @pl.when(j < N_PRE_TILES)
    def _():
        pre_ref[...] = _dot(xb_ref[...], wa_ref[...])

    @pl.when((j >= N_PRE_TILES) & (j < N_PRE_TILES + N_SB_TILES))
    def _():
        q_ref[...] = _dot(xb_ref[...], wsb_ref[...]).astype(BF16)

    def keep(out_ref, first_head):
        res = _dot(xb_ref[...], wsb_ref[...])
        for h in range(HEADS_PER_TILE):
            out_ref[pl.ds(first_head + h, tm, stride=N_HEADS), :] = res[:, h * HEAD_DIM:(h + 1) * HEAD_DIM]

    for t in range(N_SB_TILES):
        pl.when(j == N_PRE_TILES + N_SB_TILES + t)(
            functools.partial(keep, kout_ref, t * HEADS_PER_TILE))
        pl.when(j == N_PRE_TILES + 2 * N_SB_TILES + t)(
            functools.partial(keep, vout_ref, t * HEADS_PER_TILE))


def _proj(x, w_main, w_gate, name):
    w_in, w_sb = w_main
    m, k = x.shape
    tm = _pick(m, PROJ_TM)
    nt = N_PRE_TILES + 3 * N_SB_TILES
    row = lambda w: pl.BlockSpec((tm, w), lambda i, j: (i, 0))
    return pl.pallas_call(
        functools.partial(_proj_kernel, tm=tm),
        grid=(m // tm, nt),
        in_specs=[row(k),
                  pl.BlockSpec((k, PROJ_TN), lambda i, j: (0, jnp.minimum(j, N_PRE_TILES - 1))),
                  pl.BlockSpec((k, PROJ_TN), lambda i, j: (0, jnp.maximum(j - N_PRE_TILES, 0))),
                  pl.BlockSpec((k, LANES), lambda i, j: (0, 0))],
        out_specs=[pl.BlockSpec((tm, PROJ_TN), lambda i, j: (i, jnp.minimum(j, N_PRE_TILES - 1))),
                   row(LANES),
                   pl.BlockSpec((tm, PROJ_TN),
                                lambda i, j: (i, jnp.clip(j - N_PRE_TILES, 0, N_SB_TILES - 1))),
                   pl.BlockSpec((tm * N_HEADS, HEAD_DIM), lambda i, j: (i, 0)),
                   pl.BlockSpec((tm * N_HEADS, HEAD_DIM), lambda i, j: (i, 0))],
        out_shape=[jax.ShapeDtypeStruct((m, 4 * WIDTH), F32),
                   jax.ShapeDtypeStruct((m, LANES), F32),
                   jax.ShapeDtypeStruct((m, WIDTH), BF16),
                   jax.ShapeDtypeStruct((m * N_HEADS, HEAD_DIM), F32),
                   jax.ShapeDtypeStruct((m * N_HEADS, HEAD_DIM), F32)],
        scratch_shapes=[pltpu.VMEM((tm, k), BF16)],
        compiler_params=pltpu.CompilerParams(
            dimension_semantics=("parallel", "arbitrary"), vmem_limit_bytes=VMEM_LIMIT),
        name=name,
    )(x, w_in, w_sb, w_gate)


NN = (((1,), (0,)), ((), ()))
NT = (((1,), (1,)), ((), ()))
TN = (((0,), (0,)), ((), ()))

GDN_ROWS_PER_STEP = 512
GDN_PIECES = dict(qk=1, inv=1, uw=1, scan=1)


def _pieces(x, n):
    out = []
    r = x
    for i in range(n):
        p = r.astype(BF16)
        out.append(p)
        if i + 1 < n:
            r = r - p.astype(F32)
    return out


def _mm(a, b, dims=NN):
    order = max(len(a), len(b))
    acc = None
    for i, ai in enumerate(a):
        for j, bj in enumerate(b):
            if i + j < order:
                t = lax.dot_general(ai, bj, dims, preferred_element_type=F32)
                acc = t if acc is None else acc + t
    return acc


def _gdn_kernel(q_ref, k_ref, v_ref, z_ref, ba_ref, cq_ref, ck_ref, cv_ref,
                wq_ref, wk_ref, wv_ref, s0_ref, alog_ref, dtb_ref, nw_ref,
                o_ref, s_ref, tail_ref, *, nb, rows, chunk):
    C = chunk
    probs = [(bi, h) for bi in range(nb) for h in range(N_HEADS)]
    P = range(len(probs))
    row = lax.broadcasted_iota(jnp.int32, (C, C), 0)
    col = lax.broadcasted_iota(jnp.int32, (C, C), 1)
    incl = row >= col
    strict = row > col
    upper = row <= col
    l_incl = [incl.astype(BF16)]
    ones = [jnp.ones((C, C), BF16)]
    eye = (row == col).astype(F32)
    lane = lax.broadcasted_iota(jnp.int32, (C, LANES), 1)
    level_masks = []
    b = 1
    while b < C:
        s = b.bit_length() - 1
        same = (row >> (s + 1)) == (col >> (s + 1))
        level_masks.append(same & (((row >> s) & 1) == 1) & (((col >> s) & 1) == 0))
        b *= 2
    n_qk, n_inv, n_uw, n_scan = (GDN_PIECES[c] for c in ("qk", "inv", "uw", "scan"))

    neg_a = -jnp.exp(alog_ref[...])
    dtb = dtb_ref[...]
    nw = nw_ref[...]
    wq = wq_ref[...]
    wk = wk_ref[...]
    wv = wv_ref[...]

    @pl.when(pl.program_id(1) == 0)
    def _():
        s_ref[...] = s0_ref[...]
        tail_ref[0] = cq_ref[...]
        tail_ref[1] = ck_ref[...]
        tail_ref[2] = cv_ref[...]

    def conv(e, w):
        acc = e[5:5 + C] * w[0:1]
        for i in range(1, CONV_W):
            acc = acc + e[5 + i:5 + i + C] * w[i:i + 1]
        return acc * _sigmoid(acc)

    def head(x, i):
        return x[:, i * HEAD_DIM:(i + 1) * HEAD_DIM]

    def body(n, carry):
        tq, tk, tv = carry
        c0 = pl.multiple_of(n * C, C)
        xq = [q_ref[bi, pl.ds(c0, C), :] for bi in range(nb)]
        xk = [k_ref[bi, pl.ds(c0, C), :] for bi in range(nb)]
        xv = [v_ref[bi, pl.ds(c0, C), :] for bi in range(nb)]
        q_all = [conv(jnp.concatenate([tq[bi], xq[bi]], axis=0), wq) for bi in range(nb)]
        k_all = [conv(jnp.concatenate([tk[bi], xk[bi]], axis=0), wk) for bi in range(nb)]
        v_all = [conv(jnp.concatenate([tv[bi], xv[bi]], axis=0), wv) for bi in range(nb)]
        gate_all = []
        beta_all = []
        g_all = []
        for bi in range(nb):
            zc = z_ref[bi, pl.ds(c0, C), :]
            gate_all.append(zc * _sigmoid(zc))
            ba = ba_ref[bi, pl.ds(c0, C), :]
            beta_all.append(_sigmoid(ba))
            g_all.append(neg_a * _softplus(ba + dtb))

        q = [head(q_all[bi], h) for bi, h in probs]
        k = [head(k_all[bi], h) for bi, h in probs]
        q = [x * lax.rsqrt(jnp.sum(x * x, axis=-1, keepdims=True) + L2_EPS) * (HEAD_DIM ** -0.5)
             for x in q]
        k = [x * lax.rsqrt(jnp.sum(x * x, axis=-1, keepdims=True) + L2_EPS) for x in k]
        beta = [jnp.sum(jnp.where(lane == h, beta_all[bi], 0.0), axis=-1, keepdims=True)
                for bi, h in probs]
        g = [jnp.sum(jnp.where(lane == h + N_HEADS, g_all[bi], 0.0), axis=-1, keepdims=True)
             for bi, h in probs]
        k_beta = [k[i] * beta[i] for i in P]
        kp = [_pieces(k[i], n_qk) for i in P]

        gcb = [_mm(l_incl, _pieces(jnp.broadcast_to(g[i], (C, LANES)), 3)) for i in P]
        gc_row = [_mm(ones, _pieces(jnp.where(upper, jnp.broadcast_to(g[i], (C, C)), 0.0), 3))
                  for i in P]
        kk = [_mm(_pieces(k_beta[i], n_qk), kp[i], NT) for i in P]
        qk = [_mm(_pieces(q[i], n_qk), kp[i], NT) for i in P]

        decay = [jnp.where(incl, jnp.exp(gcb[i][:, :C] - gc_row[i]), 0.0) for i in P]
        egc = [jnp.exp(gcb[i]) for i in P]
        g_last = [gcb[i][C - 1:C, :] for i in P]
        m = [jnp.where(strict, kk[i] * decay[i], 0.0) for i in P]
        attn = [qk[i] * decay[i] for i in P]

        t_inv = [eye - jnp.where(level_masks[0], m[i], 0.0) for i in P]
        for mask in level_masks[1:]:
            tp = [_pieces(t_inv[i], n_inv) for i in P]
            x = [_mm(tp[i], _pieces(jnp.where(mask, m[i], 0.0), n_inv)) for i in P]
            y = [_mm(_pieces(x[i], n_inv), tp[i]) for i in P]
            t_inv = [t_inv[i] - y[i] for i in P]

        rhs = [jnp.concatenate([head(v_all[probs[i][0]], probs[i][1]) * beta[i], k_beta[i] * egc[i]],
                               axis=1) for i in P]
        uw = [_mm(_pieces(t_inv[i], n_uw), _pieces(rhs[i], n_uw)) for i in P]

        s_old = [s_ref[bi, h] for bi, h in probs]
        lhs = [jnp.concatenate([uw[i][:, HEAD_DIM:], q[i] * egc[i]], axis=0) for i in P]
        ws = [_mm(_pieces(lhs[i], n_scan), _pieces(s_old[i], n_scan)) for i in P]
        v_new = [uw[i][:, :HEAD_DIM] - ws[i][:C] for i in P]
        vp = [_pieces(v_new[i], n_scan) for i in P]
        k_tail = [k[i] * jnp.exp(g_last[i] - gcb[i]) for i in P]

        o = [ws[i][C:] + _mm(_pieces(attn[i], n_scan), vp[i]) for i in P]
        s_new = [s_old[i] * jnp.exp(g_last[i]) + _mm(_pieces(k_tail[i], n_scan), vp[i], TN)
                 for i in P]
        for i, (bi, h) in enumerate(probs):
            s_ref[bi, h] = s_new[i]
            on = (o[i] * lax.rsqrt(jnp.mean(o[i] * o[i], axis=-1, keepdims=True) + RMS_EPS)
                  * nw * head(gate_all[bi], h))
            o_ref[bi, pl.ds(c0, C), h * HEAD_DIM:(h + 1) * HEAD_DIM] = on.astype(o_ref.dtype)
        last = lambda xs: jnp.stack([x[C - 8:C] for x in xs])
        return last(xq), last(xk), last(xv)

    tails = lax.fori_loop(0, rows // C, body, (tail_ref[0], tail_ref[1], tail_ref[2]))
    for j in range(3):
        tail_ref[j] = tails[j]


GDN_BATCH_PER_STEP = 2
GDN_BATCH_SHORT = 4


def _gdn(pre, ba, conv8, s0, conv_w, alog_row, dtb_row, nw_row, name):
    bsz, seq, _ = pre.shape
    chunk = CHUNK if seq % CHUNK == 0 else seq
    rows = _pick(seq, GDN_ROWS_PER_STEP)
    nb = _pick(bsz, GDN_BATCH_PER_STEP if seq > chunk else GDN_BATCH_SHORT)

    def col(grp):
        return pl.BlockSpec((nb, rows, WIDTH), lambda b, t: (b, t, grp))

    def st(grp):
        return pl.BlockSpec((nb, SUBLANES, WIDTH), lambda b, t: (b, 0, grp))

    def cw(grp):
        return pl.BlockSpec((CONV_W, WIDTH), lambda b, t: (0, grp))

    row = pl.BlockSpec((1, LANES), lambda b, t: (0, 0))
    state = pl.BlockSpec((nb, N_HEADS, HEAD_DIM, HEAD_DIM), lambda b, t: (b, 0, 0, 0))
    return pl.pallas_call(
        functools.partial(_gdn_kernel, nb=nb, rows=rows, chunk=chunk),
        grid=(bsz // nb, seq // rows),
        in_specs=[col(0), col(1), col(2), col(3),
                  pl.BlockSpec((nb, rows, LANES), lambda b, t: (b, t, 0)),
                  st(0), st(1), st(2), cw(0), cw(1), cw(2), state, row, row, row],
        out_specs=[pl.BlockSpec((nb, rows, WIDTH), lambda b, t: (b, t, 0)), state],
        out_shape=[jax.ShapeDtypeStruct((bsz, seq, WIDTH), BF16),
                   jax.ShapeDtypeStruct((bsz, N_HEADS, HEAD_DIM, HEAD_DIM), F32)],
        scratch_shapes=[pltpu.VMEM((3, nb, SUBLANES, WIDTH), F32)],
        compiler_params=pltpu.CompilerParams(
            dimension_semantics=("parallel", "arbitrary"), vmem_limit_bytes=VMEM_LIMIT),
        name=name,
    )(pre, pre, pre, pre, ba, conv8, conv8, conv8, conv_w, conv_w, conv_w, s0,
      alog_row, dtb_row, nw_row)


SB_KEYS = 256
SB_DEAD = -104.0
SB_BOUND_SLACK = 1.05


def _suffix_ones(n):
    r = lax.broadcasted_iota(jnp.int32, (n, n), 0)
    c = lax.broadcasted_iota(jnp.int32, (n, n), 1)
    return (r >= c).astype(BF16)


def _sb_split(z, mask):
    l1m = -(jnp.maximum(z, 0.0) + jnp.log(1.0 + jnp.exp(-jnp.abs(z))))
    if mask is not None:
        l1m = jnp.where(mask, l1m, 0.0)
    hi = l1m.astype(BF16)
    return hi, (l1m - hi.astype(F32)).astype(BF16)


def _sb_sums(hi, lo, suffix):
    tk = suffix.shape[0]
    tiles = [slice(s * tk, (s + 1) * tk) for s in range(hi.shape[1] // tk)]
    return [_dot(hi[:, t], suffix) + _dot(lo[:, t], suffix) for t in tiles]


def _sb_weights(z, cs, mask, run):
    n = len(cs)
    reps = cs[0].shape[1] // LANES
    tot = [None] * n
    for s in reversed(range(n)):
        tot[s] = run if reps == 1 else jnp.concatenate([run] * reps, axis=1)
        run = run + jnp.broadcast_to(cs[s][:, 0:1], run.shape)
    off = [cs[s] + tot[s] for s in range(n)]
    w = jnp.exp(z + (off[0] if n == 1 else jnp.concatenate(off, axis=1)))
    if mask is not None:
        w = jnp.where(mask, w, 0.0)
    return w.astype(BF16), run


def _sb_prompt_kernel(q_ref, k_ref, v_ref, o_ref, kb_ref, vb_ref, kmax_ref, *, tq, hb, scale):
    qi = pl.program_id(2)
    heads = range(hb)
    ones = jnp.ones((HEAD_DIM, LANES), BF16)

    def head(x, h):
        return x[:, h * HEAD_DIM:(h + 1) * HEAD_DIM]

    @pl.when(qi == 0)
    def _():
        zeros = jnp.zeros((tq, HEAD_DIM), BF16)
        for h in heads:
            kb_ref[h, 0:tq, :] = zeros
            vb_ref[h, 0:tq, :] = zeros
            kmax_ref[h] = jnp.zeros(kmax_ref.shape[1:], F32)

    dst = pl.multiple_of((qi + 1) * tq, tq)
    for h in heads:
        first = pl.program_id(1) * hb + h
        kh = k_ref[pl.ds(first, tq, stride=N_HEADS), :]
        kb_ref[h, pl.ds(dst, tq), :] = kh.astype(BF16)
        vb_ref[h, pl.ds(dst, tq), :] = v_ref[pl.ds(first, tq, stride=N_HEADS), :].astype(BF16)
        ksq = _dot((kh * kh).astype(BF16), ones)
        kmax_ref[h] = jnp.maximum(kmax_ref[h], jnp.broadcast_to(
            jnp.max(ksq, axis=0, keepdims=True), kmax_ref.shape[1:]))

    q = q_ref[...]
    qb = [head(q, h).astype(BF16) for h in heads]
    zbound = [jnp.sqrt(_dot((head(q, h) * head(q, h)).astype(BF16), ones) * kmax_ref[h, 0:1, :])
              * (scale * SB_BOUND_SLACK) for h in heads]
    lr = _suffix_ones(SB_KEYS)

    def sweep(row0, nkeys, mask, acc, run):
        z = [_dot_nt(qb[h], kb_ref[h, pl.ds(row0, nkeys), :]) * scale for h in heads]
        pieces = [_sb_split(z[h], mask) for h in heads]
        ct = [_sb_sums(*pieces[h], lr) for h in heads]
        wr = [_sb_weights(z[h], ct[h], mask, run[h]) for h in heads]
        acc = [acc[h] + _dot(wr[h][0], vb_ref[h, pl.ds(row0, nkeys), :]) for h in heads]
        return acc, [wr[h][1] for h in heads]

    def alive(run):
        live = jnp.max(run[0] + zbound[0]) > SB_DEAD
        for h in heads[1:]:
            live = live | (jnp.max(run[h] + zbound[h]) > SB_DEAD)
        return live

    r = lax.broadcasted_iota(jnp.int32, (tq, 2 * tq), 0)
    c = lax.broadcasted_iota(jnp.int32, (tq, 2 * tq), 1)
    mask = (c < r + tq) & (c + qi * tq >= tq)
    acc, run = sweep(pl.multiple_of(qi * tq, tq), 2 * tq, mask,
                     [jnp.zeros((tq, HEAD_DIM), F32) for _ in heads],
                     [jnp.zeros((tq, LANES), F32) for _ in heads])

    def cond(carry):
        return (carry[0] >= 0) & carry[3]

    def body(carry):
        j, acc, run, _ = carry
        acc, run = sweep(pl.multiple_of((j + 1) * tq, tq), tq, None, acc, run)
        return j - 1, acc, run, alive(run)

    _, acc, _, _ = lax.while_loop(cond, body, (qi - 2, acc, run, alive(run)))
    o_ref[...] = jnp.concatenate(acc, axis=1).astype(o_ref.dtype)


SB_HEADS_PER_STEP = 2
SB_SAMPLE_KEYS = 2048


def _sb_prompt(q, k, v, name):
    bsz, seq, _ = q.shape
    tq = _pick(seq, 256)
    hb = SB_HEADS_PER_STEP
    wid = hb * HEAD_DIM
    kv = pl.BlockSpec((None, tq * N_HEADS, HEAD_DIM), lambda b, h, i: (b, i, 0))
    qo = pl.BlockSpec((None, tq, wid), lambda b, h, i: (b, i, h))
    return pl.pallas_call(
        functools.partial(_sb_prompt_kernel, tq=tq, hb=hb, scale=HEAD_DIM ** -0.5),
        grid=(bsz, N_HEADS // hb, seq // tq),
        in_specs=[qo, kv, kv],
        out_specs=qo,
        out_shape=jax.ShapeDtypeStruct((bsz, seq, WIDTH), BF16),
        scratch_shapes=[pltpu.VMEM((hb, seq + tq, HEAD_DIM), BF16),
                        pltpu.VMEM((hb, seq + tq, HEAD_DIM), BF16),
                        pltpu.VMEM((hb, SUBLANES, LANES), F32)],
        compiler_params=pltpu.CompilerParams(
            dimension_semantics=("parallel", "parallel", "arbitrary"),
            vmem_limit_bytes=VMEM_LIMIT),
        name=name,
    )(q, k, v)


def _sb_sample_kernel(q_ref, kn_ref, vn_ref, kc_ref, vc_ref, o_ref, qb_ref, acc_ref, run_ref,
                      *, seq, keys, scale):
    step = pl.program_id(1)
    heads = range(N_HEADS)
    rows = N_HEADS * seq
    def sweep(k_tiles, v_tiles, mask, lr):
        z = jnp.concatenate([_dot_nt(qb_ref[h * seq:(h + 1) * seq, :], k_tiles[h]) for h in heads],
                            axis=0) * scale
        wb, run = _sb_weights(z, _sb_sums(*_sb_split(z, mask), lr), mask, run_ref[...])
        run_ref[...] = run
        pv = [_dot(wb[h * seq:(h + 1) * seq, :], v_tiles[h]) for h in heads]
        acc_ref[...] += jnp.concatenate(pv, axis=0)

    @pl.when(step == 0)
    def _():
        for h in heads:
            qb_ref[h * seq:(h + 1) * seq, :] = q_ref[:, h * HEAD_DIM:(h + 1) * HEAD_DIM].astype(BF16)
        acc_ref[...] = jnp.zeros_like(acc_ref)
        run_ref[...] = jnp.zeros_like(run_ref)
        pad = jnp.zeros((LANES - seq, HEAD_DIM), BF16)
        kn = [jnp.concatenate([kn_ref[pl.ds(h, seq, stride=N_HEADS), :].astype(BF16), pad], axis=0)
              for h in heads]
        vn = [jnp.concatenate([vn_ref[pl.ds(h, seq, stride=N_HEADS), :].astype(BF16), pad], axis=0)
              for h in heads]
        r = lax.broadcasted_iota(jnp.int32, (rows, LANES), 0)
        c = lax.broadcasted_iota(jnp.int32, (rows, LANES), 1)
        sweep(kn, vn, c < (r & (seq - 1)), _suffix_ones(LANES))

    kt = [kc_ref[pl.ds(h, keys, stride=N_HEADS), :].astype(BF16) for h in heads]
    vt = [vc_ref[pl.ds(h, keys, stride=N_HEADS), :].astype(BF16) for h in heads]
    sweep(kt, vt, None, _suffix_ones(SB_KEYS))

    @pl.when(step == pl.num_programs(1) - 1)
    def _():
        for h in heads:
            o_ref[:, h * HEAD_DIM:(h + 1) * HEAD_DIM] = acc_ref[h * seq:(h + 1) * seq, :].astype(o_ref.dtype)


def _sb_sample(q, kn, vn, kc, vc, name):
    bsz, seq, _ = q.shape
    past = kc.shape[1]
    kc = kc.reshape(bsz, past * N_HEADS, HEAD_DIM)
    vc = vc.reshape(bsz, past * N_HEADS, HEAD_DIM)
    assert seq & (seq - 1) == 0 and seq <= LANES and past % SB_KEYS == 0
    keys = _pick(past, SB_SAMPLE_KEYS)
    nsteps = past // keys
    new = pl.BlockSpec((None, seq, WIDTH), lambda b, j: (b, 0, 0))
    newkv = pl.BlockSpec((None, seq * N_HEADS, HEAD_DIM), lambda b, j: (b, 0, 0))
    old = pl.BlockSpec((None, keys * N_HEADS, HEAD_DIM), lambda b, j: (b, nsteps - 1 - j, 0))
    rows = N_HEADS * seq
    return pl.pallas_call(
        functools.partial(_sb_sample_kernel, seq=seq, keys=keys, scale=HEAD_DIM ** -0.5),
        grid=(bsz, nsteps),
        in_specs=[new, newkv, newkv, old, old],
        out_specs=new,
        out_shape=jax.ShapeDtypeStruct((bsz, seq, WIDTH), BF16),
        scratch_shapes=[pltpu.VMEM((rows, HEAD_DIM), BF16), pltpu.VMEM((rows, HEAD_DIM), F32),
                        pltpu.VMEM((rows, LANES), F32)],
        compiler_params=pltpu.CompilerParams(
            dimension_semantics=("parallel", "arbitrary"), vmem_limit_bytes=VMEM_LIMIT),
        name=name,
    )(q, kn, vn, kc, vc)


def _layer_norm(r, g, b):
    mu = jnp.mean(r, axis=-1, keepdims=True)
    d = r - mu
    var = jnp.mean(d * d, axis=-1, keepdims=True)
    return d * lax.rsqrt(var + LN_EPS) * g + b


def _outproj_kernel(oa_ref, ob_ref, x_ref, wa_ref, wb_ref, g_ref, b_ref, y_ref, *, alpha, sub):
    for s in range(y_ref.shape[0] // sub):
        rs = slice(s * sub, (s + 1) * sub)
        acc = _dot(oa_ref[rs, :], wa_ref[...]) + _dot(ob_ref[rs, :], wb_ref[...])
        y_ref[rs, :] = _layer_norm(alpha * x_ref[rs, :] + acc, g_ref[...], b_ref[...])


def _outproj_ln(oa, ob, x, wa, wb, g, b, alpha, name):
    m, d = x.shape
    tm = _pick(m, 512)
    rows = lambda w: pl.BlockSpec((tm, w), lambda i: (i, 0))
    full = lambda a: pl.BlockSpec(a.shape, lambda i: (0, 0))
    return pl.pallas_call(
        functools.partial(_outproj_kernel, alpha=alpha, sub=_pick(tm, 256)),
        grid=(m // tm,),
        in_specs=[rows(WIDTH), rows(WIDTH), rows(d), full(wa), full(wb), full(g), full(b)],
        out_specs=rows(d),
        out_shape=jax.ShapeDtypeStruct((m, d), F32),
        compiler_params=pltpu.CompilerParams(
            dimension_semantics=("parallel",), vmem_limit_bytes=VMEM_LIMIT),
        name=name,
    )(oa, ob, x, wa, wb, g, b)


FFN_TM = 512
FFN_TF = 1024


def _ffn_kernel(x_ref, wu_ref, wd_ref, g_ref, b_ref, y_ref, xb_ref, *, alpha):
    j = pl.program_id(1)

    @pl.when(j == 0)
    def _():
        xb_ref[...] = x_ref[...].astype(BF16)
        y_ref[...] = jnp.zeros_like(y_ref)

    hid = jnp.maximum(_dot(xb_ref[...], wu_ref[...]), 0.0)
    y_ref[...] += _dot((hid * hid).astype(BF16), wd_ref[...])

    @pl.when(j == pl.num_programs(1) - 1)
    def _():
        y_ref[...] = _layer_norm(alpha * x_ref[...] + y_ref[...], g_ref[...], b_ref[...])


def _ffn_ln(x, wu, wd, g, b, alpha, name):
    m, d = x.shape
    dff = wu.shape[1]
    tm = _pick(m, FFN_TM)
    tf = _pick(dff, FFN_TF)
    return pl.pallas_call(
        functools.partial(_ffn_kernel, alpha=alpha),
        grid=(m // tm, dff // tf),
        in_specs=[pl.BlockSpec((tm, d), lambda i, j: (i, 0)),
                  pl.BlockSpec((d, tf), lambda i, j: (0, j)),
                  pl.BlockSpec((tf, d), lambda i, j: (j, 0)),
                  pl.BlockSpec((1, d), lambda i, j: (0, 0)),
                  pl.BlockSpec((1, d), lambda i, j: (0, 0))],
        out_specs=pl.BlockSpec((tm, d), lambda i, j: (i, 0)),
        out_shape=jax.ShapeDtypeStruct((m, d), F32),
        scratch_shapes=[pltpu.VMEM((tm, d), BF16)],
        compiler_params=pltpu.CompilerParams(
            dimension_semantics=("parallel", "arbitrary"), vmem_limit_bytes=VMEM_LIMIT),
        name=name,
    )(x, wu, wd, g, b)


def _layer(x, conv_buf, s0, k_past, v_past, wts, alpha, tag):
    (w_main, w_gate, conv_w, alog_row, dtb_row, nw_row,
     wo_a, wo_b, ln1_g, ln1_b, w_up, w_down, ln2_g, ln2_b) = wts
    bsz, seq, d = x.shape
    m = bsz * seq
    x2 = x.reshape(m, d)

    pre, ba, q_b, k_out, v_out = _proj(x2, w_main, w_gate, "proj_" + tag)
    pre = pre.reshape(bsz, seq, 4 * WIDTH)
    ba = ba.reshape(bsz, seq, LANES)
    q_b = q_b.reshape(bsz, seq, WIDTH)
    k_b = k_out.reshape(bsz, seq * N_HEADS, HEAD_DIM)
    v_b = v_out.reshape(bsz, seq * N_HEADS, HEAD_DIM)

    conv8 = jnp.pad(conv_buf, ((0, 0), (SUBLANES - (CONV_W - 1), 0), (0, 0)))
    o_a, s_new = _gdn(pre, ba, conv8, s0, conv_w, alog_row, dtb_row, nw_row, "gdn_" + tag)
    keep = CONV_W - 1
    conv_new = pre[:, seq - keep:, :3 * WIDTH] if seq >= keep else jnp.concatenate(
        [conv_buf, pre[:, :, :3 * WIDTH]], axis=1)[:, -keep:, :]

    if k_past is None:
        o_b = _sb_prompt(q_b, k_b, v_b, "sb_" + tag)
    else:
        o_b = _sb_sample(q_b, k_b, v_b, k_past, v_past, "sb_" + tag)

    x1 = _outproj_ln(o_a.reshape(m, WIDTH), o_b.reshape(m, WIDTH), x2, wo_a, wo_b,
                     ln1_g, ln1_b, alpha, "outproj_" + tag)
    y = _ffn_ln(x1, w_up, w_down, ln2_g, ln2_b, alpha, "ffn_" + tag)
    return (y.reshape(bsz, seq, d), conv_new, s_new,
            k_out.reshape(bsz, seq, N_HEADS, HEAD_DIM), v_out.reshape(bsz, seq, N_HEADS, HEAD_DIM))


def kernel(x_prompt, x_sample, state_gdn_conv, state_gdn_S, cache_sb_k, cache_sb_v, w_in, conv_w, a_log, dt_bias, gdn_norm_w, w_out, ln1_g, ln1_b, w_up, w_down, ln2_g, ln2_b):
    depth = w_in.shape[0]
    alpha = (2 * depth) ** 0.25
    off_b = 4 * WIDTH
    off_sb = off_b + 2 * N_HEADS
    yp, ys = x_prompt, x_sample
    outs = [[] for _ in range(8)]
    for l in range(depth):
        wl = w_in[l]
        wl = wl.astype(BF16)
        w_gate = jnp.pad(wl[:, off_b:off_sb], ((0, 0), (0, LANES - 2 * N_HEADS)))
        w_main = (wl, wl[:, off_sb:])
        pad_row = lambda v: jnp.pad(v.astype(F32), (N_HEADS, LANES - 2 * N_HEADS)).reshape(1, LANES)
        wts = (w_main, w_gate,
               conv_w[l], pad_row(a_log[l]), pad_row(dt_bias[l]),
               gdn_norm_w[l].astype(F32).reshape(1, HEAD_DIM),
               w_out[l, :WIDTH].astype(BF16), w_out[l, WIDTH:].astype(BF16),
               ln1_g[l].reshape(1, -1), ln1_b[l].reshape(1, -1),
               w_up[l].astype(BF16), w_down[l].astype(BF16),
               ln2_g[l].reshape(1, -1), ln2_b[l].reshape(1, -1))
        bp = yp.shape[0]
        zero_conv = jnp.zeros((bp, CONV_W - 1, 3 * WIDTH), F32)
        zero_s = jnp.zeros((bp, N_HEADS, HEAD_DIM, HEAD_DIM), F32)
        yp, c1, s1, k1, v1 = _layer(yp, zero_conv, zero_s, None, None, wts, alpha, "p%d" % l)
        ys, c2, s2, k2, v2 = _layer(ys, state_gdn_conv[l], state_gdn_S[l],
                                    cache_sb_k[l], cache_sb_v[l], wts, alpha, "s%d" % l)
        for lst, val in zip(outs, (c1, s1, k1, v1, c2, s2, k2, v2)):
            lst.append(val)
    return (yp, ys) + tuple(jnp.stack(o) for o in outs)
```

```python
import functools

import jax
import jax.numpy as jnp
from jax import lax
from jax.experimental import pallas as pl
from jax.experimental.pallas import tpu as pltpu

HEAD_DIM = 128
N_HEADS = 8
WIDTH = N_HEADS * HEAD_DIM
CONV_W = 4
CHUNK = 128
LN_EPS = 1e-5
RMS_EPS = 1e-6
L2_EPS = 1e-6
LANES = 128
SUBLANES = 8
VMEM_LIMIT = 56 * 1024 * 1024

F32 = jnp.float32
BF16 = jnp.bfloat16


def _pick(n, pref):
    if n <= pref:
        return n
    t = pref
    while n % t:
        t //= 2
    return t


def _dot(a, b):
    return jnp.dot(a, b, preferred_element_type=F32)


def _dot_nt(a, b):
    return lax.dot_general(a, b, (((1,), (1,)), ((), ())), preferred_element_type=F32)


def _sigmoid(x):
    return 1.0 / (1.0 + jnp.exp(-x))


def _softplus(x):
    return jnp.maximum(x, 0.0) + jnp.log1p(jnp.exp(-jnp.abs(x)))


W_CAST_TN = 512


def _w_in_cast_kernel(a_ref, b_ref, wa_ref, wsb_ref, wg_ref, *, n_a, n_gate):
    t = pl.program_id(0)

    @pl.when(t < n_a)
    def _():
        wa_ref[...] = a_ref[...].astype(BF16)

    @pl.when(t == n_a)
    def _():
        lane = lax.broadcasted_iota(jnp.int32, wg_ref.shape, 1)
        wg_ref[...] = jnp.where(lane < n_gate, a_ref[:, :LANES], 0.0).astype(BF16)

    @pl.when(t >= n_a)
    def _():
        both = jnp.concatenate([a_ref[...], b_ref[...]], axis=1)
        wsb_ref[...] = both[:, n_gate:n_gate + W_CAST_TN].astype(BF16)


def _w_in_cast(w, off_gate, off_sb, name):
    k, n = w.shape
    n_gate = off_sb - off_gate
    n_a = off_gate // W_CAST_TN
    n_sb = (n - off_sb) // W_CAST_TN
    per = W_CAST_TN // LANES
    return pl.pallas_call(
        functools.partial(_w_in_cast_kernel, n_a=n_a, n_gate=n_gate),
        grid=(n_a + n_sb,),
        in_specs=[pl.BlockSpec((k, W_CAST_TN), lambda t: (0, t)),
                  pl.BlockSpec((k, LANES), lambda t: (0, (t + 1) * per))],
        out_specs=[pl.BlockSpec((k, W_CAST_TN), lambda t: (0, jnp.minimum(t, n_a - 1))),
                   pl.BlockSpec((k, W_CAST_TN), lambda t: (0, jnp.maximum(t - n_a, 0))),
                   pl.BlockSpec((k, LANES), lambda t: (0, 0))],
        out_shape=[jax.ShapeDtypeStruct((k, off_gate), BF16),
                   jax.ShapeDtypeStruct((k, n - off_sb), BF16),
                   jax.ShapeDtypeStruct((k, LANES), BF16)],
        compiler_params=pltpu.CompilerParams(
            dimension_semantics=("arbitrary",), vmem_limit_bytes=VMEM_LIMIT),
        name=name,
    )(w, w)


PROJ_TM = 1024
PROJ_TN = 512
N_PRE_TILES = 4 * WIDTH // PROJ_TN
N_SB_TILES = WIDTH // PROJ_TN
HEADS_PER_TILE = PROJ_TN // HEAD_DIM


def _proj_kernel(x_ref, wa_ref, wsb_ref, wg_ref, pre_ref, ba_ref, q_ref, kout_ref, vout_ref,
                 xb_ref, *, tm):
    j = pl.program_id(1)

    @pl.when(j == 0)
    def _():
        xb_ref[...] = x_ref[...].astype(BF16)
        ba_ref[...] = _dot(xb_ref[...], wg_ref[...])

    @pl.when(j < N_PRE_TILES)
    def _():
        pre_ref[...] = _dot(xb_ref[...], wa_ref[...])

    @pl.when((j >= N_PRE_TILES) & (j < N_PRE_TILES + N_SB_TILES))
    def _():
        q_ref[...] = _dot(xb_ref[...], wsb_ref[...]).astype(BF16)

    def keep(out_ref, first_head):
        res = _dot(xb_ref[...], wsb_ref[...])
        for h in range(HEADS_PER_TILE):
            out_ref[pl.ds(first_head + h, tm, stride=N_HEADS), :] = res[:, h * HEAD_DIM:(h + 1) * HEAD_DIM]

    for t in range(N_SB_TILES):
        pl.when(j == N_PRE_TILES + N_SB_TILES + t)(
            functools.partial(keep, kout_ref, t * HEADS_PER_TILE))
        pl.when(j == N_PRE_TILES + 2 * N_SB_TILES + t)(
            functools.partial(keep, vout_ref, t * HEADS_PER_TILE))


def _proj(x, w_main, w_gate, name):
    w_in, w_sb = w_main
    m, k = x.shape
    tm = _pick(m, PROJ_TM)
    nt = N_PRE_TILES + 3 * N_SB_TILES
    row = lambda w: pl.BlockSpec((tm, w), lambda i, j: (i, 0))
    return pl.pallas_call(
        functools.partial(_proj_kernel, tm=tm),
        grid=(m // tm, nt),
        in_specs=[row(k),
                  pl.BlockSpec((k, PROJ_TN), lambda i, j: (0, jnp.minimum(j, N_PRE_TILES - 1))),
                  pl.BlockSpec((k, PROJ_TN), lambda i, j: (0, jnp.maximum(j - N_PRE_TILES, 0))),
                  pl.BlockSpec((k, LANES), lambda i, j: (0, 0))],
        out_specs=[pl.BlockSpec((tm, PROJ_TN), lambda i, j: (i, jnp.minimum(j, N_PRE_TILES - 1))),
                   row(LANES),
                   pl.BlockSpec((tm, PROJ_TN),
                                lambda i, j: (i, jnp.clip(j - N_PRE_TILES, 0, N_SB_TILES - 1))),
                   pl.BlockSpec((tm * N_HEADS, HEAD_DIM), lambda i, j: (i, 0)),
                   pl.BlockSpec((tm * N_HEADS, HEAD_DIM), lambda i, j: (i, 0))],
        out_shape=[jax.ShapeDtypeStruct((m, 4 * WIDTH), F32),
                   jax.ShapeDtypeStruct((m, LANES), F32),
                   jax.ShapeDtypeStruct((m, WIDTH), BF16),
                   jax.ShapeDtypeStruct((m * N_HEADS, HEAD_DIM), F32),
                   jax.ShapeDtypeStruct((m * N_HEADS, HEAD_DIM), F32)],
        scratch_shapes=[pltpu.VMEM((tm, k), BF16)],
        compiler_params=pltpu.CompilerParams(
            dimension_semantics=("parallel", "arbitrary"), vmem_limit_bytes=VMEM_LIMIT),
        name=name,
    )(x, w_in, w_sb, w_gate)


NN = (((1,), (0,)), ((), ()))
NT = (((1,), (1,)), ((), ()))
TN = (((0,), (0,)), ((), ()))

GDN_ROWS_PER_STEP = 512
GDN_PIECES = dict(qk=1, inv=1, uw=1, scan=1)


def _pieces(x, n):
    out = []
    r = x
    for i in range(n):
        p = r.astype(BF16)
        out.append(p)
        if i + 1 < n:
            r = r - p.astype(F32)
    return out


def _mm(a, b, dims=NN):
    order = max(len(a), len(b))
    acc = None
    for i, ai in enumerate(a):
        for j, bj in enumerate(b):
            if i + j < order:
                t = lax.dot_general(ai, bj, dims, preferred_element_type=F32)
                acc = t if acc is None else acc + t
    return acc


def _gdn_kernel(q_ref, k_ref, v_ref, z_ref, ba_ref, cq_ref, ck_ref, cv_ref,
                wq_ref, wk_ref, wv_ref, s0_ref, alog_ref, dtb_ref, nw_ref,
                o_ref, s_ref, tail_ref, *, nb, rows, chunk):
    C = chunk
    probs = [(bi, h) for bi in range(nb) for h in range(N_HEADS)]
    P = range(len(probs))
    row = lax.broadcasted_iota(jnp.int32, (C, C), 0)
    col = lax.broadcasted_iota(jnp.int32, (C, C), 1)
    incl = row >= col
    strict = row > col
    upper = row <= col
    l_incl = [incl.astype(BF16)]
    ones = [jnp.ones((C, C), BF16)]
    eye = (row == col).astype(F32)
    lane = lax.broadcasted_iota(jnp.int32, (C, LANES), 1)
    level_masks = []
    b = 1
    while b < C:
        s = b.bit_length() - 1
        same = (row >> (s + 1)) == (col >> (s + 1))
        level_masks.append(same & (((row >> s) & 1) == 1) & (((col >> s) & 1) == 0))
        b *= 2
    n_qk, n_inv, n_uw, n_scan = (GDN_PIECES[c] for c in ("qk", "inv", "uw", "scan"))

    neg_a = -jnp.exp(alog_ref[...])
    dtb = dtb_ref[...]
    nw = nw_ref[...]
    wq = wq_ref[...]
    wk = wk_ref[...]
    wv = wv_ref[...]

    @pl.when(pl.program_id(1) == 0)
    def _():
        s_ref[...] = s0_ref[...]
        tail_ref[0] = cq_ref[...]
        tail_ref[1] = ck_ref[...]
        tail_ref[2] = cv_ref[...]

    def conv(e, w):
        acc = e[5:5 + C] * w[0:1]
        for i in range(1, CONV_W):
            acc = acc + e[5 + i:5 + i + C] * w[i:i + 1]
        return acc * _sigmoid(acc)

    def head(x, i):
        return x[:, i * HEAD_DIM:(i + 1) * HEAD_DIM]

    def body(n, carry):
        tq, tk, tv = carry
        c0 = pl.multiple_of(n * C, C)
        xq = [q_ref[bi, pl.ds(c0, C), :] for bi in range(nb)]
        xk = [k_ref[bi, pl.ds(c0, C), :] for bi in range(nb)]
        xv = [v_ref[bi, pl.ds(c0, C), :] for bi in range(nb)]
        q_all = [conv(jnp.concatenate([tq[bi], xq[bi]], axis=0), wq) for bi in range(nb)]
        k_all = [conv(jnp.concatenate([tk[bi], xk[bi]], axis=0), wk) for bi in range(nb)]
        v_all = [conv(jnp.concatenate([tv[bi], xv[bi]], axis=0), wv) for bi in range(nb)]
        gate_all = []
        beta_all = []
        g_all = []
        for bi in range(nb):
            zc = z_ref[bi, pl.ds(c0, C), :]
            gate_all.append(zc * _sigmoid(zc))
            ba = ba_ref[bi, pl.ds(c0, C), :]
            beta_all.append(_sigmoid(ba))
            g_all.append(neg_a * _softplus(ba + dtb))

        q = [head(q_all[bi], h) for bi, h in probs]
        k = [head(k_all[bi], h) for bi, h in probs]
        q = [x * lax.rsqrt(jnp.sum(x * x, axis=-1, keepdims=True) + L2_EPS) * (HEAD_DIM ** -0.5)
             for x in q]
        k = [x * lax.rsqrt(jnp.sum(x * x, axis=-1, keepdims=True) + L2_EPS) for x in k]
        beta = [jnp.sum(jnp.where(lane == h, beta_all[bi], 0.0), axis=-1, keepdims=True)
                for bi, h in probs]
        g = [jnp.sum(jnp.where(lane == h + N_HEADS, g_all[bi], 0.0), axis=-1, keepdims=True)
             for bi, h in probs]
        k_beta = [k[i] * beta[i] for i in P]
        kp = [_pieces(k[i], n_qk) for i in P]

        gcb = [_mm(l_incl, _pieces(jnp.broadcast_to(g[i], (C, LANES)), 3)) for i in P]
        gc_row = [_mm(ones, _pieces(jnp.where(upper, jnp.broadcast_to(g[i], (C, C)), 0.0), 3))
                  for i in P]
        kk = [_mm(_pieces(k_beta[i], n_qk), kp[i], NT) for i in P]
        qk = [_mm(_pieces(q[i], n_qk), kp[i], NT) for i in P]

        decay = [jnp.where(incl, jnp.exp(gcb[i][:, :C] - gc_row[i]), 0.0) for i in P]
        egc = [jnp.exp(gcb[i]) for i in P]
        g_last = [gcb[i][C - 1:C, :] for i in P]
        m = [jnp.where(strict, kk[i] * decay[i], 0.0) for i in P]
        attn = [qk[i] * decay[i] for i in P]

        t_inv = [eye - jnp.where(level_masks[0], m[i], 0.0) for i in P]
        for mask in level_masks[1:]:
            tp = [_pieces(t_inv[i], n_inv) for i in P]
            x = [_mm(tp[i], _pieces(jnp.where(mask, m[i], 0.0), n_inv)) for i in P]
            y = [_mm(_pieces(x[i], n_inv), tp[i]) for i in P]
            t_inv = [t_inv[i] - y[i] for i in P]

        rhs = [jnp.concatenate([head(v_all[probs[i][0]], probs[i][1]) * beta[i], k_beta[i] * egc[i]],
                               axis=1) for i in P]
        uw = [_mm(_pieces(t_inv[i], n_uw), _pieces(rhs[i], n_uw)) for i in P]

        s_old = [s_ref[bi, h] for bi, h in probs]
        lhs = [jnp.concatenate([uw[i][:, HEAD_DIM:], q[i] * egc[i]], axis=0) for i in P]
        ws = [_mm(_pieces(lhs[i], n_scan), _pieces(s_old[i], n_scan)) for i in P]
        v_new = [uw[i][:, :HEAD_DIM] - ws[i][:C] for i in P]
        vp = [_pieces(v_new[i], n_scan) for i in P]
        k_tail = [k[i] * jnp.exp(g_last[i] - gcb[i]) for i in P]

        o = [ws[i][C:] + _mm(_pieces(attn[i], n_scan), vp[i]) for i in P]
        s_new = [s_old[i] * jnp.exp(g_last[i]) + _mm(_pieces(k_tail[i], n_scan), vp[i], TN)
                 for i in P]
        for i, (bi, h) in enumerate(probs):
            s_ref[bi, h] = s_new[i]
            on = (o[i] * lax.rsqrt(jnp.mean(o[i] * o[i], axis=-1, keepdims=True) + RMS_EPS)
                  * nw * head(gate_all[bi], h))
            o_ref[bi, pl.ds(c0, C), h * HEAD_DIM:(h + 1) * HEAD_DIM] = on.astype(o_ref.dtype)
        last = lambda xs: jnp.stack([x[C - 8:C] for x in xs])
        return last(xq), last(xk), last(xv)

    tails = lax.fori_loop(0, rows // C, body, (tail_ref[0], tail_ref[1], tail_ref[2]))
    for j in range(3):
        tail_ref[j] = tails[j]


GDN_BATCH_PER_STEP = 2
GDN_BATCH_SHORT = 4


def _gdn(pre, ba, conv8, s0, conv_w, alog_row, dtb_row, nw_row, name):
    bsz, seq, _ = pre.shape
    chunk = CHUNK if seq % CHUNK == 0 else seq
    rows = _pick(seq, GDN_ROWS_PER_STEP)
    nb = _pick(bsz, GDN_BATCH_PER_STEP if seq > chunk else GDN_BATCH_SHORT)

    def col(grp):
        return pl.BlockSpec((nb, rows, WIDTH), lambda b, t: (b, t, grp))

    def st(grp):
        return pl.BlockSpec((nb, SUBLANES, WIDTH), lambda b, t: (b, 0, grp))

    def cw(grp):
        return pl.BlockSpec((CONV_W, WIDTH), lambda b, t: (0, grp))

    row = pl.BlockSpec((1, LANES), lambda b, t: (0, 0))
    state = pl.BlockSpec((nb, N_HEADS, HEAD_DIM, HEAD_DIM), lambda b, t: (b, 0, 0, 0))
    return pl.pallas_call(
        functools.partial(_gdn_kernel, nb=nb, rows=rows, chunk=chunk),
        grid=(bsz // nb, seq // rows),
        in_specs=[col(0), col(1), col(2), col(3),
                  pl.BlockSpec((nb, rows, LANES), lambda b, t: (b, t, 0)),
                  st(0), st(1), st(2), cw(0), cw(1), cw(2), state, row, row, row],
        out_specs=[pl.BlockSpec((nb, rows, WIDTH), lambda b, t: (b, t, 0)), state],
        out_shape=[jax.ShapeDtypeStruct((bsz, seq, WIDTH), BF16),
                   jax.ShapeDtypeStruct((bsz, N_HEADS, HEAD_DIM, HEAD_DIM), F32)],
        scratch_shapes=[pltpu.VMEM((3, nb, SUBLANES, WIDTH), F32)],
        compiler_params=pltpu.CompilerParams(
            dimension_semantics=("parallel", "arbitrary"), vmem_limit_bytes=VMEM_LIMIT),
        name=name,
    )(pre, pre, pre, pre, ba, conv8, conv8, conv8, conv_w, conv_w, conv_w, s0,
      alog_row, dtb_row, nw_row)


SB_KEYS = 256
SB_DEAD = -104.0
SB_BOUND_SLACK = 1.05


def _suffix_ones(n):
    r = lax.broadcasted_iota(jnp.int32, (n, n), 0)
    c = lax.broadcasted_iota(jnp.int32, (n, n), 1)
    return (r >= c).astype(BF16)


def _sb_split(z, mask):
    l1m = -(jnp.maximum(z, 0.0) + jnp.log(1.0 + jnp.exp(-jnp.abs(z))))
    if mask is not None:
        l1m = jnp.where(mask, l1m, 0.0)
    hi = l1m.astype(BF16)
    return hi, (l1m - hi.astype(F32)).astype(BF16)


def _sb_sums(hi, lo, suffix):
    tk = suffix.shape[0]
    tiles = [slice(s * tk, (s + 1) * tk) for s in range(hi.shape[1] // tk)]
    return [_dot(hi[:, t], suffix) + _dot(lo[:, t], suffix) for t in tiles]


def _sb_weights(z, cs, mask, run):
    n = len(cs)
    reps = cs[0].shape[1] // LANES
    tot = [None] * n
    for s in reversed(range(n)):
        tot[s] = run if reps == 1 else jnp.concatenate([run] * reps, axis=1)
        run = run + jnp.broadcast_to(cs[s][:, 0:1], run.shape)
    off = [cs[s] + tot[s] for s in range(n)]
    w = jnp.exp(z + (off[0] if n == 1 else jnp.concatenate(off, axis=1)))
    if mask is not None:
        w = jnp.where(mask, w, 0.0)
    return w.astype(BF16), run


def _sb_prompt_kernel(q_ref, k_ref, v_ref, o_ref, kb_ref, vb_ref, kmax_ref, *, tq, hb, scale):
    qi = pl.program_id(2)
    heads = range(hb)
    ones = jnp.ones((HEAD_DIM, LANES), BF16)

    def head(x, h):
        return x[:, h * HEAD_DIM:(h + 1) * HEAD_DIM]

    @pl.when(qi == 0)
    def _():
        zeros = jnp.zeros((tq, HEAD_DIM), BF16)
        for h in heads:
            kb_ref[h, 0:tq, :] = zeros
            vb_ref[h, 0:tq, :] = zeros
            kmax_ref[h] = jnp.zeros(kmax_ref.shape[1:], F32)

    dst = pl.multiple_of((qi + 1) * tq, tq)
    for h in heads:
        first = pl.program_id(1) * hb + h
        kh = k_ref[pl.ds(first, tq, stride=N_HEADS), :]
        kb_ref[h, pl.ds(dst, tq), :] = kh.astype(BF16)
        vb_ref[h, pl.ds(dst, tq), :] = v_ref[pl.ds(first, tq, stride=N_HEADS), :].astype(BF16)
        ksq = _dot((kh * kh).astype(BF16), ones)
        kmax_ref[h] = jnp.maximum(kmax_ref[h], jnp.broadcast_to(
            jnp.max(ksq, axis=0, keepdims=True), kmax_ref.shape[1:]))

    q = q_ref[...]
    qb = [head(q, h).astype(BF16) for h in heads]
    zbound = [jnp.sqrt(_dot((head(q, h) * head(q, h)).astype(BF16), ones) * kmax_ref[h, 0:1, :])
              * (scale * SB_BOUND_SLACK) for h in heads]
    lr = _suffix_ones(SB_KEYS)

    def sweep(row0, nkeys, mask, acc, run):
        z = [_dot_nt(qb[h], kb_ref[h, pl.ds(row0, nkeys), :]) * scale for h in heads]
        pieces = [_sb_split(z[h], mask) for h in heads]
        ct = [_sb_sums(*pieces[h], lr) for h in heads]
        wr = [_sb_weights(z[h], ct[h], mask, run[h]) for h in heads]
        acc = [acc[h] + _dot(wr[h][0], vb_ref[h, pl.ds(row0, nkeys), :]) for h in heads]
        return acc, [wr[h][1] for h in heads]

    def alive(run):
        live = jnp.max(run[0] + zbound[0]) > SB_DEAD
        for h in heads[1:]:
            live = live | (jnp.max(run[h] + zbound[h]) > SB_DEAD)
        return live

    r = lax.broadcasted_iota(jnp.int32, (tq, 2 * tq), 0)
    c = lax.broadcasted_iota(jnp.int32, (tq, 2 * tq), 1)
    mask = (c < r + tq) & (c + qi * tq >= tq)
    acc, run = sweep(pl.multiple_of(qi * tq, tq), 2 * tq, mask,
                     [jnp.zeros((tq, HEAD_DIM), F32) for _ in heads],
                     [jnp.zeros((tq, LANES), F32) for _ in heads])

    def cond(carry):
        return (carry[0] >= 0) & carry[3]

    def body(carry):
        j, acc, run, _ = carry
        acc, run = sweep(pl.multiple_of((j + 1) * tq, tq), tq, None, acc, run)
        return j - 1, acc, run, alive(run)

    _, acc, _, _ = lax.while_loop(cond, body, (qi - 2, acc, run, alive(run)))
    o_ref[...] = jnp.concatenate(acc, axis=1).astype(o_ref.dtype)


SB_HEADS_PER_STEP = 2
SB_SAMPLE_KEYS = 2048


def _sb_prompt(q, k, v, name):
    bsz, seq, _ = q.shape
    tq = _pick(seq, 256)
    hb = SB_HEADS_PER_STEP
    wid = hb * HEAD_DIM
    kv = pl.BlockSpec((None, tq * N_HEADS, HEAD_DIM), lambda b, h, i: (b, i, 0))
    qo = pl.BlockSpec((None, tq, wid), lambda b, h, i: (b, i, h))
    return pl.pallas_call(
        functools.partial(_sb_prompt_kernel, tq=tq, hb=hb, scale=HEAD_DIM ** -0.5),
        grid=(bsz, N_HEADS // hb, seq // tq),
        in_specs=[qo, kv, kv],
        out_specs=qo,
        out_shape=jax.ShapeDtypeStruct((bsz, seq, WIDTH), BF16),
        scratch_shapes=[pltpu.VMEM((hb, seq + tq, HEAD_DIM), BF16),
                        pltpu.VMEM((hb, seq + tq, HEAD_DIM), BF16),
                        pltpu.VMEM((hb, SUBLANES, LANES), F32)],
        compiler_params=pltpu.CompilerParams(
            dimension_semantics=("parallel", "parallel", "arbitrary"),
            vmem_limit_bytes=VMEM_LIMIT),
        name=name,
    )(q, k, v)


def _sb_sample_kernel(q_ref, kn_ref, vn_ref, kc_ref, vc_ref, o_ref, qb_ref, acc_ref, run_ref,
                      *, seq, keys, scale):
    step = pl.program_id(1)
    heads = range(N_HEADS)
    rows = N_HEADS * seq
    def sweep(k_tiles, v_tiles, mask, lr):
        z = jnp.concatenate([_dot_nt(qb_ref[h * seq:(h + 1) * seq, :], k_tiles[h]) for h in heads],
                            axis=0) * scale
        wb, run = _sb_weights(z, _sb_sums(*_sb_split(z, mask), lr), mask, run_ref[...])
        run_ref[...] = run
        pv = [_dot(wb[h * seq:(h + 1) * seq, :], v_tiles[h]) for h in heads]
        acc_ref[...] += jnp.concatenate(pv, axis=0)

    @pl.when(step == 0)
    def _():
        for h in heads:
            qb_ref[h * seq:(h + 1) * seq, :] = q_ref[:, h * HEAD_DIM:(h + 1) * HEAD_DIM].astype(BF16)
        acc_ref[...] = jnp.zeros_like(acc_ref)
        run_ref[...] = jnp.zeros_like(run_ref)
        pad = jnp.zeros((LANES - seq, HEAD_DIM), BF16)
        kn = [jnp.concatenate([kn_ref[pl.ds(h, seq, stride=N_HEADS), :].astype(BF16), pad], axis=0)
              for h in heads]
        vn = [jnp.concatenate([vn_ref[pl.ds(h, seq, stride=N_HEADS), :].astype(BF16), pad], axis=0)
              for h in heads]
        r = lax.broadcasted_iota(jnp.int32, (rows, LANES), 0)
        c = lax.broadcasted_iota(jnp.int32, (rows, LANES), 1)
        sweep(kn, vn, c < (r & (seq - 1)), _suffix_ones(LANES))

    kt = [kc_ref[pl.ds(h, keys, stride=N_HEADS), :].astype(BF16) for h in heads]
    vt = [vc_ref[pl.ds(h, keys, stride=N_HEADS), :].astype(BF16) for h in heads]
    sweep(kt, vt, None, _suffix_ones(SB_KEYS))

    @pl.when(step == pl.num_programs(1) - 1)
    def _():
        for h in heads:
            o_ref[:, h * HEAD_DIM:(h + 1) * HEAD_DIM] = acc_ref[h * seq:(h + 1) * seq, :].astype(o_ref.dtype)


def _sb_sample(q, kn, vn, kc, vc, name):
    bsz, seq, _ = q.shape
    past = kc.shape[1]
    kc = kc.reshape(bsz, past * N_HEADS, HEAD_DIM)
    vc = vc.reshape(bsz, past * N_HEADS, HEAD_DIM)
    assert seq & (seq - 1) == 0 and seq <= LANES and past % SB_KEYS == 0
    keys = _pick(past, SB_SAMPLE_KEYS)
    nsteps = past // keys
    new = pl.BlockSpec((None, seq, WIDTH), lambda b, j: (b, 0, 0))
    newkv = pl.BlockSpec((None, seq * N_HEADS, HEAD_DIM), lambda b, j: (b, 0, 0))
    old = pl.BlockSpec((None, keys * N_HEADS, HEAD_DIM), lambda b, j: (b, nsteps - 1 - j, 0))
    rows = N_HEADS * seq
    return pl.pallas_call(
        functools.partial(_sb_sample_kernel, seq=seq, keys=keys, scale=HEAD_DIM ** -0.5),
        grid=(bsz, nsteps),
        in_specs=[new, newkv, newkv, old, old],
        out_specs=new,
        out_shape=jax.ShapeDtypeStruct((bsz, seq, WIDTH), BF16),
        scratch_shapes=[pltpu.VMEM((rows, HEAD_DIM), BF16), pltpu.VMEM((rows, HEAD_DIM), F32),
                        pltpu.VMEM((rows, LANES), F32)],
        compiler_params=pltpu.CompilerParams(
            dimension_semantics=("parallel", "arbitrary"), vmem_limit_bytes=VMEM_LIMIT),
        name=name,
    )(q, kn, vn, kc, vc)


def _layer_norm(r, g, b):
    mu = jnp.mean(r, axis=-1, keepdims=True)
    d = r - mu
    var = jnp.mean(d * d, axis=-1, keepdims=True)
    return d * lax.rsqrt(var + LN_EPS) * g + b


def _outproj_kernel(oa_ref, ob_ref, x_ref, wa_ref, wb_ref, g_ref, b_ref, y_ref, *, alpha, sub):
    for s in range(y_ref.shape[0] // sub):
        rs = slice(s * sub, (s + 1) * sub)
        acc = _dot(oa_ref[rs, :], wa_ref[...]) + _dot(ob_ref[rs, :], wb_ref[...])
        y_ref[rs, :] = _layer_norm(alpha * x_ref[rs, :] + acc, g_ref[...], b_ref[...])


def _outproj_ln(oa, ob, x, wa, wb, g, b, alpha, name):
    m, d = x.shape
    tm = _pick(m, 512)
    rows = lambda w: pl.BlockSpec((tm, w), lambda i: (i, 0))
    full = lambda a: pl.BlockSpec(a.shape, lambda i: (0, 0))
    return pl.pallas_call(
        functools.partial(_outproj_kernel, alpha=alpha, sub=_pick(tm, 256)),
        grid=(m // tm,),
        in_specs=[rows(WIDTH), rows(WIDTH), rows(d), full(wa), full(wb), full(g), full(b)],
        out_specs=rows(d),
        out_shape=jax.ShapeDtypeStruct((m, d), F32),
        compiler_params=pltpu.CompilerParams(
            dimension_semantics=("parallel",), vmem_limit_bytes=VMEM_LIMIT),
        name=name,
    )(oa, ob, x, wa, wb, g, b)


FFN_TM = 512
FFN_TF = 1024


def _ffn_kernel(x_ref, wu_ref, wd_ref, g_ref, b_ref, y_ref, xb_ref, *, alpha):
    j = pl.program_id(1)

    @pl.when(j == 0)
    def _():
        xb_ref[...] = x_ref[...].astype(BF16)
        y_ref[...] = jnp.zeros_like(y_ref)

    hid = jnp.maximum(_dot(xb_ref[...], wu_ref[...]), 0.0)
    y_ref[...] += _dot((hid * hid).astype(BF16), wd_ref[...])

    @pl.when(j == pl.num_programs(1) - 1)
    def _():
        y_ref[...] = _layer_norm(alpha * x_ref[...] + y_ref[...], g_ref[...], b_ref[...])


def _ffn_ln(x, wu, wd, g, b, alpha, name):
    m, d = x.shape
    dff = wu.shape[1]
    tm = _pick(m, FFN_TM)
    tf = _pick(dff, FFN_TF)
    return pl.pallas_call(
        functools.partial(_ffn_kernel, alpha=alpha),
        grid=(m // tm, dff // tf),
        in_specs=[pl.BlockSpec((tm, d), lambda i, j: (i, 0)),
                  pl.BlockSpec((d, tf), lambda i, j: (0, j)),
                  pl.BlockSpec((tf, d), lambda i, j: (j, 0)),
                  pl.BlockSpec((1, d), lambda i, j: (0, 0)),
                  pl.BlockSpec((1, d), lambda i, j: (0, 0))],
        out_specs=pl.BlockSpec((tm, d), lambda i, j: (i, 0)),
        out_shape=jax.ShapeDtypeStruct((m, d), F32),
        scratch_shapes=[pltpu.VMEM((tm, d), BF16)],
        compiler_params=pltpu.CompilerParams(
            dimension_semantics=("parallel", "arbitrary"), vmem_limit_bytes=VMEM_LIMIT),
        name=name,
    )(x, wu, wd, g, b)


def _layer(x, conv_buf, s0, k_past, v_past, wts, alpha, tag):
    (w_main, w_gate, conv_w, alog_row, dtb_row, nw_row,
     wo_a, wo_b, ln1_g, ln1_b, w_up, w_down, ln2_g, ln2_b) = wts
    bsz, seq, d = x.shape
    m = bsz * seq
    x2 = x.reshape(m, d)

    pre, ba, q_b, k_out, v_out = _proj(x2, w_main, w_gate, "proj_" + tag)
    pre = pre.reshape(bsz, seq, 4 * WIDTH)
    ba = ba.reshape(bsz, seq, LANES)
    q_b = q_b.reshape(bsz, seq, WIDTH)
    k_b = k_out.reshape(bsz, seq * N_HEADS, HEAD_DIM)
    v_b = v_out.reshape(bsz, seq * N_HEADS, HEAD_DIM)

    conv8 = jnp.pad(conv_buf, ((0, 0), (SUBLANES - (CONV_W - 1), 0), (0, 0)))
    o_a, s_new = _gdn(pre, ba, conv8, s0, conv_w, alog_row, dtb_row, nw_row, "gdn_" + tag)
    keep = CONV_W - 1
    conv_new = pre[:, seq - keep:, :3 * WIDTH] if seq >= keep else jnp.concatenate(
        [conv_buf, pre[:, :, :3 * WIDTH]], axis=1)[:, -keep:, :]

    if k_past is None:
        o_b = _sb_prompt(q_b, k_b, v_b, "sb_" + tag)
    else:
        o_b = _sb_sample(q_b, k_b, v_b, k_past, v_past, "sb_" + tag)

    x1 = _outproj_ln(o_a.reshape(m, WIDTH), o_b.reshape(m, WIDTH), x2, wo_a, wo_b,
                     ln1_g, ln1_b, alpha, "outproj_" + tag)
    y = _ffn_ln(x1, w_up, w_down, ln2_g, ln2_b, alpha, "ffn_" + tag)
    return (y.reshape(bsz, seq, d), conv_new, s_new,
            k_out.reshape(bsz, seq, N_HEADS, HEAD_DIM), v_out.reshape(bsz, seq, N_HEADS, HEAD_DIM))


def kernel(x_prompt, x_sample, state_gdn_conv, state_gdn_S, cache_sb_k, cache_sb_v, w_in, conv_w, a_log, dt_bias, gdn_norm_w, w_out, ln1_g, ln1_b, w_up, w_down, ln2_g, ln2_b):
    depth = w_in.shape[0]
    alpha = (2 * depth) ** 0.25
    off_b = 4 * WIDTH
    off_sb = off_b + 2 * N_HEADS
    yp, ys = x_prompt, x_sample
    outs = [[] for _ in range(8)]
    for l in range(depth):
        wl = w_in[l]
        w_a, w_sb, w_gate = _w_in_cast(wl, off_b, off_sb, "w_in_cast%d" % l)
        w_main = (w_a, w_sb)
        pad_row = lambda v: jnp.pad(v.astype(F32), (N_HEADS, LANES - 2 * N_HEADS)).reshape(1, LANES)
        wts = (w_main, w_gate,
               conv_w[l], pad_row(a_log[l]), pad_row(dt_bias[l]),
               gdn_norm_w[l].astype(F32).reshape(1, HEAD_DIM),
               w_out[l, :WIDTH].astype(BF16), w_out[l, WIDTH:].astype(BF16),
               ln1_g[l].reshape(1, -1), ln1_b[l].reshape(1, -1),
               w_up[l].astype(BF16), w_down[l].astype(BF16),
               ln2_g[l].reshape(1, -1), ln2_b[l].reshape(1, -1))
        bp = yp.shape[0]
        zero_conv = jnp.zeros((bp, CONV_W - 1, 3 * WIDTH), F32)
        zero_s = jnp.zeros((bp, N_HEADS, HEAD_DIM, HEAD_DIM), F32)
        yp, c1, s1, k1, v1 = _layer(yp, zero_conv, zero_s, None, None, wts, alpha, "p%d" % l)
        ys, c2, s2, k2, v2 = _layer(ys, state_gdn_conv[l], state_gdn_S[l],
                                    cache_sb_k[l], cache_sb_v[l], wts, alpha, "s%d" % l)
        for lst, val in zip(outs, (c1, s1, k1, v1, c2, s2, k2, v2)):
            lst.append(val)
    return (yp, ys) + tuple(jnp.stack(o) for o in outs)
```

```python
import functools

import jax
import jax.numpy as jnp
from jax import lax
from jax.experimental import pallas as pl
from jax.experimental.pallas import tpu as pltpu

HEAD_DIM = 128
N_HEADS = 8
WIDTH = N_HEADS * HEAD_DIM
CONV_W = 4
CHUNK = 128
LN_EPS = 1e-5
RMS_EPS = 1e-6
L2_EPS = 1e-6
LANES = 128
SUBLANES = 8
VMEM_LIMIT = 56 * 1024 * 1024

F32 = jnp.float32
BF16 = jnp.bfloat16


def _pick(n, pref):
    if n <= pref:
        return n
    t = pref
    while n % t:
        t //= 2
    return t


def _dot(a, b):
    return jnp.dot(a, b, preferred_element_type=F32)


def _dot_nt(a, b):
    return lax.dot_general(a, b, (((1,), (1,)), ((), ())), preferred_element_type=F32)


def _sigmoid(x):
    return 1.0 / (1.0 + jnp.exp(-x))


def _softplus(x):
    return jnp.maximum(x, 0.0) + jnp.log1p(jnp.exp(-jnp.abs(x)))


PROJ_TM = 1024
PROJ_TN = 512
N_PRE_TILES = 4 * WIDTH // PROJ_TN
N_SB_TILES = WIDTH // PROJ_TN
HEADS_PER_TILE = PROJ_TN // HEAD_DIM


def _proj_kernel(x_ref, wa_ref, wsb_ref, wg_ref, pre_ref, ba_ref, q_ref, kout_ref, vout_ref,
                 xb_ref, *, tm):
    j = pl.program_id(1)

    @pl.when(j == 0)
    def _():
        xb_ref[...] = x_ref[...].astype(BF16)
        ba_ref[...] = _dot(xb_ref[...], wg_ref[...])

    @pl.when(j < N_PRE_TILES)
    def _():
        pre_ref[...] = _dot(xb_ref[...], wa_ref[...])

    @pl.when((j >= N_PRE_TILES) & (j < N_PRE_TILES + N_SB_TILES))
    def _():
        q_ref[...] = _dot(xb_ref[...], wsb_ref[...]).astype(BF16)

    def keep(out_ref, first_head):
        res = _dot(xb_ref[...], wsb_ref[...])
        for h in range(HEADS_PER_TILE):
            out_ref[pl.ds(first_head + h, tm, stride=N_HEADS), :] = res[:, h * HEAD_DIM:(h + 1) * HEAD_DIM]

    for t in range(N_SB_TILES):
        pl.when(j == N_PRE_TILES + N_SB_TILES + t)(
            functools.partial(keep, kout_ref, t * HEADS_PER_TILE))
        pl.when(j == N_PRE_TILES + 2 * N_SB_TILES + t)(
            functools.partial(keep, vout_ref, t * HEADS_PER_TILE))


def _proj(x, w_main, w_gate, name):
    w_in, w_sb = w_main
    m, k = x.shape
    tm = _pick(m, PROJ_TM)
    nt = N_PRE_TILES + 3 * N_SB_TILES
    row = lambda w: pl.BlockSpec((tm, w), lambda i, j: (i, 0))
    return pl.pallas_call(
        functools.partial(_proj_kernel, tm=tm),
        grid=(m // tm, nt),
        in_specs=[row(k),
                  pl.BlockSpec((k, PROJ_TN), lambda i, j: (0, jnp.minimum(j, N_PRE_TILES - 1))),
                  pl.BlockSpec((k, PROJ_TN), lambda i, j: (0, jnp.maximum(j - N_PRE_TILES, 0))),
                  pl.BlockSpec((k, LANES), lambda i, j: (0, 0))],
        out_specs=[pl.BlockSpec((tm, PROJ_TN), lambda i, j: (i, jnp.minimum(j, N_PRE_TILES - 1))),
                   row(LANES),
                   pl.BlockSpec((tm, PROJ_TN),
                                lambda i, j: (i, jnp.clip(j - N_PRE_TILES, 0, N_SB_TILES - 1))),
                   pl.BlockSpec((tm * N_HEADS, HEAD_DIM), lambda i, j: (i, 0)),
                   pl.BlockSpec((tm * N_HEADS, HEAD_DIM), lambda i, j: (i, 0))],
        out_shape=[jax.ShapeDtypeStruct((m, 4 * WIDTH), F32),
                   jax.ShapeDtypeStruct((m, LANES), F32),
                   jax.ShapeDtypeStruct((m, WIDTH), BF16),
                   jax.ShapeDtypeStruct((m * N_HEADS, HEAD_DIM), F32),
                   jax.ShapeDtypeStruct((m * N_HEADS, HEAD_DIM), F32)],
        scratch_shapes=[pltpu.VMEM((tm, k), BF16)],
        compiler_params=pltpu.CompilerParams(
            dimension_semantics=("parallel", "arbitrary"), vmem_limit_bytes=VMEM_LIMIT),
        name=name,
    )(x, w_in, w_sb, w_gate)


NN = (((1,), (0,)), ((), ()))
NT = (((1,), (1,)), ((), ()))
TN = (((0,), (0,)), ((), ()))

GDN_ROWS_PER_STEP = 512
GDN_PIECES = dict(qk=1, inv=1, uw=1, scan=1)


def _pieces(x, n):
    out = []
    r = x
    for i in range(n):
        p = r.astype(BF16)
        out.append(p)
        if i + 1 < n:
            r = r - p.astype(F32)
    return out


def _mm(a, b, dims=NN):
    order = max(len(a), len(b))
    acc = None
    for i, ai in enumerate(a):
        for j, bj in enumerate(b):
            if i + j < order:
                t = lax.dot_general(ai, bj, dims, preferred_element_type=F32)
                acc = t if acc is None else acc + t
    return acc


def _gdn_kernel(q_ref, k_ref, v_ref, z_ref, ba_ref, cq_ref, ck_ref, cv_ref,
                wq_ref, wk_ref, wv_ref, s0_ref, alog_ref, dtb_ref, nw_ref,
                o_ref, s_ref, tail_ref, *, nb, rows, chunk):
    C = chunk
    probs = [(bi, h) for bi in range(nb) for h in range(N_HEADS)]
    P = range(len(probs))
    row = lax.broadcasted_iota(jnp.int32, (C, C), 0)
    col = lax.broadcasted_iota(jnp.int32, (C, C), 1)
    incl = row >= col
    strict = row > col
    l_incl = [incl.astype(BF16)]
    eye = (row == col).astype(F32)
    lane = lax.broadcasted_iota(jnp.int32, (C, LANES), 1)
    sel_r = lax.broadcasted_iota(jnp.int32, (LANES, LANES), 0)
    sel_col = [[(sel_r == h + N_HEADS).astype(BF16)] for h in range(N_HEADS)]
    sel_row = [[(lane == h + N_HEADS).astype(BF16)] for h in range(N_HEADS)]
    level_masks = []
    b = 1
    while b < C:
        s = b.bit_length() - 1
        same = (row >> (s + 1)) == (col >> (s + 1))
        level_masks.append(same & (((row >> s) & 1) == 1) & (((col >> s) & 1) == 0))
        b *= 2
    n_qk, n_inv, n_uw, n_scan = (GDN_PIECES[c] for c in ("qk", "inv", "uw", "scan"))

    neg_a = -jnp.exp(alog_ref[...])
    dtb = dtb_ref[...]
    nw = nw_ref[...]
    wq = wq_ref[...]
    wk = wk_ref[...]
    wv = wv_ref[...]

    @pl.when(pl.program_id(1) == 0)
    def _():
        s_ref[...] = s0_ref[...]
        tail_ref[0] = cq_ref[...]
        tail_ref[1] = ck_ref[...]
        tail_ref[2] = cv_ref[...]

    def conv(e, w):
        first = SUBLANES - (CONV_W - 1)
        acc = e[first:first + C] * w[0:1]
        for i in range(1, CONV_W):
            acc = acc + e[first + i:first + i + C] * w[i:i + 1]
        return acc * _sigmoid(acc)

    def head(x, i):
        return x[:, i * HEAD_DIM:(i + 1) * HEAD_DIM]

    def body(n, carry):
        tq, tk, tv = carry
        c0 = pl.multiple_of(n * C, C)
        xq = [q_ref[bi, pl.ds(c0, C), :] for bi in range(nb)]
        xk = [k_ref[bi, pl.ds(c0, C), :] for bi in range(nb)]
        xv = [v_ref[bi, pl.ds(c0, C), :] for bi in range(nb)]
        q_all = [conv(jnp.concatenate([tq[bi], xq[bi]], axis=0), wq) for bi in range(nb)]
        k_all = [conv(jnp.concatenate([tk[bi], xk[bi]], axis=0), wk) for bi in range(nb)]
        v_all = [conv(jnp.concatenate([tv[bi], xv[bi]], axis=0), wv) for bi in range(nb)]
        gate_all = []
        beta_all = []
        g_all = []
        for bi in range(nb):
            zc = z_ref[bi, pl.ds(c0, C), :]
            gate_all.append(zc * _sigmoid(zc))
            ba = ba_ref[bi, pl.ds(c0, C), :]
            beta_all.append(_sigmoid(ba))
            g_all.append(neg_a * _softplus(ba + dtb))

        q = [head(q_all[bi], h) for bi, h in probs]
        k = [head(k_all[bi], h) for bi, h in probs]
        q = [x * lax.rsqrt(jnp.sum(x * x, axis=-1, keepdims=True) + L2_EPS) * (HEAD_DIM ** -0.5)
             for x in q]
        k = [x * lax.rsqrt(jnp.sum(x * x, axis=-1, keepdims=True) + L2_EPS) for x in k]
        beta = [jnp.sum(jnp.where(lane == h, beta_all[bi], 0.0), axis=-1, keepdims=True)
                for bi, h in probs]
        k_beta = [k[i] * beta[i] for i in P]
        kp = [_pieces(k[i], n_qk) for i in P]

        gc_all = [_pieces(_mm(l_incl, _pieces(g_all[bi], 3)), 3) for bi in range(nb)]
        gcb = [_mm(gc_all[bi], sel_col[h]) for bi, h in probs]
        gc_row = [_mm(sel_row[h], gc_all[bi], NT) for bi, h in probs]
        kk = [_mm(_pieces(k_beta[i], n_qk), kp[i], NT) for i in P]
        qk = [_mm(_pieces(q[i], n_qk), kp[i], NT) for i in P]

        decay = [jnp.where(incl, jnp.exp(gcb[i][:, :C] - gc_row[i]), 0.0) for i in P]
        egc = [jnp.exp(gcb[i]) for i in P]
        g_last = [gcb[i][C - 1:C, :] for i in P]
        m = [jnp.where(strict, kk[i] * decay[i], 0.0) for i in P]
        attn = [qk[i] * decay[i] for i in P]

        t_inv = [eye - jnp.where(level_masks[0], m[i], 0.0) for i in P]
        for mask in level_masks[1:]:
            tp = [_pieces(t_inv[i], n_inv) for i in P]
            x = [_mm(tp[i], _pieces(jnp.where(mask, m[i], 0.0), n_inv)) for i in P]
            y = [_mm(_pieces(x[i], n_inv), tp[i]) for i in P]
            t_inv = [t_inv[i] - y[i] for i in P]

        rhs = [jnp.concatenate([head(v_all[probs[i][0]], probs[i][1]) * beta[i], k_beta[i] * egc[i]],
                               axis=1) for i in P]
        uw = [_mm(_pieces(t_inv[i], n_uw), _pieces(rhs[i], n_uw)) for i in P]

        s_old = [s_ref[bi, h] for bi, h in probs]
        lhs = [jnp.concatenate([uw[i][:, HEAD_DIM:], q[i] * egc[i]], axis=0) for i in P]
        ws = [_mm(_pieces(lhs[i], n_scan), _pieces(s_old[i], n_scan)) for i in P]
        v_new = [uw[i][:, :HEAD_DIM] - ws[i][:C] for i in P]
        vp = [_pieces(v_new[i], n_scan) for i in P]
        k_tail = [k[i] * jnp.exp(g_last[i] - gcb[i]) for i in P]

        o = [ws[i][C:] + _mm(_pieces(attn[i], n_scan), vp[i]) for i in P]
        s_new = [s_old[i] * jnp.exp(g_last[i]) + _mm(_pieces(k_tail[i], n_scan), vp[i], TN)
                 for i in P]
        for i, (bi, h) in enumerate(probs):
            s_ref[bi, h] = s_new[i]
            on = (o[i] * lax.rsqrt(jnp.mean(o[i] * o[i], axis=-1, keepdims=True) + RMS_EPS)
                  * nw * head(gate_all[bi], h))
            o_ref[bi, pl.ds(c0, C), h * HEAD_DIM:(h + 1) * HEAD_DIM] = on.astype(o_ref.dtype)
        last = lambda xs: jnp.stack([x[C - 8:C] for x in xs])
        return last(xq), last(xk), last(xv)

    tails = lax.fori_loop(0, rows // C, body, (tail_ref[0], tail_ref[1], tail_ref[2]))
    for j in range(3):
        tail_ref[j] = tails[j]


GDN_BATCH_PER_STEP = 2
GDN_BATCH_SHORT = 4


def _gdn(pre, ba, conv8, s0, conv_w, alog_row, dtb_row, nw_row, name):
    bsz, seq, _ = pre.shape
    chunk = CHUNK if seq % CHUNK == 0 else seq
    rows = _pick(seq, GDN_ROWS_PER_STEP)
    nb = _pick(bsz, GDN_BATCH_PER_STEP if seq > chunk else GDN_BATCH_SHORT)

    def col(grp):
        return pl.BlockSpec((nb, rows, WIDTH), lambda b, t: (b, t, grp))

    def st(grp):
        return pl.BlockSpec((nb, SUBLANES, WIDTH), lambda b, t: (b, 0, grp))

    def cw(grp):
        return pl.BlockSpec((CONV_W, WIDTH), lambda b, t: (0, grp))

    row = pl.BlockSpec((1, LANES), lambda b, t: (0, 0))
    state = pl.BlockSpec((nb, N_HEADS, HEAD_DIM, HEAD_DIM), lambda b, t: (b, 0, 0, 0))
    return pl.pallas_call(
        functools.partial(_gdn_kernel, nb=nb, rows=rows, chunk=chunk),
        grid=(bsz // nb, seq // rows),
        in_specs=[col(0), col(1), col(2), col(3),
                  pl.BlockSpec((nb, rows, LANES), lambda b, t: (b, t, 0)),
                  st(0), st(1), st(2), cw(0), cw(1), cw(2), state, row, row, row],
        out_specs=[pl.BlockSpec((nb, rows, WIDTH), lambda b, t: (b, t, 0)), state],
        out_shape=[jax.ShapeDtypeStruct((bsz, seq, WIDTH), BF16),
                   jax.ShapeDtypeStruct((bsz, N_HEADS, HEAD_DIM, HEAD_DIM), F32)],
        scratch_shapes=[pltpu.VMEM((3, nb, SUBLANES, WIDTH), F32)],
        compiler_params=pltpu.CompilerParams(
            dimension_semantics=("parallel", "arbitrary"), vmem_limit_bytes=VMEM_LIMIT),
        name=name,
    )(pre, pre, pre, pre, ba, conv8, conv8, conv8, conv_w, conv_w, conv_w, s0,
      alog_row, dtb_row, nw_row)


SB_KEYS = 256
SB_DEAD = -104.0
SB_BOUND_SLACK = 1.05


def _suffix_ones(n):
    r = lax.broadcasted_iota(jnp.int32, (n, n), 0)
    c = lax.broadcasted_iota(jnp.int32, (n, n), 1)
    return (r >= c).astype(BF16)


def _sb_split(z, mask):
    l1m = -(jnp.maximum(z, 0.0) + jnp.log(1.0 + jnp.exp(-jnp.abs(z))))
    if mask is not None:
        l1m = jnp.where(mask, l1m, 0.0)
    hi = l1m.astype(BF16)
    return hi, (l1m - hi.astype(F32)).astype(BF16)


def _sb_sums(hi, lo, suffix):
    tk = suffix.shape[0]
    tiles = [slice(s * tk, (s + 1) * tk) for s in range(hi.shape[1] // tk)]
    return [_dot(hi[:, t], suffix) + _dot(lo[:, t], suffix) for t in tiles]


def _sb_weights(z, cs, mask, run):
    n = len(cs)
    reps = cs[0].shape[1] // LANES
    tot = [None] * n
    for s in reversed(range(n)):
        tot[s] = run if reps == 1 else jnp.concatenate([run] * reps, axis=1)
        run = run + jnp.broadcast_to(cs[s][:, 0:1], run.shape)
    off = [cs[s] + tot[s] for s in range(n)]
    w = jnp.exp(z + (off[0] if n == 1 else jnp.concatenate(off, axis=1)))
    if mask is not None:
        w = jnp.where(mask, w, 0.0)
    return w.astype(BF16), run


def _sb_prompt_kernel(q_ref, k_ref, v_ref, o_ref, kb_ref, vb_ref, kmax_ref, *, tq, hb, scale):
    qi = pl.program_id(2)
    heads = range(hb)
    ones = jnp.ones((HEAD_DIM, LANES), BF16)

    def head(x, h):
        return x[:, h * HEAD_DIM:(h + 1) * HEAD_DIM]

    @pl.when(qi == 0)
    def _():
        zeros = jnp.zeros((tq, HEAD_DIM), BF16)
        for h in heads:
            kb_ref[h, 0:tq, :] = zeros
            vb_ref[h, 0:tq, :] = zeros
            kmax_ref[h] = jnp.zeros(kmax_ref.shape[1:], F32)

    dst = pl.multiple_of((qi + 1) * tq, tq)
    for h in heads:
        first = pl.program_id(1) * hb + h
        kh = k_ref[pl.ds(first, tq, stride=N_HEADS), :]
        kb_ref[h, pl.ds(dst, tq), :] = kh.astype(BF16)
        vb_ref[h, pl.ds(dst, tq), :] = v_ref[pl.ds(first, tq, stride=N_HEADS), :].astype(BF16)
        ksq = _dot((kh * kh).astype(BF16), ones)
        kmax_ref[h] = jnp.maximum(kmax_ref[h], jnp.broadcast_to(
            jnp.max(ksq, axis=0, keepdims=True), kmax_ref.shape[1:]))

    q = q_ref[...]
    qb = [head(q, h).astype(BF16) for h in heads]
    zbound = [jnp.sqrt(_dot((head(q, h) * head(q, h)).astype(BF16), ones) * kmax_ref[h, 0:1, :])
              * (scale * SB_BOUND_SLACK) for h in heads]
    lr = _suffix_ones(SB_KEYS)

    def sweep(row0, nkeys, mask, acc, run):
        z = [_dot_nt(qb[h], kb_ref[h, pl.ds(row0, nkeys), :]) * scale for h in heads]
        pieces = [_sb_split(z[h], mask) for h in heads]
        ct = [_sb_sums(*pieces[h], lr) for h in heads]
        wr = [_sb_weights(z[h], ct[h], mask, run[h]) for h in heads]
        acc = [acc[h] + _dot(wr[h][0], vb_ref[h, pl.ds(row0, nkeys), :]) for h in heads]
        return acc, [wr[h][1] for h in heads]

    def alive(run):
        live = jnp.max(run[0] + zbound[0]) > SB_DEAD
        for h in heads[1:]:
            live = live | (jnp.max(run[h] + zbound[h]) > SB_DEAD)
        return live

    r = lax.broadcasted_iota(jnp.int32, (tq, 2 * tq), 0)
    c = lax.broadcasted_iota(jnp.int32, (tq, 2 * tq), 1)
    mask = (c < r + tq) & (c + qi * tq >= tq)
    acc, run = sweep(pl.multiple_of(qi * tq, tq), 2 * tq, mask,
                     [jnp.zeros((tq, HEAD_DIM), F32) for _ in heads],
                     [jnp.zeros((tq, LANES), F32) for _ in heads])

    def cond(carry):
        return (carry[0] >= 0) & carry[3]

    def body(carry):
        j, acc, run, _ = carry
        acc, run = sweep(pl.multiple_of((j + 1) * tq, tq), tq, None, acc, run)
        return j - 1, acc, run, alive(run)

    _, acc, _, _ = lax.while_loop(cond, body, (qi - 2, acc, run, alive(run)))
    o_ref[...] = jnp.concatenate(acc, axis=1).astype(o_ref.dtype)


SB_HEADS_PER_STEP = 2
SB_SAMPLE_KEYS = 2048


def _sb_prompt(q, k, v, name):
    bsz, seq, _ = q.shape
    tq = _pick(seq, 256)
    hb = SB_HEADS_PER_STEP
    wid = hb * HEAD_DIM
    kv = pl.BlockSpec((None, tq * N_HEADS, HEAD_DIM), lambda b, h, i: (b, i, 0))
    qo = pl.BlockSpec((None, tq, wid), lambda b, h, i: (b, i, h))
    return pl.pallas_call(
        functools.partial(_sb_prompt_kernel, tq=tq, hb=hb, scale=HEAD_DIM ** -0.5),
        grid=(bsz, N_HEADS // hb, seq // tq),
        in_specs=[qo, kv, kv],
        out_specs=qo,
        out_shape=jax.ShapeDtypeStruct((bsz, seq, WIDTH), BF16),
        scratch_shapes=[pltpu.VMEM((hb, seq + tq, HEAD_DIM), BF16),
                        pltpu.VMEM((hb, seq + tq, HEAD_DIM), BF16),
                        pltpu.VMEM((hb, SUBLANES, LANES), F32)],
        compiler_params=pltpu.CompilerParams(
            dimension_semantics=("parallel", "parallel", "arbitrary"),
            vmem_limit_bytes=VMEM_LIMIT),
        name=name,
    )(q, k, v)


def _sb_sample_kernel(q_ref, kn_ref, vn_ref, kc_ref, vc_ref, o_ref, qb_ref, acc_ref, run_ref,
                      *, seq, keys, scale):
    step = pl.program_id(1)
    heads = range(N_HEADS)
    rows = N_HEADS * seq
    def sweep(k_tiles, v_tiles, mask, lr):
        z = jnp.concatenate([_dot_nt(qb_ref[h * seq:(h + 1) * seq, :], k_tiles[h]) for h in heads],
                            axis=0) * scale
        wb, run = _sb_weights(z, _sb_sums(*_sb_split(z, mask), lr), mask, run_ref[...])
        run_ref[...] = run
        pv = [_dot(wb[h * seq:(h + 1) * seq, :], v_tiles[h]) for h in heads]
        acc_ref[...] += jnp.concatenate(pv, axis=0)

    @pl.when(step == 0)
    def _():
        for h in heads:
            qb_ref[h * seq:(h + 1) * seq, :] = q_ref[:, h * HEAD_DIM:(h + 1) * HEAD_DIM].astype(BF16)
        acc_ref[...] = jnp.zeros_like(acc_ref)
        run_ref[...] = jnp.zeros_like(run_ref)
        pad = jnp.zeros((LANES - seq, HEAD_DIM), BF16)
        kn = [jnp.concatenate([kn_ref[pl.ds(h, seq, stride=N_HEADS), :].astype(BF16), pad], axis=0)
              for h in heads]
        vn = [jnp.concatenate([vn_ref[pl.ds(h, seq, stride=N_HEADS), :].astype(BF16), pad], axis=0)
              for h in heads]
        r = lax.broadcasted_iota(jnp.int32, (rows, LANES), 0)
        c = lax.broadcasted_iota(jnp.int32, (rows, LANES), 1)
        sweep(kn, vn, c < (r & (seq - 1)), _suffix_ones(LANES))

    kt = [kc_ref[pl.ds(h, keys, stride=N_HEADS), :].astype(BF16) for h in heads]
    vt = [vc_ref[pl.ds(h, keys, stride=N_HEADS), :].astype(BF16) for h in heads]
    sweep(kt, vt, None, _suffix_ones(SB_KEYS))

    @pl.when(step == pl.num_programs(1) - 1)
    def _():
        for h in heads:
            o_ref[:, h * HEAD_DIM:(h + 1) * HEAD_DIM] = acc_ref[h * seq:(h + 1) * seq, :].astype(o_ref.dtype)


def _sb_sample(q, kn, vn, kc, vc, name):
    bsz, seq, _ = q.shape
    past = kc.shape[1]
    kc = kc.reshape(bsz, past * N_HEADS, HEAD_DIM)
    vc = vc.reshape(bsz, past * N_HEADS, HEAD_DIM)
    assert seq & (seq - 1) == 0 and seq <= LANES and past % SB_KEYS == 0
    keys = _pick(past, SB_SAMPLE_KEYS)
    nsteps = past // keys
    new = pl.BlockSpec((None, seq, WIDTH), lambda b, j: (b, 0, 0))
    newkv = pl.BlockSpec((None, seq * N_HEADS, HEAD_DIM), lambda b, j: (b, 0, 0))
    old = pl.BlockSpec((None, keys * N_HEADS, HEAD_DIM), lambda b, j: (b, nsteps - 1 - j, 0))
    rows = N_HEADS * seq
    return pl.pallas_call(
        functools.partial(_sb_sample_kernel, seq=seq, keys=keys, scale=HEAD_DIM ** -0.5),
        grid=(bsz, nsteps),
        in_specs=[new, newkv, newkv, old, old],
        out_specs=new,
        out_shape=jax.ShapeDtypeStruct((bsz, seq, WIDTH), BF16),
        scratch_shapes=[pltpu.VMEM((rows, HEAD_DIM), BF16), pltpu.VMEM((rows, HEAD_DIM), F32),
                        pltpu.VMEM((rows, LANES), F32)],
        compiler_params=pltpu.CompilerParams(
            dimension_semantics=("parallel", "arbitrary"), vmem_limit_bytes=VMEM_LIMIT),
        name=name,
    )(q, kn, vn, kc, vc)


def _layer_norm(r, g, b):
    mu = jnp.mean(r, axis=-1, keepdims=True)
    d = r - mu
    var = jnp.mean(d * d, axis=-1, keepdims=True)
    return d * lax.rsqrt(var + LN_EPS) * g + b


def _outproj_kernel(oa_ref, ob_ref, x_ref, wa_ref, wb_ref, g_ref, b_ref, y_ref, *, alpha, sub):
    for s in range(y_ref.shape[0] // sub):
        rs = slice(s * sub, (s + 1) * sub)
        acc = _dot(oa_ref[rs, :], wa_ref[...]) + _dot(ob_ref[rs, :], wb_ref[...])
        y_ref[rs, :] = _layer_norm(alpha * x_ref[rs, :] + acc, g_ref[...], b_ref[...])


def _outproj_ln(oa, ob, x, wa, wb, g, b, alpha, name):
    m, d = x.shape
    tm = _pick(m, 512)
    rows = lambda w: pl.BlockSpec((tm, w), lambda i: (i, 0))
    full = lambda a: pl.BlockSpec(a.shape, lambda i: (0, 0))
    return pl.pallas_call(
        functools.partial(_outproj_kernel, alpha=alpha, sub=_pick(tm, 256)),
        grid=(m // tm,),
        in_specs=[rows(WIDTH), rows(WIDTH), rows(d), full(wa), full(wb), full(g), full(b)],
        out_specs=rows(d),
        out_shape=jax.ShapeDtypeStruct((m, d), F32),
        compiler_params=pltpu.CompilerParams(
            dimension_semantics=("parallel",), vmem_limit_bytes=VMEM_LIMIT),
        name=name,
    )(oa, ob, x, wa, wb, g, b)


FFN_TM = 512
FFN_TF = 1024


def _ffn_kernel(x_ref, wu_ref, wd_ref, g_ref, b_ref, y_ref, xb_ref, *, alpha):
    j = pl.program_id(1)

    @pl.when(j == 0)
    def _():
        xb_ref[...] = x_ref[...].astype(BF16)
        y_ref[...] = jnp.zeros_like(y_ref)

    hid = jnp.maximum(_dot(xb_ref[...], wu_ref[...]), 0.0)
    y_ref[...] += _dot((hid * hid).astype(BF16), wd_ref[...])

    @pl.when(j == pl.num_programs(1) - 1)
    def _():
        y_ref[...] = _layer_norm(alpha * x_ref[...] + y_ref[...], g_ref[...], b_ref[...])


def _ffn_ln(x, wu, wd, g, b, alpha, name):
    m, d = x.shape
    dff = wu.shape[1]
    tm = _pick(m, FFN_TM)
    tf = _pick(dff, FFN_TF)
    return pl.pallas_call(
        functools.partial(_ffn_kernel, alpha=alpha),
        grid=(m // tm, dff // tf),
        in_specs=[pl.BlockSpec((tm, d), lambda i, j: (i, 0)),
                  pl.BlockSpec((d, tf), lambda i, j: (0, j)),
                  pl.BlockSpec((tf, d), lambda i, j: (j, 0)),
                  pl.BlockSpec((1, d), lambda i, j: (0, 0)),
                  pl.BlockSpec((1, d), lambda i, j: (0, 0))],
        out_specs=pl.BlockSpec((tm, d), lambda i, j: (i, 0)),
        out_shape=jax.ShapeDtypeStruct((m, d), F32),
        scratch_shapes=[pltpu.VMEM((tm, d), BF16)],
        compiler_params=pltpu.CompilerParams(
            dimension_semantics=("parallel", "arbitrary"), vmem_limit_bytes=VMEM_LIMIT),
        name=name,
    )(x, wu, wd, g, b)


def _layer(x, conv_buf, s0, k_past, v_past, wts, alpha, tag):
    (w_main, w_gate, conv_w, alog_row, dtb_row, nw_row,
     wo_a, wo_b, ln1_g, ln1_b, w_up, w_down, ln2_g, ln2_b) = wts
    bsz, seq, d = x.shape
    m = bsz * seq
    x2 = x.reshape(m, d)

    pre, ba, q_b, k_out, v_out = _proj(x2, w_main, w_gate, "proj_" + tag)
    pre = pre.reshape(bsz, seq, 4 * WIDTH)
    ba = ba.reshape(bsz, seq, LANES)
    q_b = q_b.reshape(bsz, seq, WIDTH)
    k_b = k_out.reshape(bsz, seq * N_HEADS, HEAD_DIM)
    v_b = v_out.reshape(bsz, seq * N_HEADS, HEAD_DIM)

    conv8 = jnp.pad(conv_buf, ((0, 0), (SUBLANES - (CONV_W - 1), 0), (0, 0)))
    o_a, s_new = _gdn(pre, ba, conv8, s0, conv_w, alog_row, dtb_row, nw_row, "gdn_" + tag)
    keep = CONV_W - 1
    conv_new = pre[:, seq - keep:, :3 * WIDTH] if seq >= keep else jnp.concatenate(
        [conv_buf, pre[:, :, :3 * WIDTH]], axis=1)[:, -keep:, :]

    if k_past is None:
        o_b = _sb_prompt(q_b, k_b, v_b, "sb_" + tag)
    else:
        o_b = _sb_sample(q_b, k_b, v_b, k_past, v_past, "sb_" + tag)

    x1 = _outproj_ln(o_a.reshape(m, WIDTH), o_b.reshape(m, WIDTH), x2, wo_a, wo_b,
                     ln1_g, ln1_b, alpha, "outproj_" + tag)
    y = _ffn_ln(x1, w_up, w_down, ln2_g, ln2_b, alpha, "ffn_" + tag)
    return (y.reshape(bsz, seq, d), conv_new, s_new,
            k_out.reshape(bsz, seq, N_HEADS, HEAD_DIM), v_out.reshape(bsz, seq, N_HEADS, HEAD_DIM))


def kernel(x_prompt, x_sample, state_gdn_conv, state_gdn_S, cache_sb_k, cache_sb_v, w_in, conv_w, a_log, dt_bias, gdn_norm_w, w_out, ln1_g, ln1_b, w_up, w_down, ln2_g, ln2_b):
    depth = w_in.shape[0]
    alpha = (2 * depth) ** 0.25
    off_b = 4 * WIDTH
    off_sb = off_b + 2 * N_HEADS
    yp, ys = x_prompt, x_sample
    outs = [[] for _ in range(8)]
    for l in range(depth):
        wl = w_in[l]
        wl = wl.astype(BF16)
        w_gate = jnp.pad(wl[:, off_b:off_sb], ((0, 0), (0, LANES - 2 * N_HEADS)))
        w_main = (wl, wl[:, off_sb:])
        pad_row = lambda v: jnp.pad(v.astype(F32), (N_HEADS, LANES - 2 * N_HEADS)).reshape(1, LANES)
        wts = (w_main, w_gate,
               conv_w[l], pad_row(a_log[l]), pad_row(dt_bias[l]),
               gdn_norm_w[l].astype(F32).reshape(1, HEAD_DIM),
               w_out[l, :WIDTH].astype(BF16), w_out[l, WIDTH:].astype(BF16),
               ln1_g[l].reshape(1, -1), ln1_b[l].reshape(1, -1),
               w_up[l].astype(BF16), w_down[l].astype(BF16),
               ln2_g[l].reshape(1, -1), ln2_b[l].reshape(1, -1))
        bp = yp.shape[0]
        zero_conv = jnp.zeros((bp, CONV_W - 1, 3 * WIDTH), F32)
        zero_s = jnp.zeros((bp, N_HEADS, HEAD_DIM, HEAD_DIM), F32)
        yp, c1, s1, k1, v1 = _layer(yp, zero_conv, zero_s, None, None, wts, alpha, "p%d" % l)
        ys, c2, s2, k2, v2 = _layer(ys, state_gdn_conv[l], state_gdn_S[l],
                                    cache_sb_k[l], cache_sb_v[l], wts, alpha, "s%d" % l)
        for lst, val in zip(outs, (c1, s1, k1, v1, c2, s2, k2, v2)):
            lst.append(val)
    return (yp, ys) + tuple(jnp.stack(o) for o in outs)
```

```python
import functools

import jax
import jax.numpy as jnp
from jax import lax
from jax.experimental import pallas as pl
from jax.experimental.pallas import tpu as pltpu

HEAD_DIM = 128
N_HEADS = 8
WIDTH = N_HEADS * HEAD_DIM
CONV_W = 4
CHUNK = 128
LN_EPS = 1e-5
RMS_EPS = 1e-6
L2_EPS = 1e-6
LANES = 128
SUBLANES = 8
VMEM_LIMIT = 56 * 1024 * 1024

F32 = jnp.float32
BF16 = jnp.bfloat16


def _pick(n, pref):
    if n <= pref:
        return n
    t = pref
    while n % t:
        t //= 2
    return t


def _dot(a, b):
    return jnp.dot(a, b, preferred_element_type=F32)


def _dot_nt(a, b):
    return lax.dot_general(a, b, (((1,), (1,)), ((), ())), preferred_element_type=F32)


def _sigmoid(x):
    return 1.0 / (1.0 + jnp.exp(-x))


def _softplus(x):
    return jnp.maximum(x, 0.0) + jnp.log1p(jnp.exp(-jnp.abs(x)))


PROJ_TM = 1024
PROJ_TN = 512
N_PRE_TILES = 4 * WIDTH // PROJ_TN
N_SB_TILES = WIDTH // PROJ_TN
HEADS_PER_TILE = PROJ_TN // HEAD_DIM


def _proj_kernel(x_ref, wa_ref, wsb_ref, wg_ref, pre_ref, ba_ref, q_ref, kout_ref, vout_ref,
                 xb_ref, *, tm):
    j = pl.program_id(1)

    @pl.when(j == 0)
    def _():
        xb_ref[...] = x_ref[...].astype(BF16)
        ba_ref[...] = _dot(xb_ref[...], wg_ref[...])

    @pl.when(j < N_PRE_TILES)
    def _():
        pre_ref[...] = _dot(xb_ref[...], wa_ref[...])

    @pl.when((j >= N_PRE_TILES) & (j < N_PRE_TILES + N_SB_TILES))
    def _():
        q_ref[...] = _dot(xb_ref[...], wsb_ref[...]).astype(BF16)

    def keep(out_ref, first_head):
        res = _dot(xb_ref[...], wsb_ref[...])
        for h in range(HEADS_PER_TILE):
            out_ref[pl.ds(first_head + h, tm, stride=N_HEADS), :] = res[:, h * HEAD_DIM:(h + 1) * HEAD_DIM]

    for t in range(N_SB_TILES):
        pl.when(j == N_PRE_TILES + N_SB_TILES + t)(
            functools.partial(keep, kout_ref, t * HEADS_PER_TILE))
        pl.when(j == N_PRE_TILES + 2 * N_SB_TILES + t)(
            functools.partial(keep, vout_ref, t * HEADS_PER_TILE))


def _proj(x, w_main, w_gate, name):
    w_in, w_sb = w_main
    m, k = x.shape
    tm = _pick(m, PROJ_TM)
    nt = N_PRE_TILES + 3 * N_SB_TILES
    row = lambda w: pl.BlockSpec((tm, w), lambda i, j: (i, 0))
    return pl.pallas_call(
        functools.partial(_proj_kernel, tm=tm),
        grid=(m // tm, nt),
        in_specs=[row(k),
                  pl.BlockSpec((k, PROJ_TN), lambda i, j: (0, jnp.minimum(j, N_PRE_TILES - 1))),
                  pl.BlockSpec((k, PROJ_TN), lambda i, j: (0, jnp.maximum(j - N_PRE_TILES, 0))),
                  pl.BlockSpec((k, LANES), lambda i, j: (0, 0))],
        out_specs=[pl.BlockSpec((tm, PROJ_TN), lambda i, j: (i, jnp.minimum(j, N_PRE_TILES - 1))),
                   row(LANES),
                   pl.BlockSpec((tm, PROJ_TN),
                                lambda i, j: (i, jnp.clip(j - N_PRE_TILES, 0, N_SB_TILES - 1))),
                   pl.BlockSpec((tm * N_HEADS, HEAD_DIM), lambda i, j: (i, 0)),
                   pl.BlockSpec((tm * N_HEADS, HEAD_DIM), lambda i, j: (i, 0))],
        out_shape=[jax.ShapeDtypeStruct((m, 4 * WIDTH), F32),
                   jax.ShapeDtypeStruct((m, LANES), F32),
                   jax.ShapeDtypeStruct((m, WIDTH), BF16),
                   jax.ShapeDtypeStruct((m * N_HEADS, HEAD_DIM), F32),
                   jax.ShapeDtypeStruct((m * N_HEADS, HEAD_DIM), F32)],
        scratch_shapes=[pltpu.VMEM((tm, k), BF16)],
        compiler_params=pltpu.CompilerParams(
            dimension_semantics=("parallel", "arbitrary"), vmem_limit_bytes=VMEM_LIMIT),
        name=name,
    )(x, w_in, w_sb, w_gate)


NN = (((1,), (0,)), ((), ()))
NT = (((1,), (1,)), ((), ()))
TN = (((0,), (0,)), ((), ()))

GDN_ROWS_PER_STEP = 512
GDN_PIECES = dict(qk=1, inv=1, uw=1, scan=1)


def _pieces(x, n):
    out = []
    r = x
    for i in range(n):
        p = r.astype(BF16)
        out.append(p)
        if i + 1 < n:
            r = r - p.astype(F32)
    return out


def _mm(a, b, dims=NN):
    order = max(len(a), len(b))
    acc = None
    for i, ai in enumerate(a):
        for j, bj in enumerate(b):
            if i + j < order:
                t = lax.dot_general(ai, bj, dims, preferred_element_type=F32)
                acc = t if acc is None else acc + t
    return acc


def _gdn_kernel(q_ref, k_ref, v_ref, z_ref, ba_ref, cq_ref, ck_ref, cv_ref,
                wq_ref, wk_ref, wv_ref, s0_ref, alog_ref, dtb_ref, nw_ref,
                o_ref, s_ref, tail_ref, *, nb, rows, chunk):
    C = chunk
    probs = [(bi, h) for bi in range(nb) for h in range(N_HEADS)]
    P = range(len(probs))
    row = lax.broadcasted_iota(jnp.int32, (C, C), 0)
    col = lax.broadcasted_iota(jnp.int32, (C, C), 1)
    incl = row >= col
    strict = row > col
    l_incl = [incl.astype(BF16)]
    eye = (row == col).astype(F32)
    lane = lax.broadcasted_iota(jnp.int32, (C, LANES), 1)
    sel_r = lax.broadcasted_iota(jnp.int32, (LANES, LANES), 0)
    sel_col = [[(sel_r == h + N_HEADS).astype(BF16)] for h in range(N_HEADS)]
    sel_row = [[(lane == h + N_HEADS).astype(BF16)] for h in range(N_HEADS)]
    level_masks = []
    b = 1
    while b < C:
        s = b.bit_length() - 1
        same = (row >> (s + 1)) == (col >> (s + 1))
        level_masks.append(same & (((row >> s) & 1) == 1) & (((col >> s) & 1) == 0))
        b *= 2
    n_qk, n_inv, n_uw, n_scan = (GDN_PIECES[c] for c in ("qk", "inv", "uw", "scan"))

    neg_a = -jnp.exp(alog_ref[...])
    dtb = dtb_ref[...]
    nw = nw_ref[...]
    wq = wq_ref[...]
    wk = wk_ref[...]
    wv = wv_ref[...]

    @pl.when(pl.program_id(1) == 0)
    def _():
        s_ref[...] = s0_ref[...]
        tail_ref[0] = cq_ref[...]
        tail_ref[1] = ck_ref[...]
        tail_ref[2] = cv_ref[...]

    def conv(e, w):
        first = SUBLANES - (CONV_W - 1)
        acc = e[first:first + C] * w[0:1]
        for i in range(1, CONV_W):
            acc = acc + e[first + i:first + i + C] * w[i:i + 1]
        return acc * _sigmoid(acc)

    def head(x, i):
        return x[:, i * HEAD_DIM:(i + 1) * HEAD_DIM]

    def body(n, carry):
        tq, tk, tv = carry
        c0 = pl.multiple_of(n * C, C)
        xq = [q_ref[bi, pl.ds(c0, C), :] for bi in range(nb)]
        xk = [k_ref[bi, pl.ds(c0, C), :] for bi in range(nb)]
        xv = [v_ref[bi, pl.ds(c0, C), :] for bi in range(nb)]
        q_all = [conv(jnp.concatenate([tq[bi], xq[bi]], axis=0), wq) for bi in range(nb)]
        k_all = [conv(jnp.concatenate([tk[bi], xk[bi]], axis=0), wk) for bi in range(nb)]
        v_all = [conv(jnp.concatenate([tv[bi], xv[bi]], axis=0), wv) for bi in range(nb)]
        gate_all = []
        beta_all = []
        g_all = []
        for bi in range(nb):
            zc = z_ref[bi, pl.ds(c0, C), :]
            gate_all.append(zc * _sigmoid(zc))
            ba = ba_ref[bi, pl.ds(c0, C), :]
            beta_all.append(_sigmoid(ba))
            g_all.append(neg_a * _softplus(ba + dtb))

        q = [head(q_all[bi], h) for bi, h in probs]
        k = [head(k_all[bi], h) for bi, h in probs]
        q = [x * lax.rsqrt(jnp.sum(x * x, axis=-1, keepdims=True) + L2_EPS) * (HEAD_DIM ** -0.5)
             for x in q]
        k = [x * lax.rsqrt(jnp.sum(x * x, axis=-1, keepdims=True) + L2_EPS) for x in k]
        beta = [jnp.sum(jnp.where(lane == h, beta_all[bi], 0.0), axis=-1, keepdims=True)
                for bi, h in probs]
        k_beta = [k[i] * beta[i] for i in P]
        kp = [_pieces(k[i], n_qk) for i in P]

        gc_all = [_pieces(_mm(l_incl, _pieces(g_all[bi], 3)), 3) for bi in range(nb)]
        gcb = [_mm(gc_all[bi], sel_col[h]) for bi, h in probs]
        gc_row = [_mm(sel_row[h], gc_all[bi], NT) for bi, h in probs]
        kk = [_mm(_pieces(k_beta[i], n_qk), kp[i], NT) for i in P]
        qk = [_mm(_pieces(q[i], n_qk), kp[i], NT) for i in P]

        decay = [jnp.where(incl, jnp.exp(gcb[i][:, :C] - gc_row[i]), 0.0) for i in P]
        egc = [jnp.exp(gcb[i]) for i in P]
        g_last = [gcb[i][C - 1:C, :] for i in P]
        m = [jnp.where(strict, kk[i] * decay[i], 0.0) for i in P]
        attn = [qk[i] * decay[i] for i in P]

        t_inv = [eye - jnp.where(level_masks[0], m[i], 0.0) for i in P]
        for mask in level_masks[1:]:
            tp = [_pieces(t_inv[i], n_inv) for i in P]
            x = [_mm(tp[i], _pieces(jnp.where(mask, m[i], 0.0), n_inv)) for i in P]
            y = [_mm(_pieces(x[i], n_inv), tp[i]) for i in P]
            t_inv = [t_inv[i] - y[i] for i in P]

        rhs = [jnp.concatenate([head(v_all[probs[i][0]], probs[i][1]) * beta[i], k_beta[i] * egc[i]],
                               axis=1) for i in P]
        uw = [_mm(_pieces(t_inv[i], n_uw), _pieces(rhs[i], n_uw)) for i in P]

        s_old = [s_ref[bi, h] for bi, h in probs]
        lhs = [jnp.concatenate([uw[i][:, HEAD_DIM:], q[i] * egc[i]], axis=0) for i in P]
        ws = [_mm(_pieces(lhs[i], n_scan), _pieces(s_old[i], n_scan)) for i in P]
        v_new = [uw[i][:, :HEAD_DIM] - ws[i][:C] for i in P]
        vp = [_pieces(v_new[i], n_scan) for i in P]
        k_tail = [k[i] * jnp.exp(g_last[i] - gcb[i]) for i in P]

        o = [ws[i][C:] + _mm(_pieces(attn[i], n_scan), vp[i]) for i in P]
        s_new = [s_old[i] * jnp.exp(g_last[i]) + _mm(_pieces(k_tail[i], n_scan), vp[i], TN)
                 for i in P]
        for i, (bi, h) in enumerate(probs):
            s_ref[bi, h] = s_new[i]
            on = (o[i] * lax.rsqrt(jnp.mean(o[i] * o[i], axis=-1, keepdims=True) + RMS_EPS)
                  * nw * head(gate_all[bi], h))
            o_ref[bi, pl.ds(c0, C), h * HEAD_DIM:(h + 1) * HEAD_DIM] = on.astype(o_ref.dtype)
        last = lambda xs: jnp.stack([x[C - 8:C] for x in xs])
        return last(xq), last(xk), last(xv)

    tails = lax.fori_loop(0, rows // C, body, (tail_ref[0], tail_ref[1], tail_ref[2]))
    for j in range(3):
        tail_ref[j] = tails[j]


GDN_BATCH_PER_STEP = 2
GDN_BATCH_SHORT = 4


def _gdn(pre, ba, conv8, s0, conv_w, alog_row, dtb_row, nw_row, name):
    bsz, seq, _ = pre.shape
    chunk = CHUNK if seq % CHUNK == 0 else seq
    rows = _pick(seq, GDN_ROWS_PER_STEP)
    nb = _pick(bsz, GDN_BATCH_PER_STEP if seq > chunk else GDN_BATCH_SHORT)

    def col(grp):
        return pl.BlockSpec((nb, rows, WIDTH), lambda b, t: (b, t, grp))

    def st(grp):
        return pl.BlockSpec((nb, SUBLANES, WIDTH), lambda b, t: (b, 0, grp))

    def cw(grp):
        return pl.BlockSpec((CONV_W, WIDTH), lambda b, t: (0, grp))

    row = pl.BlockSpec((1, LANES), lambda b, t: (0, 0))
    state = pl.BlockSpec((nb, N_HEADS, HEAD_DIM, HEAD_DIM), lambda b, t: (b, 0, 0, 0))
    return pl.pallas_call(
        functools.partial(_gdn_kernel, nb=nb, rows=rows, chunk=chunk),
        grid=(bsz // nb, seq // rows),
        in_specs=[col(0), col(1), col(2), col(3),
                  pl.BlockSpec((nb, rows, LANES), lambda b, t: (b, t, 0)),
                  st(0), st(1), st(2), cw(0), cw(1), cw(2), state, row, row, row],
        out_specs=[pl.BlockSpec((nb, rows, WIDTH), lambda b, t: (b, t, 0)), state],
        out_shape=[jax.ShapeDtypeStruct((bsz, seq, WIDTH), BF16),
                   jax.ShapeDtypeStruct((bsz, N_HEADS, HEAD_DIM, HEAD_DIM), F32)],
        scratch_shapes=[pltpu.VMEM((3, nb, SUBLANES, WIDTH), F32)],
        compiler_params=pltpu.CompilerParams(
            dimension_semantics=("parallel", "arbitrary"), vmem_limit_bytes=VMEM_LIMIT),
        name=name,
    )(pre, pre, pre, pre, ba, conv8, conv8, conv8, conv_w, conv_w, conv_w, s0,
      alog_row, dtb_row, nw_row)


SB_KEYS = 256
SB_DEAD = -104.0
SB_BOUND_SLACK = 1.05


def _suffix_ones(n):
    r = lax.broadcasted_iota(jnp.int32, (n, n), 0)
    c = lax.broadcasted_iota(jnp.int32, (n, n), 1)
    return (r >= c).astype(BF16)


def _sb_split(z, mask):
    l1m = -(jnp.maximum(z, 0.0) + jnp.log(1.0 + jnp.exp(-jnp.abs(z))))
    if mask is not None:
        l1m = jnp.where(mask, l1m, 0.0)
    hi = l1m.astype(BF16)
    return hi, (l1m - hi.astype(F32)).astype(BF16)


def _sb_sums(hi, lo, suffix):
    tk = suffix.shape[0]
    tiles = [slice(s * tk, (s + 1) * tk) for s in range(hi.shape[1] // tk)]
    return [_dot(hi[:, t], suffix) + _dot(lo[:, t], suffix) for t in tiles]


def _sb_weights(z, cs, mask, run):
    n = len(cs)
    reps = cs[0].shape[1] // LANES
    tot = [None] * n
    for s in reversed(range(n)):
        tot[s] = run if reps == 1 else jnp.concatenate([run] * reps, axis=1)
        run = run + jnp.broadcast_to(cs[s][:, 0:1], run.shape)
    off = [cs[s] + tot[s] for s in range(n)]
    w = jnp.exp(z + (off[0] if n == 1 else jnp.concatenate(off, axis=1)))
    if mask is not None:
        w = jnp.where(mask, w, 0.0)
    return w.astype(BF16), run


def _sb_prompt_kernel(q_ref, k_ref, v_ref, o_ref, kb_ref, vb_ref, kmax_ref, *, tq, hb, qpb, scale):
    step = pl.program_id(2)
    qi0 = step * qpb
    heads = range(hb)
    probs = [(sub, h) for sub in range(qpb) for h in heads]
    ones = jnp.ones((HEAD_DIM, LANES), BF16)

    @pl.when(step == 0)
    def _():
        zeros = jnp.zeros((tq, HEAD_DIM), BF16)
        for h in heads:
            kb_ref[h, 0:tq, :] = zeros
            vb_ref[h, 0:tq, :] = zeros
            kmax_ref[h] = jnp.zeros(kmax_ref.shape[1:], F32)

    nk = qpb * tq
    dst = pl.multiple_of((qi0 + 1) * tq, tq)
    for h in heads:
        first = pl.program_id(1) * hb + h
        kh = k_ref[pl.ds(first, nk, stride=N_HEADS), :]
        kb_ref[h, pl.ds(dst, nk), :] = kh.astype(BF16)
        vb_ref[h, pl.ds(dst, nk), :] = v_ref[pl.ds(first, nk, stride=N_HEADS), :].astype(BF16)
        ksq = _dot((kh * kh).astype(BF16), ones)
        kmax_ref[h] = jnp.maximum(kmax_ref[h], jnp.broadcast_to(
            jnp.max(ksq, axis=0, keepdims=True), kmax_ref.shape[1:]))

    q = q_ref[...]
    qf = [q[sub * tq:(sub + 1) * tq, h * HEAD_DIM:(h + 1) * HEAD_DIM] for sub, h in probs]
    qb = [x.astype(BF16) for x in qf]
    zbound = [jnp.sqrt(_dot((x * x).astype(BF16), ones) * kmax_ref[h, 0:1, :])
              * (scale * SB_BOUND_SLACK) for x, (_, h) in zip(qf, probs)]
    lr = _suffix_ones(SB_KEYS)

    def sweep(ps, row0, nkeys, mask, acc, run):
        n = range(len(ps))
        hd = [probs[p][1] for p in ps]
        z = [_dot_nt(qb[ps[i]], kb_ref[hd[i], pl.ds(row0[i], nkeys), :]) * scale for i in n]
        pieces = [_sb_split(z[i], mask[i]) for i in n]
        ct = [_sb_sums(*pieces[i], lr) for i in n]
        wr = [_sb_weights(z[i], ct[i], mask[i], run[i]) for i in n]
        acc = [acc[i] + _dot(wr[i][0], vb_ref[hd[i], pl.ds(row0[i], nkeys), :]) for i in n]
        return acc, [wr[i][1] for i in n]

    r = lax.broadcasted_iota(jnp.int32, (tq, 2 * tq), 0)
    c = lax.broadcasted_iota(jnp.int32, (tq, 2 * tq), 1)
    every = list(range(len(probs)))
    acc, run = sweep(
        every,
        [pl.multiple_of((qi0 + sub) * tq, tq) for sub, _ in probs], 2 * tq,
        [(c < r + tq) & (c + (qi0 + sub) * tq >= tq) for sub, _ in probs],
        [jnp.zeros((tq, HEAD_DIM), F32) for _ in probs],
        [jnp.zeros((tq, LANES), F32) for _ in probs])

    for sub in range(qpb):
        ps = [p for p in every if probs[p][0] == sub]

        def alive(run, ps=ps):
            live = jnp.max(run[0] + zbound[ps[0]]) > SB_DEAD
            for i in range(1, len(ps)):
                live = live | (jnp.max(run[i] + zbound[ps[i]]) > SB_DEAD)
            return live

        def cond(carry):
            return (carry[0] >= 0) & carry[3]

        def body(carry, ps=ps, alive=alive):
            j, acc, run, _ = carry
            row0 = pl.multiple_of((j + 1) * tq, tq)
            acc, run = sweep(ps, [row0] * len(ps), tq, [None] * len(ps), acc, run)
            return j - 1, acc, run, alive(run)

        acc_s = [acc[p] for p in ps]
        run_s = [run[p] for p in ps]
        _, acc_s, _, _ = lax.while_loop(cond, body, (qi0 + sub - 2, acc_s, run_s, alive(run_s)))
        o_ref[sub * tq:(sub + 1) * tq, :] = jnp.concatenate(acc_s, axis=1).astype(o_ref.dtype)


SB_HEADS_PER_STEP = 2
SB_QBLOCKS_PER_STEP = 2
SB_SAMPLE_KEYS = 2048


def _sb_prompt(q, k, v, name):
    bsz, seq, _ = q.shape
    tq = _pick(seq, 256)
    hb = SB_HEADS_PER_STEP
    qpb = _pick(seq // tq, SB_QBLOCKS_PER_STEP)
    wid = hb * HEAD_DIM
    kv = pl.BlockSpec((None, qpb * tq * N_HEADS, HEAD_DIM), lambda b, h, i: (b, i, 0))
    qo = pl.BlockSpec((None, qpb * tq, wid), lambda b, h, i: (b, i, h))
    return pl.pallas_call(
        functools.partial(_sb_prompt_kernel, tq=tq, hb=hb, qpb=qpb, scale=HEAD_DIM ** -0.5),
        grid=(bsz, N_HEADS // hb, seq // (tq * qpb)),
        in_specs=[qo, kv, kv],
        out_specs=qo,
        out_shape=jax.ShapeDtypeStruct((bsz, seq, WIDTH), BF16),
        scratch_shapes=[pltpu.VMEM((hb, seq + tq, HEAD_DIM), BF16),
                        pltpu.VMEM((hb, seq + tq, HEAD_DIM), BF16),
                        pltpu.VMEM((hb, SUBLANES, LANES), F32)],
        compiler_params=pltpu.CompilerParams(
            dimension_semantics=("parallel", "parallel", "arbitrary"),
            vmem_limit_bytes=VMEM_LIMIT),
        name=name,
    )(q, k, v)


def _sb_sample_kernel(q_ref, kn_ref, vn_ref, kc_ref, vc_ref, o_ref, qb_ref, acc_ref, run_ref,
                      *, seq, keys, scale):
    step = pl.program_id(1)
    heads = range(N_HEADS)
    rows = N_HEADS * seq
    def sweep(k_tiles, v_tiles, mask, lr):
        z = jnp.concatenate([_dot_nt(qb_ref[h * seq:(h + 1) * seq, :], k_tiles[h]) for h in heads],
                            axis=0) * scale
        wb, run = _sb_weights(z, _sb_sums(*_sb_split(z, mask), lr), mask, run_ref[...])
        run_ref[...] = run
        pv = [_dot(wb[h * seq:(h + 1) * seq, :], v_tiles[h]) for h in heads]
        acc_ref[...] += jnp.concatenate(pv, axis=0)

    @pl.when(step == 0)
    def _():
        for h in heads:
            qb_ref[h * seq:(h + 1) * seq, :] = q_ref[:, h * HEAD_DIM:(h + 1) * HEAD_DIM].astype(BF16)
        acc_ref[...] = jnp.zeros_like(acc_ref)
        run_ref[...] = jnp.zeros_like(run_ref)
        pad = jnp.zeros((LANES - seq, HEAD_DIM), BF16)
        kn = [jnp.concatenate([kn_ref[pl.ds(h, seq, stride=N_HEADS), :].astype(BF16), pad], axis=0)
              for h in heads]
        vn = [jnp.concatenate([vn_ref[pl.ds(h, seq, stride=N_HEADS), :].astype(BF16), pad], axis=0)
              for h in heads]
        r = lax.broadcasted_iota(jnp.int32, (rows, LANES), 0)
        c = lax.broadcasted_iota(jnp.int32, (rows, LANES), 1)
        sweep(kn, vn, c < (r & (seq - 1)), _suffix_ones(LANES))

    kt = [kc_ref[pl.ds(h, keys, stride=N_HEADS), :].astype(BF16) for h in heads]
    vt = [vc_ref[pl.ds(h, keys, stride=N_HEADS), :].astype(BF16) for h in heads]
    sweep(kt, vt, None, _suffix_ones(SB_KEYS))

    @pl.when(step == pl.num_programs(1) - 1)
    def _():
        for h in heads:
            o_ref[:, h * HEAD_DIM:(h + 1) * HEAD_DIM] = acc_ref[h * seq:(h + 1) * seq, :].astype(o_ref.dtype)


def _sb_sample(q, kn, vn, kc, vc, name):
    bsz, seq, _ = q.shape
    past = kc.shape[1]
    kc = kc.reshape(bsz, past * N_HEADS, HEAD_DIM)
    vc = vc.reshape(bsz, past * N_HEADS, HEAD_DIM)
    assert seq & (seq - 1) == 0 and seq <= LANES and past % SB_KEYS == 0
    keys = _pick(past, SB_SAMPLE_KEYS)
    nsteps = past // keys
    new = pl.BlockSpec((None, seq, WIDTH), lambda b, j: (b, 0, 0))
    newkv = pl.BlockSpec((None, seq * N_HEADS, HEAD_DIM), lambda b, j: (b, 0, 0))
    old = pl.BlockSpec((None, keys * N_HEADS, HEAD_DIM), lambda b, j: (b, nsteps - 1 - j, 0))
    rows = N_HEADS * seq
    return pl.pallas_call(
        functools.partial(_sb_sample_kernel, seq=seq, keys=keys, scale=HEAD_DIM ** -0.5),
        grid=(bsz, nsteps),
        in_specs=[new, newkv, newkv, old, old],
        out_specs=new,
        out_shape=jax.ShapeDtypeStruct((bsz, seq, WIDTH), BF16),
        scratch_shapes=[pltpu.VMEM((rows, HEAD_DIM), BF16), pltpu.VMEM((rows, HEAD_DIM), F32),
                        pltpu.VMEM((rows, LANES), F32)],
        compiler_params=pltpu.CompilerParams(
            dimension_semantics=("parallel", "arbitrary"), vmem_limit_bytes=VMEM_LIMIT),
        name=name,
    )(q, kn, vn, kc, vc)


def _layer_norm(r, g, b):
    mu = jnp.mean(r, axis=-1, keepdims=True)
    d = r - mu
    var = jnp.mean(d * d, axis=-1, keepdims=True)
    return d * lax.rsqrt(var + LN_EPS) * g + b


def _outproj_kernel(oa_ref, ob_ref, x_ref, wa_ref, wb_ref, g_ref, b_ref, y_ref, *, alpha, sub):
    for s in range(y_ref.shape[0] // sub):
        rs = slice(s * sub, (s + 1) * sub)
        acc = _dot(oa_ref[rs, :], wa_ref[...]) + _dot(ob_ref[rs, :], wb_ref[...])
        y_ref[rs, :] = _layer_norm(alpha * x_ref[rs, :] + acc, g_ref[...], b_ref[...])


def _outproj_ln(oa, ob, x, wa, wb, g, b, alpha, name):
    m, d = x.shape
    tm = _pick(m, 512)
    rows = lambda w: pl.BlockSpec((tm, w), lambda i: (i, 0))
    full = lambda a: pl.BlockSpec(a.shape, lambda i: (0, 0))
    return pl.pallas_call(
        functools.partial(_outproj_kernel, alpha=alpha, sub=_pick(tm, 256)),
        grid=(m // tm,),
        in_specs=[rows(WIDTH), rows(WIDTH), rows(d), full(wa), full(wb), full(g), full(b)],
        out_specs=rows(d),
        out_shape=jax.ShapeDtypeStruct((m, d), F32),
        compiler_params=pltpu.CompilerParams(
            dimension_semantics=("parallel",), vmem_limit_bytes=VMEM_LIMIT),
        name=name,
    )(oa, ob, x, wa, wb, g, b)


FFN_TM = 512
FFN_TF = 1024


def _ffn_kernel(x_ref, wu_ref, wd_ref, g_ref, b_ref, y_ref, xb_ref, *, alpha):
    j = pl.program_id(1)

    @pl.when(j == 0)
    def _():
        xb_ref[...] = x_ref[...].astype(BF16)
        y_ref[...] = jnp.zeros_like(y_ref)

    hid = jnp.maximum(_dot(xb_ref[...], wu_ref[...]), 0.0)
    y_ref[...] += _dot((hid * hid).astype(BF16), wd_ref[...])

    @pl.when(j == pl.num_programs(1) - 1)
    def _():
        y_ref[...] = _layer_norm(alpha * x_ref[...] + y_ref[...], g_ref[...], b_ref[...])


def _ffn_ln(x, wu, wd, g, b, alpha, name):
    m, d = x.shape
    dff = wu.shape[1]
    tm = _pick(m, FFN_TM)
    tf = _pick(dff, FFN_TF)
    return pl.pallas_call(
        functools.partial(_ffn_kernel, alpha=alpha),
        grid=(m // tm, dff // tf),
        in_specs=[pl.BlockSpec((tm, d), lambda i, j: (i, 0)),
                  pl.BlockSpec((d, tf), lambda i, j: (0, j)),
                  pl.BlockSpec((tf, d), lambda i, j: (j, 0)),
                  pl.BlockSpec((1, d), lambda i, j: (0, 0)),
                  pl.BlockSpec((1, d), lambda i, j: (0, 0))],
        out_specs=pl.BlockSpec((tm, d), lambda i, j: (i, 0)),
        out_shape=jax.ShapeDtypeStruct((m, d), F32),
        scratch_shapes=[pltpu.VMEM((tm, d), BF16)],
        compiler_params=pltpu.CompilerParams(
            dimension_semantics=("parallel", "arbitrary"), vmem_limit_bytes=VMEM_LIMIT),
        name=name,
    )(x, wu, wd, g, b)


def _layer(x, conv_buf, s0, k_past, v_past, wts, alpha, tag):
    (w_main, w_gate, conv_w, alog_row, dtb_row, nw_row,
     wo_a, wo_b, ln1_g, ln1_b, w_up, w_down, ln2_g, ln2_b) = wts
    bsz, seq, d = x.shape
    m = bsz * seq
    x2 = x.reshape(m, d)

    pre, ba, q_b, k_out, v_out = _proj(x2, w_main, w_gate, "proj_" + tag)
    pre = pre.reshape(bsz, seq, 4 * WIDTH)
    ba = ba.reshape(bsz, seq, LANES)
    q_b = q_b.reshape(bsz, seq, WIDTH)
    k_b = k_out.reshape(bsz, seq * N_HEADS, HEAD_DIM)
    v_b = v_out.reshape(bsz, seq * N_HEADS, HEAD_DIM)

    conv8 = jnp.pad(conv_buf, ((0, 0), (SUBLANES - (CONV_W - 1), 0), (0, 0)))
    o_a, s_new = _gdn(pre, ba, conv8, s0, conv_w, alog_row, dtb_row, nw_row, "gdn_" + tag)
    keep = CONV_W - 1
    conv_new = pre[:, seq - keep:, :3 * WIDTH] if seq >= keep else jnp.concatenate(
        [conv_buf, pre[:, :, :3 * WIDTH]], axis=1)[:, -keep:, :]

    if k_past is None:
        o_b = _sb_prompt(q_b, k_b, v_b, "sb_" + tag)
    else:
        o_b = _sb_sample(q_b, k_b, v_b, k_past, v_past, "sb_" + tag)

    x1 = _outproj_ln(o_a.reshape(m, WIDTH), o_b.reshape(m, WIDTH), x2, wo_a, wo_b,
                     ln1_g, ln1_b, alpha, "outproj_" + tag)
    y = _ffn_ln(x1, w_up, w_down, ln2_g, ln2_b, alpha, "ffn_" + tag)
    return (y.reshape(bsz, seq, d), conv_new, s_new,
            k_out.reshape(bsz, seq, N_HEADS, HEAD_DIM), v_out.reshape(bsz, seq, N_HEADS, HEAD_DIM))


def kernel(x_prompt, x_sample, state_gdn_conv, state_gdn_S, cache_sb_k, cache_sb_v, w_in, conv_w, a_log, dt_bias, gdn_norm_w, w_out, ln1_g, ln1_b, w_up, w_down, ln2_g, ln2_b):
    depth = w_in.shape[0]
    alpha = (2 * depth) ** 0.25
    off_b = 4 * WIDTH
    off_sb = off_b + 2 * N_HEADS
    yp, ys = x_prompt, x_sample
    outs = [[] for _ in range(8)]
    for l in range(depth):
        wl = w_in[l]
        wl = wl.astype(BF16)
        w_gate = jnp.pad(wl[:, off_b:off_sb], ((0, 0), (0, LANES - 2 * N_HEADS)))
        w_main = (wl, wl[:, off_sb:])
        pad_row = lambda v: jnp.pad(v.astype(F32), (N_HEADS, LANES - 2 * N_HEADS)).reshape(1, LANES)
        wts = (w_main, w_gate,
               conv_w[l], pad_row(a_log[l]), pad_row(dt_bias[l]),
               gdn_norm_w[l].astype(F32).reshape(1, HEAD_DIM),
               w_out[l, :WIDTH].astype(BF16), w_out[l, WIDTH:].astype(BF16),
               ln1_g[l].reshape(1, -1), ln1_b[l].reshape(1, -1),
               w_up[l].astype(BF16), w_down[l].astype(BF16),
               ln2_g[l].reshape(1, -1), ln2_b[l].reshape(1, -1))
        bp = yp.shape[0]
        zero_conv = jnp.zeros((bp, CONV_W - 1, 3 * WIDTH), F32)
        zero_s = jnp.zeros((bp, N_HEADS, HEAD_DIM, HEAD_DIM), F32)
        yp, c1, s1, k1, v1 = _layer(yp, zero_conv, zero_s, None, None, wts, alpha, "p%d" % l)
        ys, c2, s2, k2, v2 = _layer(ys, state_gdn_conv[l], state_gdn_S[l],
                                    cache_sb_k[l], cache_sb_v[l], wts, alpha, "s%d" % l)
        for lst, val in zip(outs, (c1, s1, k1, v1, c2, s2, k2, v2)):
            lst.append(val)
    return (yp, ys) + tuple(jnp.stack(o) for o in outs)
```

```python
import functools

import jax
import jax.numpy as jnp
from jax import lax
from jax.experimental import pallas as pl
from jax.experimental.pallas import tpu as pltpu

HEAD_DIM = 128
N_HEADS = 8
WIDTH = N_HEADS * HEAD_DIM
CONV_W = 4
CHUNK = 128
LN_EPS = 1e-5
RMS_EPS = 1e-6
L2_EPS = 1e-6
LANES = 128
SUBLANES = 8
VMEM_LIMIT = 56 * 1024 * 1024

F32 = jnp.float32
BF16 = jnp.bfloat16


def _pick(n, pref):
    if n <= pref:
        return n
    t = pref
    while n % t:
        t //= 2
    return t


def _dot(a, b):
    return jnp.dot(a, b, preferred_element_type=F32)


def _dot_nt(a, b):
    return lax.dot_general(a, b, (((1,), (1,)), ((), ())), preferred_element_type=F32)


def _sigmoid(x):
    return 1.0 / (1.0 + jnp.exp(-x))


def _softplus(x):
    return jnp.maximum(x, 0.0) + jnp.log1p(jnp.exp(-jnp.abs(x)))


PROJ_TM = 1024
PROJ_TN = 512
N_PRE_TILES = 4 * WIDTH // PROJ_TN
N_SB_TILES = WIDTH // PROJ_TN
HEADS_PER_TILE = PROJ_TN // HEAD_DIM


def _proj_kernel(x_ref, wa_ref, wsb_ref, wg_ref, pre_ref, ba_ref, q_ref, kout_ref, vout_ref,
                 xb_ref, *, tm):
    j = pl.program_id(1)

    @pl.when(j == 0)
    def _():
        xb_ref[...] = x_ref[...].astype(BF16)
        ba_ref[...] = _dot(xb_ref[...], wg_ref[...])

    @pl.when(j < N_PRE_TILES)
    def _():
        pre_ref[...] = _dot(xb_ref[...], wa_ref[...])

    @pl.when((j >= N_PRE_TILES) & (j < N_PRE_TILES + N_SB_TILES))
    def _():
        q_ref[...] = _dot(xb_ref[...], wsb_ref[...]).astype(BF16)

    def keep(out_ref, first_head):
        res = _dot(xb_ref[...], wsb_ref[...])
        for h in range(HEADS_PER_TILE):
            out_ref[pl.ds(first_head + h, tm, stride=N_HEADS), :] = res[:, h * HEAD_DIM:(h + 1) * HEAD_DIM]

    for t in range(N_SB_TILES):
        pl.when(j == N_PRE_TILES + N_SB_TILES + t)(
            functools.partial(keep, kout_ref, t * HEADS_PER_TILE))
        pl.when(j == N_PRE_TILES + 2 * N_SB_TILES + t)(
            functools.partial(keep, vout_ref, t * HEADS_PER_TILE))


def _proj(x, w_main, w_gate, name):
    w_in, w_sb = w_main
    m, k = x.shape
    tm = _pick(m, PROJ_TM)
    nt = N_PRE_TILES + 3 * N_SB_TILES
    row = lambda w: pl.BlockSpec((tm, w), lambda i, j: (i, 0))
    return pl.pallas_call(
        functools.partial(_proj_kernel, tm=tm),
        grid=(m // tm, nt),
        in_specs=[row(k),
                  pl.BlockSpec((k, PROJ_TN), lambda i, j: (0, jnp.minimum(j, N_PRE_TILES - 1))),
                  pl.BlockSpec((k, PROJ_TN), lambda i, j: (0, jnp.maximum(j - N_PRE_TILES, 0))),
                  pl.BlockSpec((k, LANES), lambda i, j: (0, 0))],
        out_specs=[pl.BlockSpec((tm, PROJ_TN), lambda i, j: (i, jnp.minimum(j, N_PRE_TILES - 1))),
                   row(LANES),
                   pl.BlockSpec((tm, PROJ_TN),
                                lambda i, j: (i, jnp.clip(j - N_PRE_TILES, 0, N_SB_TILES - 1))),
                   pl.BlockSpec((tm * N_HEADS, HEAD_DIM), lambda i, j: (i, 0)),
                   pl.BlockSpec((tm * N_HEADS, HEAD_DIM), lambda i, j: (i, 0))],
        out_shape=[jax.ShapeDtypeStruct((m, 4 * WIDTH), F32),
                   jax.ShapeDtypeStruct((m, LANES), F32),
                   jax.ShapeDtypeStruct((m, WIDTH), BF16),
                   jax.ShapeDtypeStruct((m * N_HEADS, HEAD_DIM), F32),
                   jax.ShapeDtypeStruct((m * N_HEADS, HEAD_DIM), F32)],
        scratch_shapes=[pltpu.VMEM((tm, k), BF16)],
        compiler_params=pltpu.CompilerParams(
            dimension_semantics=("parallel", "arbitrary"), vmem_limit_bytes=VMEM_LIMIT),
        name=name,
    )(x, w_in, w_sb, w_gate)


NN = (((1,), (0,)), ((), ()))
NT = (((1,), (1,)), ((), ()))
TN = (((0,), (0,)), ((), ()))

GDN_ROWS_PER_STEP = 512
GDN_PIECES = dict(qk=1, inv=1, uw=1, scan=1)


def _pieces(x, n):
    out = []
    r = x
    for i in range(n):
        p = r.astype(BF16)
        out.append(p)
        if i + 1 < n:
            r = r - p.astype(F32)
    return out


def _mm(a, b, dims=NN):
    order = max(len(a), len(b))
    acc = None
    for i, ai in enumerate(a):
        for j, bj in enumerate(b):
            if i + j < order:
                t = lax.dot_general(ai, bj, dims, preferred_element_type=F32)
                acc = t if acc is None else acc + t
    return acc


def _gdn_kernel(q_ref, k_ref, v_ref, z_ref, ba_ref, cq_ref, ck_ref, cv_ref,
                wq_ref, wk_ref, wv_ref, s0_ref, alog_ref, dtb_ref, nw_ref,
                o_ref, s_ref, tail_ref, *, nb, rows, chunk):
    C = chunk
    probs = [(bi, h) for bi in range(nb) for h in range(N_HEADS)]
    P = range(len(probs))
    row = lax.broadcasted_iota(jnp.int32, (C, C), 0)
    col = lax.broadcasted_iota(jnp.int32, (C, C), 1)
    incl = row >= col
    strict = row > col
    l_incl = [incl.astype(BF16)]
    eye = (row == col).astype(F32)
    lane = lax.broadcasted_iota(jnp.int32, (C, LANES), 1)
    sel_r = lax.broadcasted_iota(jnp.int32, (LANES, LANES), 0)
    sel_col = [[(sel_r == h + N_HEADS).astype(BF16)] for h in range(N_HEADS)]
    sel_row = [[(lane == h + N_HEADS).astype(BF16)] for h in range(N_HEADS)]
    level_masks = []
    b = 1
    while b < C:
        s = b.bit_length() - 1
        same = (row >> (s + 1)) == (col >> (s + 1))
        level_masks.append(same & (((row >> s) & 1) == 1) & (((col >> s) & 1) == 0))
        b *= 2
    n_qk, n_inv, n_uw, n_scan = (GDN_PIECES[c] for c in ("qk", "inv", "uw", "scan"))

    neg_a = -jnp.exp(alog_ref[...])
    dtb = dtb_ref[...]
    nw = nw_ref[...]
    wq = wq_ref[...]
    wk = wk_ref[...]
    wv = wv_ref[...]

    @pl.when(pl.program_id(1) == 0)
    def _():
        s_ref[...] = s0_ref[...]
        tail_ref[0] = cq_ref[...]
        tail_ref[1] = ck_ref[...]
        tail_ref[2] = cv_ref[...]

    def conv(e, w):
        first = SUBLANES - (CONV_W - 1)
        acc = e[first:first + C] * w[0:1]
        for i in range(1, CONV_W):
            acc = acc + e[first + i:first + i + C] * w[i:i + 1]
        return acc * _sigmoid(acc)

    def head(x, i):
        return x[:, i * HEAD_DIM:(i + 1) * HEAD_DIM]

    def body(n, carry):
        tq, tk, tv = carry
        c0 = pl.multiple_of(n * C, C)
        xq = [q_ref[bi, pl.ds(c0, C), :] for bi in range(nb)]
        xk = [k_ref[bi, pl.ds(c0, C), :] for bi in range(nb)]
        xv = [v_ref[bi, pl.ds(c0, C), :] for bi in range(nb)]
        q_all = [conv(jnp.concatenate([tq[bi], xq[bi]], axis=0), wq) for bi in range(nb)]
        k_all = [conv(jnp.concatenate([tk[bi], xk[bi]], axis=0), wk) for bi in range(nb)]
        v_all = [conv(jnp.concatenate([tv[bi], xv[bi]], axis=0), wv) for bi in range(nb)]
        gate_all = []
        beta_all = []
        g_all = []
        for bi in range(nb):
            zc = z_ref[bi, pl.ds(c0, C), :]
            gate_all.append(zc * _sigmoid(zc))
            ba = ba_ref[bi, pl.ds(c0, C), :]
            beta_all.append(_sigmoid(ba))
            g_all.append(neg_a * _softplus(ba + dtb))

        q = [head(q_all[bi], h) for bi, h in probs]
        k = [head(k_all[bi], h) for bi, h in probs]
        q = [x * lax.rsqrt(jnp.sum(x * x, axis=-1, keepdims=True) + L2_EPS) * (HEAD_DIM ** -0.5)
             for x in q]
        k = [x * lax.rsqrt(jnp.sum(x * x, axis=-1, keepdims=True) + L2_EPS) for x in k]
        beta = [jnp.sum(jnp.where(lane == h, beta_all[bi], 0.0), axis=-1, keepdims=True)
                for bi, h in probs]
        k_beta = [k[i] * beta[i] for i in P]
        kp = [_pieces(k[i], n_qk) for i in P]

        gc_all = [_pieces(_mm(l_incl, _pieces(g_all[bi], 3)), 3) for bi in range(nb)]
        gcb = [_mm(gc_all[bi], sel_col[h]) for bi, h in probs]
        gc_row = [_mm(sel_row[h], gc_all[bi], NT) for bi, h in probs]
        kk = [_mm(_pieces(k_beta[i], n_qk), kp[i], NT) for i in P]
        qk = [_mm(_pieces(q[i], n_qk), kp[i], NT) for i in P]

        decay = [jnp.where(incl, jnp.exp(gcb[i][:, :C] - gc_row[i]), 0.0) for i in P]
        egc = [jnp.exp(gcb[i]) for i in P]
        g_last = [gcb[i][C - 1:C, :] for i in P]
        m = [jnp.where(strict, kk[i] * decay[i], 0.0) for i in P]
        attn = [qk[i] * decay[i] for i in P]

        t_inv = [eye - jnp.where(level_masks[0], m[i], 0.0) for i in P]
        for mask in level_masks[1:]:
            tp = [_pieces(t_inv[i], n_inv) for i in P]
            x = [_mm(tp[i], _pieces(jnp.where(mask, m[i], 0.0), n_inv)) for i in P]
            y = [_mm(_pieces(x[i], n_inv), tp[i]) for i in P]
            t_inv = [t_inv[i] - y[i] for i in P]

        rhs = [jnp.concatenate([head(v_all[probs[i][0]], probs[i][1]) * beta[i], k_beta[i] * egc[i]],
                               axis=1) for i in P]
        uw = [_mm(_pieces(t_inv[i], n_uw), _pieces(rhs[i], n_uw)) for i in P]

        s_old = [s_ref[bi, h] for bi, h in probs]
        lhs = [jnp.concatenate([uw[i][:, HEAD_DIM:], q[i] * egc[i]], axis=0) for i in P]
        ws = [_mm(_pieces(lhs[i], n_scan), _pieces(s_old[i], n_scan)) for i in P]
        v_new = [uw[i][:, :HEAD_DIM] - ws[i][:C] for i in P]
        vp = [_pieces(v_new[i], n_scan) for i in P]
        k_tail = [k[i] * jnp.exp(g_last[i] - gcb[i]) for i in P]

        o = [ws[i][C:] + _mm(_pieces(attn[i], n_scan), vp[i]) for i in P]
        s_new = [s_old[i] * jnp.exp(g_last[i]) + _mm(_pieces(k_tail[i], n_scan), vp[i], TN)
                 for i in P]
        for i, (bi, h) in enumerate(probs):
            s_ref[bi, h] = s_new[i]
            on = (o[i] * lax.rsqrt(jnp.mean(o[i] * o[i], axis=-1, keepdims=True) + RMS_EPS)
                  * nw * head(gate_all[bi], h))
            o_ref[bi, pl.ds(c0, C), h * HEAD_DIM:(h + 1) * HEAD_DIM] = on.astype(o_ref.dtype)
        last = lambda xs: jnp.stack([x[C - 8:C] for x in xs])
        return last(xq), last(xk), last(xv)

    tails = lax.fori_loop(0, rows // C, body, (tail_ref[0], tail_ref[1], tail_ref[2]))
    for j in range(3):
        tail_ref[j] = tails[j]


GDN_BATCH_PER_STEP = 2
GDN_BATCH_SHORT = 4


def _gdn(pre, ba, conv8, s0, conv_w, alog_row, dtb_row, nw_row, name):
    bsz, seq, _ = pre.shape
    chunk = CHUNK if seq % CHUNK == 0 else seq
    rows = _pick(seq, GDN_ROWS_PER_STEP)
    nb = _pick(bsz, GDN_BATCH_PER_STEP if seq > chunk else GDN_BATCH_SHORT)

    def col(grp):
        return pl.BlockSpec((nb, rows, WIDTH), lambda b, t: (b, t, grp))

    def st(grp):
        return pl.BlockSpec((nb, SUBLANES, WIDTH), lambda b, t: (b, 0, grp))

    def cw(grp):
        return pl.BlockSpec((CONV_W, WIDTH), lambda b, t: (0, grp))

    row = pl.BlockSpec((1, LANES), lambda b, t: (0, 0))
    state = pl.BlockSpec((nb, N_HEADS, HEAD_DIM, HEAD_DIM), lambda b, t: (b, 0, 0, 0))
    return pl.pallas_call(
        functools.partial(_gdn_kernel, nb=nb, rows=rows, chunk=chunk),
        grid=(bsz // nb, seq // rows),
        in_specs=[col(0), col(1), col(2), col(3),
                  pl.BlockSpec((nb, rows, LANES), lambda b, t: (b, t, 0)),
                  st(0), st(1), st(2), cw(0), cw(1), cw(2), state, row, row, row],
        out_specs=[pl.BlockSpec((nb, rows, WIDTH), lambda b, t: (b, t, 0)), state],
        out_shape=[jax.ShapeDtypeStruct((bsz, seq, WIDTH), BF16),
                   jax.ShapeDtypeStruct((bsz, N_HEADS, HEAD_DIM, HEAD_DIM), F32)],
        scratch_shapes=[pltpu.VMEM((3, nb, SUBLANES, WIDTH), F32)],
        compiler_params=pltpu.CompilerParams(
            dimension_semantics=("parallel", "arbitrary"), vmem_limit_bytes=VMEM_LIMIT),
        name=name,
    )(pre, pre, pre, pre, ba, conv8, conv8, conv8, conv_w, conv_w, conv_w, s0,
      alog_row, dtb_row, nw_row)


SB_KEYS = 256
SB_DEAD = -104.0
SB_BOUND_SLACK = 1.05


def _suffix_ones(n):
    r = lax.broadcasted_iota(jnp.int32, (n, n), 0)
    c = lax.broadcasted_iota(jnp.int32, (n, n), 1)
    return (r >= c).astype(BF16)


def _sb_split(z, mask):
    l1m = -(jnp.maximum(z, 0.0) + jnp.log(1.0 + jnp.exp(-jnp.abs(z))))
    if mask is not None:
        l1m = jnp.where(mask, l1m, 0.0)
    hi = l1m.astype(BF16)
    return hi, (l1m - hi.astype(F32)).astype(BF16)


def _sb_sums(hi, lo, suffix):
    tk = suffix.shape[0]
    tiles = [slice(s * tk, (s + 1) * tk) for s in range(hi.shape[1] // tk)]
    return [_dot(hi[:, t], suffix) + _dot(lo[:, t], suffix) for t in tiles]


def _sb_weights(z, cs, mask, run):
    n = len(cs)
    reps = cs[0].shape[1] // LANES
    tot = [None] * n
    for s in reversed(range(n)):
        tot[s] = run if reps == 1 else jnp.concatenate([run] * reps, axis=1)
        run = run + jnp.broadcast_to(cs[s][:, 0:1], run.shape)
    off = [cs[s] + tot[s] for s in range(n)]
    w = jnp.exp(z + (off[0] if n == 1 else jnp.concatenate(off, axis=1)))
    if mask is not None:
        w = jnp.where(mask, w, 0.0)
    return w.astype(BF16), run


def _sb_prompt_kernel(q_ref, k_ref, v_ref, o_ref, kb_ref, vb_ref, kmax_ref, *, tq, hb, qpb, scale):
    step = pl.program_id(2)
    qi0 = step * qpb
    heads = range(hb)
    probs = [(sub, h) for sub in range(qpb) for h in heads]
    ones = jnp.ones((HEAD_DIM, LANES), BF16)

    @pl.when(step == 0)
    def _():
        zeros = jnp.zeros((tq, HEAD_DIM), BF16)
        for h in heads:
            kb_ref[h, 0:tq, :] = zeros
            vb_ref[h, 0:tq, :] = zeros
            kmax_ref[h] = jnp.zeros(kmax_ref.shape[1:], F32)

    nk = qpb * tq
    dst = pl.multiple_of((qi0 + 1) * tq, tq)
    for h in heads:
        first = pl.program_id(1) * hb + h
        kh = k_ref[pl.ds(first, nk, stride=N_HEADS), :]
        kb_ref[h, pl.ds(dst, nk), :] = kh.astype(BF16)
        vb_ref[h, pl.ds(dst, nk), :] = v_ref[pl.ds(first, nk, stride=N_HEADS), :].astype(BF16)
        ksq = _dot((kh * kh).astype(BF16), ones)
        kmax_ref[h] = jnp.maximum(kmax_ref[h], jnp.broadcast_to(
            jnp.max(ksq, axis=0, keepdims=True), kmax_ref.shape[1:]))

    q = q_ref[...]
    qf = [q[sub * tq:(sub + 1) * tq, h * HEAD_DIM:(h + 1) * HEAD_DIM] for sub, h in probs]
    qb = [x.astype(BF16) for x in qf]
    zbound = [jnp.sqrt(_dot((x * x).astype(BF16), ones) * kmax_ref[h, 0:1, :])
              * (scale * SB_BOUND_SLACK) for x, (_, h) in zip(qf, probs)]
    lr = _suffix_ones(SB_KEYS)

    def sweep(ps, row0, nkeys, mask, acc, run):
        n = range(len(ps))
        hd = [probs[p][1] for p in ps]
        z = [_dot_nt(qb[ps[i]], kb_ref[hd[i], pl.ds(row0[i], nkeys), :]) * scale for i in n]
        pieces = [_sb_split(z[i], mask[i]) for i in n]
        ct = [_sb_sums(*pieces[i], lr) for i in n]
        wr = [_sb_weights(z[i], ct[i], mask[i], run[i]) for i in n]
        acc = [acc[i] + _dot(wr[i][0], vb_ref[hd[i], pl.ds(row0[i], nkeys), :]) for i in n]
        return acc, [wr[i][1] for i in n]

    r = lax.broadcasted_iota(jnp.int32, (tq, 2 * tq), 0)
    c = lax.broadcasted_iota(jnp.int32, (tq, 2 * tq), 1)
    every = list(range(len(probs)))
    acc, run = sweep(
        every,
        [pl.multiple_of((qi0 + sub) * tq, tq) for sub, _ in probs], 2 * tq,
        [(c < r + tq) & (c + (qi0 + sub) * tq >= tq) for sub, _ in probs],
        [jnp.zeros((tq, HEAD_DIM), F32) for _ in probs],
        [jnp.zeros((tq, LANES), F32) for _ in probs])

    for sub in range(qpb):
        ps = [p for p in every if probs[p][0] == sub]

        def alive(run, ps=ps):
            live = jnp.max(run[0] + zbound[ps[0]]) > SB_DEAD
            for i in range(1, len(ps)):
                live = live | (jnp.max(run[i] + zbound[ps[i]]) > SB_DEAD)
            return live

        def cond(carry):
            return (carry[0] >= 0) & carry[3]

        def body(carry, ps=ps, alive=alive):
            j, acc, run, _ = carry
            row0 = pl.multiple_of((j + 1) * tq, tq)
            acc, run = sweep(ps, [row0] * len(ps), tq, [None] * len(ps), acc, run)
            return j - 1, acc, run, alive(run)

        acc_s = [acc[p] for p in ps]
        run_s = [run[p] for p in ps]
        _, acc_s, _, _ = lax.while_loop(cond, body, (qi0 + sub - 2, acc_s, run_s, alive(run_s)))
        o_ref[sub * tq:(sub + 1) * tq, :] = jnp.concatenate(acc_s, axis=1).astype(o_ref.dtype)


SB_HEADS_PER_STEP = 2
SB_QBLOCKS_PER_STEP = 4
SB_SAMPLE_KEYS = 2048


def _sb_prompt(q, k, v, name):
    bsz, seq, _ = q.shape
    tq = _pick(seq, 256)
    hb = SB_HEADS_PER_STEP
    qpb = _pick(seq // tq, SB_QBLOCKS_PER_STEP)
    wid = hb * HEAD_DIM
    kv = pl.BlockSpec((None, qpb * tq * N_HEADS, HEAD_DIM), lambda b, h, i: (b, i, 0))
    qo = pl.BlockSpec((None, qpb * tq, wid), lambda b, h, i: (b, i, h))
    return pl.pallas_call(
        functools.partial(_sb_prompt_kernel, tq=tq, hb=hb, qpb=qpb, scale=HEAD_DIM ** -0.5),
        grid=(bsz, N_HEADS // hb, seq // (tq * qpb)),
        in_specs=[qo, kv, kv],
        out_specs=qo,
        out_shape=jax.ShapeDtypeStruct((bsz, seq, WIDTH), BF16),
        scratch_shapes=[pltpu.VMEM((hb, seq + tq, HEAD_DIM), BF16),
                        pltpu.VMEM((hb, seq + tq, HEAD_DIM), BF16),
                        pltpu.VMEM((hb, SUBLANES, LANES), F32)],
        compiler_params=pltpu.CompilerParams(
            dimension_semantics=("parallel", "parallel", "arbitrary"),
            vmem_limit_bytes=VMEM_LIMIT),
        name=name,
    )(q, k, v)


def _sb_sample_kernel(q_ref, kn_ref, vn_ref, kc_ref, vc_ref, o_ref, qb_ref, acc_ref, run_ref,
                      *, seq, keys, scale):
    step = pl.program_id(1)
    heads = range(N_HEADS)
    rows = N_HEADS * seq
    def sweep(k_tiles, v_tiles, mask, lr):
        z = jnp.concatenate([_dot_nt(qb_ref[h * seq:(h + 1) * seq, :], k_tiles[h]) for h in heads],
                            axis=0) * scale
        wb, run = _sb_weights(z, _sb_sums(*_sb_split(z, mask), lr), mask, run_ref[...])
        run_ref[...] = run
        pv = [_dot(wb[h * seq:(h + 1) * seq, :], v_tiles[h]) for h in heads]
        acc_ref[...] += jnp.concatenate(pv, axis=0)

    @pl.when(step == 0)
    def _():
        for h in heads:
            qb_ref[h * seq:(h + 1) * seq, :] = q_ref[:, h * HEAD_DIM:(h + 1) * HEAD_DIM].astype(BF16)
        acc_ref[...] = jnp.zeros_like(acc_ref)
        run_ref[...] = jnp.zeros_like(run_ref)
        pad = jnp.zeros((LANES - seq, HEAD_DIM), BF16)
        kn = [jnp.concatenate([kn_ref[pl.ds(h, seq, stride=N_HEADS), :].astype(BF16), pad], axis=0)
              for h in heads]
        vn = [jnp.concatenate([vn_ref[pl.ds(h, seq, stride=N_HEADS), :].astype(BF16), pad], axis=0)
              for h in heads]
        r = lax.broadcasted_iota(jnp.int32, (rows, LANES), 0)
        c = lax.broadcasted_iota(jnp.int32, (rows, LANES), 1)
        sweep(kn, vn, c < (r & (seq - 1)), _suffix_ones(LANES))

    kt = [kc_ref[pl.ds(h, keys, stride=N_HEADS), :].astype(BF16) for h in heads]
    vt = [vc_ref[pl.ds(h, keys, stride=N_HEADS), :].astype(BF16) for h in heads]
    sweep(kt, vt, None, _suffix_ones(SB_KEYS))

    @pl.when(step == pl.num_programs(1) - 1)
    def _():
        for h in heads:
            o_ref[:, h * HEAD_DIM:(h + 1) * HEAD_DIM] = acc_ref[h * seq:(h + 1) * seq, :].astype(o_ref.dtype)


def _sb_sample(q, kn, vn, kc, vc, name):
    bsz, seq, _ = q.shape
    past = kc.shape[1]
    kc = kc.reshape(bsz, past * N_HEADS, HEAD_DIM)
    vc = vc.reshape(bsz, past * N_HEADS, HEAD_DIM)
    assert seq & (seq - 1) == 0 and seq <= LANES and past % SB_KEYS == 0
    keys = _pick(past, SB_SAMPLE_KEYS)
    nsteps = past // keys
    new = pl.BlockSpec((None, seq, WIDTH), lambda b, j: (b, 0, 0))
    newkv = pl.BlockSpec((None, seq * N_HEADS, HEAD_DIM), lambda b, j: (b, 0, 0))
    old = pl.BlockSpec((None, keys * N_HEADS, HEAD_DIM), lambda b, j: (b, nsteps - 1 - j, 0))
    rows = N_HEADS * seq
    return pl.pallas_call(
        functools.partial(_sb_sample_kernel, seq=seq, keys=keys, scale=HEAD_DIM ** -0.5),
        grid=(bsz, nsteps),
        in_specs=[new, newkv, newkv, old, old],
        out_specs=new,
        out_shape=jax.ShapeDtypeStruct((bsz, seq, WIDTH), BF16),
        scratch_shapes=[pltpu.VMEM((rows, HEAD_DIM), BF16), pltpu.VMEM((rows, HEAD_DIM), F32),
                        pltpu.VMEM((rows, LANES), F32)],
        compiler_params=pltpu.CompilerParams(
            dimension_semantics=("parallel", "arbitrary"), vmem_limit_bytes=VMEM_LIMIT),
        name=name,
    )(q, kn, vn, kc, vc)


def _layer_norm(r, g, b):
    mu = jnp.mean(r, axis=-1, keepdims=True)
    d = r - mu
    var = jnp.mean(d * d, axis=-1, keepdims=True)
    return d * lax.rsqrt(var + LN_EPS) * g + b


def _outproj_kernel(oa_ref, ob_ref, x_ref, wa_ref, wb_ref, g_ref, b_ref, y_ref, *, alpha, sub):
    for s in range(y_ref.shape[0] // sub):
        rs = slice(s * sub, (s + 1) * sub)
        acc = _dot(oa_ref[rs, :], wa_ref[...]) + _dot(ob_ref[rs, :], wb_ref[...])
        y_ref[rs, :] = _layer_norm(alpha * x_ref[rs, :] + acc, g_ref[...], b_ref[...])


def _outproj_ln(oa, ob, x, wa, wb, g, b, alpha, name):
    m, d = x.shape
    tm = _pick(m, 512)
    rows = lambda w: pl.BlockSpec((tm, w), lambda i: (i, 0))
    full = lambda a: pl.BlockSpec(a.shape, lambda i: (0, 0))
    return pl.pallas_call(
        functools.partial(_outproj_kernel, alpha=alpha, sub=_pick(tm, 256)),
        grid=(m // tm,),
        in_specs=[rows(WIDTH), rows(WIDTH), rows(d), full(wa), full(wb), full(g), full(b)],
        out_specs=rows(d),
        out_shape=jax.ShapeDtypeStruct((m, d), F32),
        compiler_params=pltpu.CompilerParams(
            dimension_semantics=("parallel",), vmem_limit_bytes=VMEM_LIMIT),
        name=name,
    )(oa, ob, x, wa, wb, g, b)


FFN_TM = 512
FFN_TF = 1024


def _ffn_kernel(x_ref, wu_ref, wd_ref, g_ref, b_ref, y_ref, xb_ref, *, alpha):
    j = pl.program_id(1)

    @pl.when(j == 0)
    def _():
        xb_ref[...] = x_ref[...].astype(BF16)
        y_ref[...] = jnp.zeros_like(y_ref)

    hid = jnp.maximum(_dot(xb_ref[...], wu_ref[...]), 0.0)
    y_ref[...] += _dot((hid * hid).astype(BF16), wd_ref[...])

    @pl.when(j == pl.num_programs(1) - 1)
    def _():
        y_ref[...] = _layer_norm(alpha * x_ref[...] + y_ref[...], g_ref[...], b_ref[...])


def _ffn_ln(x, wu, wd, g, b, alpha, name):
    m, d = x.shape
    dff = wu.shape[1]
    tm = _pick(m, FFN_TM)
    tf = _pick(dff, FFN_TF)
    return pl.pallas_call(
        functools.partial(_ffn_kernel, alpha=alpha),
        grid=(m // tm, dff // tf),
        in_specs=[pl.BlockSpec((tm, d), lambda i, j: (i, 0)),
                  pl.BlockSpec((d, tf), lambda i, j: (0, j)),
                  pl.BlockSpec((tf, d), lambda i, j: (j, 0)),
                  pl.BlockSpec((1, d), lambda i, j: (0, 0)),
                  pl.BlockSpec((1, d), lambda i, j: (0, 0))],
        out_specs=pl.BlockSpec((tm, d), lambda i, j: (i, 0)),
        out_shape=jax.ShapeDtypeStruct((m, d), F32),
        scratch_shapes=[pltpu.VMEM((tm, d), BF16)],
        compiler_params=pltpu.CompilerParams(
            dimension_semantics=("parallel", "arbitrary"), vmem_limit_bytes=VMEM_LIMIT),
        name=name,
    )(x, wu, wd, g, b)


def _layer(x, conv_buf, s0, k_past, v_past, wts, alpha, tag):
    (w_main, w_gate, conv_w, alog_row, dtb_row, nw_row,
     wo_a, wo_b, ln1_g, ln1_b, w_up, w_down, ln2_g, ln2_b) = wts
    bsz, seq, d = x.shape
    m = bsz * seq
    x2 = x.reshape(m, d)

    pre, ba, q_b, k_out, v_out = _proj(x2, w_main, w_gate, "proj_" + tag)
    pre = pre.reshape(bsz, seq, 4 * WIDTH)
    ba = ba.reshape(bsz, seq, LANES)
    q_b = q_b.reshape(bsz, seq, WIDTH)
    k_b = k_out.reshape(bsz, seq * N_HEADS, HEAD_DIM)
    v_b = v_out.reshape(bsz, seq * N_HEADS, HEAD_DIM)

    conv8 = jnp.pad(conv_buf, ((0, 0), (SUBLANES - (CONV_W - 1), 0), (0, 0)))
    o_a, s_new = _gdn(pre, ba, conv8, s0, conv_w, alog_row, dtb_row, nw_row, "gdn_" + tag)
    keep = CONV_W - 1
    conv_new = pre[:, seq - keep:, :3 * WIDTH] if seq >= keep else jnp.concatenate(
        [conv_buf, pre[:, :, :3 * WIDTH]], axis=1)[:, -keep:, :]

    if k_past is None:
        o_b = _sb_prompt(q_b, k_b, v_b, "sb_" + tag)
    else:
        o_b = _sb_sample(q_b, k_b, v_b, k_past, v_past, "sb_" + tag)

    x1 = _outproj_ln(o_a.reshape(m, WIDTH), o_b.reshape(m, WIDTH), x2, wo_a, wo_b,
                     ln1_g, ln1_b, alpha, "outproj_" + tag)
    y = _ffn_ln(x1, w_up, w_down, ln2_g, ln2_b, alpha, "ffn_" + tag)
    return (y.reshape(bsz, seq, d), conv_new, s_new,
            k_out.reshape(bsz, seq, N_HEADS, HEAD_DIM), v_out.reshape(bsz, seq, N_HEADS, HEAD_DIM))


def kernel(x_prompt, x_sample, state_gdn_conv, state_gdn_S, cache_sb_k, cache_sb_v, w_in, conv_w, a_log, dt_bias, gdn_norm_w, w_out, ln1_g, ln1_b, w_up, w_down, ln2_g, ln2_b):
    depth = w_in.shape[0]
    alpha = (2 * depth) ** 0.25
    off_b = 4 * WIDTH
    off_sb = off_b + 2 * N_HEADS
    yp, ys = x_prompt, x_sample
    outs = [[] for _ in range(8)]
    for l in range(depth):
        wl = w_in[l]
        wl = wl.astype(BF16)
        w_gate = jnp.pad(wl[:, off_b:off_sb], ((0, 0), (0, LANES - 2 * N_HEADS)))
        w_main = (wl, wl[:, off_sb:])
        pad_row = lambda v: jnp.pad(v.astype(F32), (N_HEADS, LANES - 2 * N_HEADS)).reshape(1, LANES)
        wts = (w_main, w_gate,
               conv_w[l], pad_row(a_log[l]), pad_row(dt_bias[l]),
               gdn_norm_w[l].astype(F32).reshape(1, HEAD_DIM),
               w_out[l, :WIDTH].astype(BF16), w_out[l, WIDTH:].astype(BF16),
               ln1_g[l].reshape(1, -1), ln1_b[l].reshape(1, -1),
               w_up[l].astype(BF16), w_down[l].astype(BF16),
               ln2_g[l].reshape(1, -1), ln2_b[l].reshape(1, -1))
        bp = yp.shape[0]
        zero_conv = jnp.zeros((bp, CONV_W - 1, 3 * WIDTH), F32)
        zero_s = jnp.zeros((bp, N_HEADS, HEAD_DIM, HEAD_DIM), F32)
        yp, c1, s1, k1, v1 = _layer(yp, zero_conv, zero_s, None, None, wts, alpha, "p%d" % l)
        ys, c2, s2, k2, v2 = _layer(ys, state_gdn_conv[l], state_gdn_S[l],
                                    cache_sb_k[l], cache_sb_v[l], wts, alpha, "s%d" % l)
        for lst, val in zip(outs, (c1, s1, k1, v1, c2, s2, k2, v2)):
            lst.append(val)
    return (yp, ys) + tuple(jnp.stack(o) for o in outs)
```

```python
import functools

import jax
import jax.numpy as jnp
from jax import lax
from jax.experimental import pallas as pl
from jax.experimental.pallas import tpu as pltpu

HEAD_DIM = 128
N_HEADS = 8
WIDTH = N_HEADS * HEAD_DIM
CONV_W = 4
CHUNK = 128
LN_EPS = 1e-5
RMS_EPS = 1e-6
L2_EPS = 1e-6
LANES = 128
SUBLANES = 8
VMEM_LIMIT = 56 * 1024 * 1024

F32 = jnp.float32
BF16 = jnp.bfloat16


def _pick(n, pref):
    if n <= pref:
        return n
    t = pref
    while n % t:
        t //= 2
    return t


def _dot(a, b):
    return jnp.dot(a, b, preferred_element_type=F32)


def _dot_nt(a, b):
    return lax.dot_general(a, b, (((1,), (1,)), ((), ())), preferred_element_type=F32)


def _sigmoid(x):
    return 1.0 / (1.0 + jnp.exp(-x))


def _softplus(x):
    return jnp.maximum(x, 0.0) + jnp.log1p(jnp.exp(-jnp.abs(x)))


PROJ_TM = 1024
PROJ_TN = 512
N_PRE_TILES = 4 * WIDTH // PROJ_TN
N_SB_TILES = WIDTH // PROJ_TN
HEADS_PER_TILE = PROJ_TN // HEAD_DIM


def _proj_kernel(x_ref, wa_ref, wsb_ref, wg_ref, pre_ref, ba_ref, q_ref, kout_ref, vout_ref,
                 xb_ref, *, tm):
    j = pl.program_id(1)

    @pl.when(j == 0)
    def _():
        xb_ref[...] = x_ref[...].astype(BF16)
        ba_ref[...] = _dot(xb_ref[...], wg_ref[...])

    @pl.when(j < N_PRE_TILES)
    def _():
        pre_ref[...] = _dot(xb_ref[...], wa_ref[...])

    @pl.when((j >= N_PRE_TILES) & (j < N_PRE_TILES + N_SB_TILES))
    def _():
        q_ref[...] = _dot(xb_ref[...], wsb_ref[...]).astype(BF16)

    def keep(out_ref, first_head):
        res = _dot(xb_ref[...], wsb_ref[...])
        for h in range(HEADS_PER_TILE):
            out_ref[pl.ds(first_head + h, tm, stride=N_HEADS), :] = res[:, h * HEAD_DIM:(h + 1) * HEAD_DIM]

    for t in range(N_SB_TILES):
        pl.when(j == N_PRE_TILES + N_SB_TILES + t)(
            functools.partial(keep, kout_ref, t * HEADS_PER_TILE))
        pl.when(j == N_PRE_TILES + 2 * N_SB_TILES + t)(
            functools.partial(keep, vout_ref, t * HEADS_PER_TILE))


def _proj(x, w_main, w_gate, name):
    w_in, w_sb = w_main
    m, k = x.shape
    tm = _pick(m, PROJ_TM)
    nt = N_PRE_TILES + 3 * N_SB_TILES
    row = lambda w: pl.BlockSpec((tm, w), lambda i, j: (i, 0))
    return pl.pallas_call(
        functools.partial(_proj_kernel, tm=tm),
        grid=(m // tm, nt),
        in_specs=[row(k),
                  pl.BlockSpec((k, PROJ_TN), lambda i, j: (0, jnp.minimum(j, N_PRE_TILES - 1))),
                  pl.BlockSpec((k, PROJ_TN), lambda i, j: (0, jnp.maximum(j - N_PRE_TILES, 0))),
                  pl.BlockSpec((k, LANES), lambda i, j: (0, 0))],
        out_specs=[pl.BlockSpec((tm, PROJ_TN), lambda i, j: (i, jnp.minimum(j, N_PRE_TILES - 1))),
                   row(LANES),
                   pl.BlockSpec((tm, PROJ_TN),
                                lambda i, j: (i, jnp.clip(j - N_PRE_TILES, 0, N_SB_TILES - 1))),
                   pl.BlockSpec((tm * N_HEADS, HEAD_DIM), lambda i, j: (i, 0)),
                   pl.BlockSpec((tm * N_HEADS, HEAD_DIM), lambda i, j: (i, 0))],
        out_shape=[jax.ShapeDtypeStruct((m, 4 * WIDTH), F32),
                   jax.ShapeDtypeStruct((m, LANES), F32),
                   jax.ShapeDtypeStruct((m, WIDTH), BF16),
                   jax.ShapeDtypeStruct((m * N_HEADS, HEAD_DIM), F32),
                   jax.ShapeDtypeStruct((m * N_HEADS, HEAD_DIM), F32)],
        scratch_shapes=[pltpu.VMEM((tm, k), BF16)],
        compiler_params=pltpu.CompilerParams(
            dimension_semantics=("parallel", "arbitrary"), vmem_limit_bytes=VMEM_LIMIT),
        name=name,
    )(x, w_in, w_sb, w_gate)


NN = (((1,), (0,)), ((), ()))
NT = (((1,), (1,)), ((), ()))
TN = (((0,), (0,)), ((), ()))

GDN_ROWS_PER_STEP = 512
GDN_PIECES = dict(qk=1, inv=1, uw=1, scan=1)


def _pieces(x, n):
    out = []
    r = x
    for i in range(n):
        p = r.astype(BF16)
        out.append(p)
        if i + 1 < n:
            r = r - p.astype(F32)
    return out


def _mm(a, b, dims=NN):
    order = max(len(a), len(b))
    acc = None
    for i, ai in enumerate(a):
        for j, bj in enumerate(b):
            if i + j < order:
                t = lax.dot_general(ai, bj, dims, preferred_element_type=F32)
                acc = t if acc is None else acc + t
    return acc


def _gdn_kernel(q_ref, k_ref, v_ref, z_ref, ba_ref, cq_ref, ck_ref, cv_ref,
                wq_ref, wk_ref, wv_ref, s0_ref, alog_ref, dtb_ref, nw_ref,
                o_ref, s_ref, tail_ref, *, nb, rows, chunk):
    C = chunk
    probs = [(bi, h) for bi in range(nb) for h in range(N_HEADS)]
    P = range(len(probs))
    row = lax.broadcasted_iota(jnp.int32, (C, C), 0)
    col = lax.broadcasted_iota(jnp.int32, (C, C), 1)
    incl = row >= col
    strict = row > col
    l_incl = [incl.astype(BF16)]
    eye = (row == col).astype(F32)
    lane = lax.broadcasted_iota(jnp.int32, (C, LANES), 1)
    sel_r = lax.broadcasted_iota(jnp.int32, (LANES, LANES), 0)
    sel_col = [[(sel_r == h + N_HEADS).astype(BF16)] for h in range(N_HEADS)]
    sel_row = [[(lane == h + N_HEADS).astype(BF16)] for h in range(N_HEADS)]
    level_masks = []
    b = 1
    while b < C:
        s = b.bit_length() - 1
        same = (row >> (s + 1)) == (col >> (s + 1))
        level_masks.append(same & (((row >> s) & 1) == 1) & (((col >> s) & 1) == 0))
        b *= 2
    n_qk, n_inv, n_uw, n_scan = (GDN_PIECES[c] for c in ("qk", "inv", "uw", "scan"))

    neg_a = -jnp.exp(alog_ref[...])
    dtb = dtb_ref[...]
    nw = nw_ref[...]
    wq = wq_ref[...]
    wk = wk_ref[...]
    wv = wv_ref[...]

    @pl.when(pl.program_id(1) == 0)
    def _():
        s_ref[...] = s0_ref[...]
        tail_ref[0] = cq_ref[...]
        tail_ref[1] = ck_ref[...]
        tail_ref[2] = cv_ref[...]

    def conv(e, w):
        first = SUBLANES - (CONV_W - 1)
        acc = e[first:first + C] * w[0:1]
        for i in range(1, CONV_W):
            acc = acc + e[first + i:first + i + C] * w[i:i + 1]
        return acc * _sigmoid(acc)

    def head(x, i):
        return x[:, i * HEAD_DIM:(i + 1) * HEAD_DIM]

    def body(n, carry):
        tq, tk, tv = carry
        c0 = pl.multiple_of(n * C, C)
        xq = [q_ref[bi, pl.ds(c0, C), :] for bi in range(nb)]
        xk = [k_ref[bi, pl.ds(c0, C), :] for bi in range(nb)]
        xv = [v_ref[bi, pl.ds(c0, C), :] for bi in range(nb)]
        q_all = [conv(jnp.concatenate([tq[bi], xq[bi]], axis=0), wq) for bi in range(nb)]
        k_all = [conv(jnp.concatenate([tk[bi], xk[bi]], axis=0), wk) for bi in range(nb)]
        v_all = [conv(jnp.concatenate([tv[bi], xv[bi]], axis=0), wv) for bi in range(nb)]
        gate_all = []
        beta_all = []
        g_all = []
        for bi in range(nb):
            zc = z_ref[bi, pl.ds(c0, C), :]
            gate_all.append(zc * _sigmoid(zc))
            ba = ba_ref[bi, pl.ds(c0, C), :]
            beta_all.append(_sigmoid(ba))
            g_all.append(neg_a * _softplus(ba + dtb))

        q = [head(q_all[bi], h) for bi, h in probs]
        k = [head(k_all[bi], h) for bi, h in probs]
        q = [x * lax.rsqrt(jnp.sum(x * x, axis=-1, keepdims=True) + L2_EPS) * (HEAD_DIM ** -0.5)
             for x in q]
        k = [x * lax.rsqrt(jnp.sum(x * x, axis=-1, keepdims=True) + L2_EPS) for x in k]
        beta = [jnp.sum(jnp.where(lane == h, beta_all[bi], 0.0), axis=-1, keepdims=True)
                for bi, h in probs]
        k_beta = [k[i] * beta[i] for i in P]
        kp = [_pieces(k[i], n_qk) for i in P]

        gc_all = [_pieces(_mm(l_incl, _pieces(g_all[bi], 3)), 3) for bi in range(nb)]
        gcb = [_mm(gc_all[bi], sel_col[h]) for bi, h in probs]
        gc_row = [_mm(sel_row[h], gc_all[bi], NT) for bi, h in probs]
        kk = [_mm(_pieces(k_beta[i], n_qk), kp[i], NT) for i in P]
        qk = [_mm(_pieces(q[i], n_qk), kp[i], NT) for i in P]

        decay = [jnp.where(incl, jnp.exp(gcb[i][:, :C] - gc_row[i]), 0.0) for i in P]
        egc = [jnp.exp(gcb[i]) for i in P]
        g_last = [gcb[i][C - 1:C, :] for i in P]
        m = [jnp.where(strict, kk[i] * decay[i], 0.0) for i in P]
        attn = [qk[i] * decay[i] for i in P]

        t_inv = [eye - jnp.where(level_masks[0], m[i], 0.0) for i in P]
        for mask in level_masks[1:]:
            tp = [_pieces(t_inv[i], n_inv) for i in P]
            x = [_mm(tp[i], _pieces(jnp.where(mask, m[i], 0.0), n_inv)) for i in P]
            y = [_mm(_pieces(x[i], n_inv), tp[i]) for i in P]
            t_inv = [t_inv[i] - y[i] for i in P]

        rhs = [jnp.concatenate([head(v_all[probs[i][0]], probs[i][1]) * beta[i], k_beta[i] * egc[i]],
                               axis=1) for i in P]
        uw = [_mm(_pieces(t_inv[i], n_uw), _pieces(rhs[i], n_uw)) for i in P]

        s_old = [s_ref[bi, h] for bi, h in probs]
        lhs = [jnp.concatenate([uw[i][:, HEAD_DIM:], q[i] * egc[i]], axis=0) for i in P]
        ws = [_mm(_pieces(lhs[i], n_scan), _pieces(s_old[i], n_scan)) for i in P]
        v_new = [uw[i][:, :HEAD_DIM] - ws[i][:C] for i in P]
        vp = [_pieces(v_new[i], n_scan) for i in P]
        k_tail = [k[i] * jnp.exp(g_last[i] - gcb[i]) for i in P]

        o = [ws[i][C:] + _mm(_pieces(attn[i], n_scan), vp[i]) for i in P]
        s_new = [s_old[i] * jnp.exp(g_last[i]) + _mm(_pieces(k_tail[i], n_scan), vp[i], TN)
                 for i in P]
        for i, (bi, h) in enumerate(probs):
            s_ref[bi, h] = s_new[i]
            on = (o[i] * lax.rsqrt(jnp.mean(o[i] * o[i], axis=-1, keepdims=True) + RMS_EPS)
                  * nw * head(gate_all[bi], h))
            o_ref[bi, pl.ds(c0, C), h * HEAD_DIM:(h + 1) * HEAD_DIM] = on.astype(o_ref.dtype)
        last = lambda xs: jnp.stack([x[C - 8:C] for x in xs])
        return last(xq), last(xk), last(xv)

    tails = lax.fori_loop(0, rows // C, body, (tail_ref[0], tail_ref[1], tail_ref[2]))
    for j in range(3):
        tail_ref[j] = tails[j]


GDN_BATCH_PER_STEP = 2
GDN_BATCH_SHORT = 4


def _gdn(pre, ba, conv8, s0, conv_w, alog_row, dtb_row, nw_row, name):
    bsz, seq, _ = pre.shape
    chunk = CHUNK if seq % CHUNK == 0 else seq
    rows = _pick(seq, GDN_ROWS_PER_STEP)
    nb = _pick(bsz, GDN_BATCH_PER_STEP if seq > chunk else GDN_BATCH_SHORT)

    def col(grp):
        return pl.BlockSpec((nb, rows, WIDTH), lambda b, t: (b, t, grp))

    def st(grp):
        return pl.BlockSpec((nb, SUBLANES, WIDTH), lambda b, t: (b, 0, grp))

    def cw(grp):
        return pl.BlockSpec((CONV_W, WIDTH), lambda b, t: (0, grp))

    row = pl.BlockSpec((1, LANES), lambda b, t: (0, 0))
    state = pl.BlockSpec((nb, N_HEADS, HEAD_DIM, HEAD_DIM), lambda b, t: (b, 0, 0, 0))
    return pl.pallas_call(
        functools.partial(_gdn_kernel, nb=nb, rows=rows, chunk=chunk),
        grid=(bsz // nb, seq // rows),
        in_specs=[col(0), col(1), col(2), col(3),
                  pl.BlockSpec((nb, rows, LANES), lambda b, t: (b, t, 0)),
                  st(0), st(1), st(2), cw(0), cw(1), cw(2), state, row, row, row],
        out_specs=[pl.BlockSpec((nb, rows, WIDTH), lambda b, t: (b, t, 0)), state],
        out_shape=[jax.ShapeDtypeStruct((bsz, seq, WIDTH), BF16),
                   jax.ShapeDtypeStruct((bsz, N_HEADS, HEAD_DIM, HEAD_DIM), F32)],
        scratch_shapes=[pltpu.VMEM((3, nb, SUBLANES, WIDTH), F32)],
        compiler_params=pltpu.CompilerParams(
            dimension_semantics=("parallel", "arbitrary"), vmem_limit_bytes=VMEM_LIMIT),
        name=name,
    )(pre, pre, pre, pre, ba, conv8, conv8, conv8, conv_w, conv_w, conv_w, s0,
      alog_row, dtb_row, nw_row)


SB_KEYS = 256
SB_DEAD = -104.0
SB_BOUND_SLACK = 1.05


def _suffix_ones(n):
    r = lax.broadcasted_iota(jnp.int32, (n, n), 0)
    c = lax.broadcasted_iota(jnp.int32, (n, n), 1)
    return (r >= c).astype(BF16)


def _sb_split(z, mask):
    l1m = -(jnp.maximum(z, 0.0) + jnp.log(1.0 + jnp.exp(-jnp.abs(z))))
    if mask is not None:
        l1m = jnp.where(mask, l1m, 0.0)
    hi = l1m.astype(BF16)
    return hi, (l1m - hi.astype(F32)).astype(BF16)


def _sb_sums(hi, lo, suffix):
    tk = suffix.shape[0]
    tiles = [slice(s * tk, (s + 1) * tk) for s in range(hi.shape[1] // tk)]
    return [_dot(hi[:, t], suffix) + _dot(lo[:, t], suffix) for t in tiles]


def _sb_weights(z, cs, mask, run):
    n = len(cs)
    reps = cs[0].shape[1] // LANES
    tot = [None] * n
    for s in reversed(range(n)):
        tot[s] = run if reps == 1 else jnp.concatenate([run] * reps, axis=1)
        run = run + jnp.broadcast_to(cs[s][:, 0:1], run.shape)
    off = [cs[s] + tot[s] for s in range(n)]
    w = jnp.exp(z + (off[0] if n == 1 else jnp.concatenate(off, axis=1)))
    if mask is not None:
        w = jnp.where(mask, w, 0.0)
    return w.astype(BF16), run


def _sb_prompt_kernel(q_ref, k_ref, v_ref, o_ref, kb_ref, vb_ref, kmax_ref, *, tq, hb, qpb, scale):
    step = pl.program_id(2)
    qi0 = step * qpb
    heads = range(hb)
    probs = [(sub, h) for sub in range(qpb) for h in heads]
    ones = jnp.ones((HEAD_DIM, LANES), BF16)

    @pl.when(step == 0)
    def _():
        zeros = jnp.zeros((tq, HEAD_DIM), BF16)
        for h in heads:
            kb_ref[h, 0:tq, :] = zeros
            vb_ref[h, 0:tq, :] = zeros
            kmax_ref[h] = jnp.zeros(kmax_ref.shape[1:], F32)

    nk = qpb * tq
    dst = pl.multiple_of((qi0 + 1) * tq, tq)
    for h in heads:
        first = pl.program_id(1) * hb + h
        kh = k_ref[pl.ds(first, nk, stride=N_HEADS), :]
        kb_ref[h, pl.ds(dst, nk), :] = kh.astype(BF16)
        vb_ref[h, pl.ds(dst, nk), :] = v_ref[pl.ds(first, nk, stride=N_HEADS), :].astype(BF16)
        ksq = _dot((kh * kh).astype(BF16), ones)
        kmax_ref[h] = jnp.maximum(kmax_ref[h], jnp.broadcast_to(
            jnp.max(ksq, axis=0, keepdims=True), kmax_ref.shape[1:]))

    q = q_ref[...]
    qf = [q[sub * tq:(sub + 1) * tq, h * HEAD_DIM:(h + 1) * HEAD_DIM] for sub, h in probs]
    qb = [x.astype(BF16) for x in qf]
    zbound = [jnp.sqrt(_dot((x * x).astype(BF16), ones) * kmax_ref[h, 0:1, :])
              * (scale * SB_BOUND_SLACK) for x, (_, h) in zip(qf, probs)]
    lr = _suffix_ones(SB_KEYS)

    def sweep(ps, row0, nkeys, mask, acc, run):
        n = range(len(ps))
        hd = [probs[p][1] for p in ps]
        z = [_dot_nt(qb[ps[i]], kb_ref[hd[i], pl.ds(row0[i], nkeys), :]) * scale for i in n]
        pieces = [_sb_split(z[i], mask[i]) for i in n]
        ct = [_sb_sums(*pieces[i], lr) for i in n]
        wr = [_sb_weights(z[i], ct[i], mask[i], run[i]) for i in n]
        acc = [acc[i] + _dot(wr[i][0], vb_ref[hd[i], pl.ds(row0[i], nkeys), :]) for i in n]
        return acc, [wr[i][1] for i in n]

    r = lax.broadcasted_iota(jnp.int32, (tq, 2 * tq), 0)
    c = lax.broadcasted_iota(jnp.int32, (tq, 2 * tq), 1)
    every = list(range(len(probs)))
    acc, run = sweep(
        every,
        [pl.multiple_of((qi0 + sub) * tq, tq) for sub, _ in probs], 2 * tq,
        [(c < r + tq) & (c + (qi0 + sub) * tq >= tq) for sub, _ in probs],
        [jnp.zeros((tq, HEAD_DIM), F32) for _ in probs],
        [jnp.zeros((tq, LANES), F32) for _ in probs])

    for sub in range(qpb):
        ps = [p for p in every if probs[p][0] == sub]

        def alive(run, ps=ps):
            live = jnp.max(run[0] + zbound[ps[0]]) > SB_DEAD
            for i in range(1, len(ps)):
                live = live | (jnp.max(run[i] + zbound[ps[i]]) > SB_DEAD)
            return live

        def cond(carry):
            return (carry[0] >= 0) & carry[3]

        def body(carry, ps=ps, alive=alive):
            j, acc, run, _ = carry
            row0 = pl.multiple_of((j + 1) * tq, tq)
            acc, run = sweep(ps, [row0] * len(ps), tq, [None] * len(ps), acc, run)
            return j - 1, acc, run, alive(run)

        acc_s = [acc[p] for p in ps]
        run_s = [run[p] for p in ps]
        _, acc_s, _, _ = lax.while_loop(cond, body, (qi0 + sub - 2, acc_s, run_s, alive(run_s)))
        o_ref[sub * tq:(sub + 1) * tq, :] = jnp.concatenate(acc_s, axis=1).astype(o_ref.dtype)


SB_HEADS_PER_STEP = 2
SB_QBLOCKS_PER_STEP = 4
SB_SAMPLE_KEYS = 2048
SB_SAMPLE_PARTS = 4


def _sb_prompt(q, k, v, name):
    bsz, seq, _ = q.shape
    tq = _pick(seq, 256)
    hb = SB_HEADS_PER_STEP
    qpb = _pick(seq // tq, SB_QBLOCKS_PER_STEP)
    wid = hb * HEAD_DIM
    kv = pl.BlockSpec((None, qpb * tq * N_HEADS, HEAD_DIM), lambda b, h, i: (b, i, 0))
    qo = pl.BlockSpec((None, qpb * tq, wid), lambda b, h, i: (b, i, h))
    return pl.pallas_call(
        functools.partial(_sb_prompt_kernel, tq=tq, hb=hb, qpb=qpb, scale=HEAD_DIM ** -0.5),
        grid=(bsz, N_HEADS // hb, seq // (tq * qpb)),
        in_specs=[qo, kv, kv],
        out_specs=qo,
        out_shape=jax.ShapeDtypeStruct((bsz, seq, WIDTH), BF16),
        scratch_shapes=[pltpu.VMEM((hb, seq + tq, HEAD_DIM), BF16),
                        pltpu.VMEM((hb, seq + tq, HEAD_DIM), BF16),
                        pltpu.VMEM((hb, SUBLANES, LANES), F32)],
        compiler_params=pltpu.CompilerParams(
            dimension_semantics=("parallel", "parallel", "arbitrary"),
            vmem_limit_bytes=VMEM_LIMIT),
        name=name,
    )(q, k, v)


def _sb_sample_kernel(q_ref, kn_ref, vn_ref, kc_ref, vc_ref, o_ref, qb_ref, acc_ref, run_ref,
                      *, seq, keys, scale):
    step = pl.program_id(1)
    heads = range(N_HEADS)
    rows = N_HEADS * seq
    def sweep(k_tiles, v_tiles, mask, lr):
        z = jnp.concatenate([_dot_nt(qb_ref[h * seq:(h + 1) * seq, :], k_tiles[h]) for h in heads],
                            axis=0) * scale
        wb, run = _sb_weights(z, _sb_sums(*_sb_split(z, mask), lr), mask, run_ref[...])
        run_ref[...] = run
        pv = [_dot(wb[h * seq:(h + 1) * seq, :], v_tiles[h]) for h in heads]
        acc_ref[...] += jnp.concatenate(pv, axis=0)

    @pl.when(step == 0)
    def _():
        for h in heads:
            qb_ref[h * seq:(h + 1) * seq, :] = q_ref[:, h * HEAD_DIM:(h + 1) * HEAD_DIM].astype(BF16)
        acc_ref[...] = jnp.zeros_like(acc_ref)
        run_ref[...] = jnp.zeros_like(run_ref)
        pad = jnp.zeros((LANES - seq, HEAD_DIM), BF16)
        kn = [jnp.concatenate([kn_ref[pl.ds(h, seq, stride=N_HEADS), :].astype(BF16), pad], axis=0)
              for h in heads]
        vn = [jnp.concatenate([vn_ref[pl.ds(h, seq, stride=N_HEADS), :].astype(BF16), pad], axis=0)
              for h in heads]
        r = lax.broadcasted_iota(jnp.int32, (rows, LANES), 0)
        c = lax.broadcasted_iota(jnp.int32, (rows, LANES), 1)
        sweep(kn, vn, c < (r & (seq - 1)), _suffix_ones(LANES))

    parts = _pick(keys // SB_KEYS, SB_SAMPLE_PARTS)
    part = keys // parts
    for s in reversed(range(parts)):
        kt = [kc_ref[pl.ds(s * part * N_HEADS + h, part, stride=N_HEADS), :].astype(BF16)
              for h in heads]
        vt = [vc_ref[pl.ds(s * part * N_HEADS + h, part, stride=N_HEADS), :].astype(BF16)
              for h in heads]
        sweep(kt, vt, None, _suffix_ones(SB_KEYS))

    @pl.when(step == pl.num_programs(1) - 1)
    def _():
        for h in heads:
            o_ref[:, h * HEAD_DIM:(h + 1) * HEAD_DIM] = acc_ref[h * seq:(h + 1) * seq, :].astype(o_ref.dtype)


def _sb_sample(q, kn, vn, kc, vc, name):
    bsz, seq, _ = q.shape
    past = kc.shape[1]
    kc = kc.reshape(bsz, past * N_HEADS, HEAD_DIM)
    vc = vc.reshape(bsz, past * N_HEADS, HEAD_DIM)
    assert seq & (seq - 1) == 0 and seq <= LANES and past % SB_KEYS == 0
    keys = _pick(past, SB_SAMPLE_KEYS)
    nsteps = past // keys
    new = pl.BlockSpec((None, seq, WIDTH), lambda b, j: (b, 0, 0))
    newkv = pl.BlockSpec((None, seq * N_HEADS, HEAD_DIM), lambda b, j: (b, 0, 0))
    old = pl.BlockSpec((None, keys * N_HEADS, HEAD_DIM), lambda b, j: (b, nsteps - 1 - j, 0))
    rows = N_HEADS * seq
    return pl.pallas_call(
        functools.partial(_sb_sample_kernel, seq=seq, keys=keys, scale=HEAD_DIM ** -0.5),
        grid=(bsz, nsteps),
        in_specs=[new, newkv, newkv, old, old],
        out_specs=new,
        out_shape=jax.ShapeDtypeStruct((bsz, seq, WIDTH), BF16),
        scratch_shapes=[pltpu.VMEM((rows, HEAD_DIM), BF16), pltpu.VMEM((rows, HEAD_DIM), F32),
                        pltpu.VMEM((rows, LANES), F32)],
        compiler_params=pltpu.CompilerParams(
            dimension_semantics=("parallel", "arbitrary"), vmem_limit_bytes=VMEM_LIMIT),
        name=name,
    )(q, kn, vn, kc, vc)


def _layer_norm(r, g, b):
    mu = jnp.mean(r, axis=-1, keepdims=True)
    d = r - mu
    var = jnp.mean(d * d, axis=-1, keepdims=True)
    return d * lax.rsqrt(var + LN_EPS) * g + b


def _outproj_kernel(oa_ref, ob_ref, x_ref, wa_ref, wb_ref, g_ref, b_ref, y_ref, *, alpha, sub):
    for s in range(y_ref.shape[0] // sub):
        rs = slice(s * sub, (s + 1) * sub)
        acc = _dot(oa_ref[rs, :], wa_ref[...]) + _dot(ob_ref[rs, :], wb_ref[...])
        y_ref[rs, :] = _layer_norm(alpha * x_ref[rs, :] + acc, g_ref[...], b_ref[...])


def _outproj_ln(oa, ob, x, wa, wb, g, b, alpha, name):
    m, d = x.shape
    tm = _pick(m, 512)
    rows = lambda w: pl.BlockSpec((tm, w), lambda i: (i, 0))
    full = lambda a: pl.BlockSpec(a.shape, lambda i: (0, 0))
    return pl.pallas_call(
        functools.partial(_outproj_kernel, alpha=alpha, sub=_pick(tm, 256)),
        grid=(m // tm,),
        in_specs=[rows(WIDTH), rows(WIDTH), rows(d), full(wa), full(wb), full(g), full(b)],
        out_specs=rows(d),
        out_shape=jax.ShapeDtypeStruct((m, d), F32),
        compiler_params=pltpu.CompilerParams(
            dimension_semantics=("parallel",), vmem_limit_bytes=VMEM_LIMIT),
        name=name,
    )(oa, ob, x, wa, wb, g, b)


FFN_TM = 512
FFN_TF = 1024


def _ffn_kernel(x_ref, wu_ref, wd_ref, g_ref, b_ref, y_ref, xb_ref, *, alpha):
    j = pl.program_id(1)

    @pl.when(j == 0)
    def _():
        xb_ref[...] = x_ref[...].astype(BF16)
        y_ref[...] = jnp.zeros_like(y_ref)

    hid = jnp.maximum(_dot(xb_ref[...], wu_ref[...]), 0.0)
    y_ref[...] += _dot((hid * hid).astype(BF16), wd_ref[...])

    @pl.when(j == pl.num_programs(1) - 1)
    def _():
        y_ref[...] = _layer_norm(alpha * x_ref[...] + y_ref[...], g_ref[...], b_ref[...])


def _ffn_ln(x, wu, wd, g, b, alpha, name):
    m, d = x.shape
    dff = wu.shape[1]
    tm = _pick(m, FFN_TM)
    tf = _pick(dff, FFN_TF)
    return pl.pallas_call(
        functools.partial(_ffn_kernel, alpha=alpha),
        grid=(m // tm, dff // tf),
        in_specs=[pl.BlockSpec((tm, d), lambda i, j: (i, 0)),
                  pl.BlockSpec((d, tf), lambda i, j: (0, j)),
                  pl.BlockSpec((tf, d), lambda i, j: (j, 0)),
                  pl.BlockSpec((1, d), lambda i, j: (0, 0)),
                  pl.BlockSpec((1, d), lambda i, j: (0, 0))],
        out_specs=pl.BlockSpec((tm, d), lambda i, j: (i, 0)),
        out_shape=jax.ShapeDtypeStruct((m, d), F32),
        scratch_shapes=[pltpu.VMEM((tm, d), BF16)],
        compiler_params=pltpu.CompilerParams(
            dimension_semantics=("parallel", "arbitrary"), vmem_limit_bytes=VMEM_LIMIT),
        name=name,
    )(x, wu, wd, g, b)


def _layer(x, conv_buf, s0, k_past, v_past, wts, alpha, tag):
    (w_main, w_gate, conv_w, alog_row, dtb_row, nw_row,
     wo_a, wo_b, ln1_g, ln1_b, w_up, w_down, ln2_g, ln2_b) = wts
    bsz, seq, d = x.shape
    m = bsz * seq
    x2 = x.reshape(m, d)

    pre, ba, q_b, k_out, v_out = _proj(x2, w_main, w_gate, "proj_" + tag)
    pre = pre.reshape(bsz, seq, 4 * WIDTH)
    ba = ba.reshape(bsz, seq, LANES)
    q_b = q_b.reshape(bsz, seq, WIDTH)
    k_b = k_out.reshape(bsz, seq * N_HEADS, HEAD_DIM)
    v_b = v_out.reshape(bsz, seq * N_HEADS, HEAD_DIM)

    conv8 = jnp.pad(conv_buf, ((0, 0), (SUBLANES - (CONV_W - 1), 0), (0, 0)))
    o_a, s_new = _gdn(pre, ba, conv8, s0, conv_w, alog_row, dtb_row, nw_row, "gdn_" + tag)
    keep = CONV_W - 1
    conv_new = pre[:, seq - keep:, :3 * WIDTH] if seq >= keep else jnp.concatenate(
        [conv_buf, pre[:, :, :3 * WIDTH]], axis=1)[:, -keep:, :]

    if k_past is None:
        o_b = _sb_prompt(q_b, k_b, v_b, "sb_" + tag)
    else:
        o_b = _sb_sample(q_b, k_b, v_b, k_past, v_past, "sb_" + tag)

    x1 = _outproj_ln(o_a.reshape(m, WIDTH), o_b.reshape(m, WIDTH), x2, wo_a, wo_b,
                     ln1_g, ln1_b, alpha, "outproj_" + tag)
    y = _ffn_ln(x1, w_up, w_down, ln2_g, ln2_b, alpha, "ffn_" + tag)
    return (y.reshape(bsz, seq, d), conv_new, s_new,
            k_out.reshape(bsz, seq, N_HEADS, HEAD_DIM), v_out.reshape(bsz, seq, N_HEADS, HEAD_DIM))


def kernel(x_prompt, x_sample, state_gdn_conv, state_gdn_S, cache_sb_k, cache_sb_v, w_in, conv_w, a_log, dt_bias, gdn_norm_w, w_out, ln1_g, ln1_b, w_up, w_down, ln2_g, ln2_b):
    depth = w_in.shape[0]
    alpha = (2 * depth) ** 0.25
    off_b = 4 * WIDTH
    off_sb = off_b + 2 * N_HEADS
    yp, ys = x_prompt, x_sample
    outs = [[] for _ in range(8)]
    for l in range(depth):
        wl = w_in[l]
        wl = wl.astype(BF16)
        w_gate = jnp.pad(wl[:, off_b:off_sb], ((0, 0), (0, LANES - 2 * N_HEADS)))
        w_main = (wl, wl[:, off_sb:])
        pad_row = lambda v: jnp.pad(v.astype(F32), (N_HEADS, LANES - 2 * N_HEADS)).reshape(1, LANES)
        wts = (w_main, w_gate,
               conv_w[l], pad_row(a_log[l]), pad_row(dt_bias[l]),
               gdn_norm_w[l].astype(F32).reshape(1, HEAD_DIM),
               w_out[l, :WIDTH].astype(BF16), w_out[l, WIDTH:].astype(BF16),
               ln1_g[l].reshape(1, -1), ln1_b[l].reshape(1, -1),
               w_up[l].astype(BF16), w_down[l].astype(BF16),
               ln2_g[l].reshape(1, -1), ln2_b[l].reshape(1, -1))
        bp = yp.shape[0]
        zero_conv = jnp.zeros((bp, CONV_W - 1, 3 * WIDTH), F32)
        zero_s = jnp.zeros((bp, N_HEADS, HEAD_DIM, HEAD_DIM), F32)
        yp, c1, s1, k1, v1 = _layer(yp, zero_conv, zero_s, None, None, wts, alpha, "p%d" % l)
        ys, c2, s2, k2, v2 = _layer(ys, state_gdn_conv[l], state_gdn_S[l],
                                    cache_sb_k[l], cache_sb_v[l], wts, alpha, "s%d" % l)
        for lst, val in zip(outs, (c1, s1, k1, v1, c2, s2, k2, v2)):
            lst.append(val)
    return (yp, ys) + tuple(jnp.stack(o) for o in outs)
```
